```python
import math
import jax
import jax.numpy as jnp
from jax import lax
import numpy as np

D_MODEL = 1024
BATCH = 16
SEQ = 2048
DEPTH = 2

N_GROUPS = 4
GROUP_WIDTH = D_MODEL // N_GROUPS
HEADS = 4
HEAD_DIM = GROUP_WIDTH // HEADS
GDN_CONV = 4
GDN_CHUNK = 64
HGRN_CHUNK = 16
MOBA_BLOCK = 256
MOBA_TOPK = 3
MOBA_Q_CHUNK = 32
DIFF_DIM = HEAD_DIM // 2
ATTN_Q_BLOCK = 128
NORM_EPS = 1e-6
GDN_COLS = 4 * GROUP_WIDTH + 2 * HEADS
HGRN_COLS = 4 * GROUP_WIDTH
MOBA_COLS = 4 * GROUP_WIDTH
DIFF_COLS = 4 * GROUP_WIDTH
IN_COLS = GDN_COLS + HGRN_COLS + MOBA_COLS + DIFF_COLS

kernel_name = 'hybrid_parallel_heads_gdn_hgrn2_moba_diffattn'


def rmsnorm(x, g):
    xf = x.astype(jnp.float32)
    y = xf * lax.rsqrt(jnp.mean(xf * xf, axis=-1, keepdims=True) + NORM_EPS)
    return y * g.astype(jnp.float32)


def l2norm(x):
    return x * lax.rsqrt(jnp.sum(x * x, axis=-1, keepdims=True) + NORM_EPS)


def to_heads(t):
    B, T, _ = t.shape
    return t.reshape(B, T, HEADS, -1).transpose(0, 2, 1, 3)


def from_heads(t):
    B, H, T, d = t.shape
    return t.transpose(0, 2, 1, 3).reshape(B, T, H * d)


def causal_conv(x, w):
    K = w.shape[0]
    T = x.shape[1]
    xp = jnp.pad(x, ((0, 0), (K - 1, 0), (0, 0)))
    return sum(xp[:, j:j + T] * w[j] for j in range(K))


def gated_delta_rule(q, k, v, g, beta):
    B, H, T, dk = q.shape
    dv = v.shape[-1]
    C = GDN_CHUNK
    N = T // C

    def chunks(t):
        return t.reshape(B, H, N, C, *t.shape[3:])

    q = chunks(q * dk ** -0.5)
    k = chunks(k)
    v = chunks(v)
    beta = chunks(beta)
    g = jnp.cumsum(chunks(g), axis=-1)
    lower = jnp.tril(jnp.ones((C, C), dtype=bool))
    strict = jnp.tril(jnp.ones((C, C), dtype=bool), -1)
    decay = jnp.exp(jnp.where(lower, g[..., :, None] - g[..., None, :], -jnp.inf))
    k_beta = k * beta[..., None]
    a_kk = jnp.where(strict, jnp.einsum('bhncd,bhnsd->bhncs', k_beta, k) * decay, 0.0)
    eye = jnp.eye(C, dtype=q.dtype)
    t_inv = lax.linalg.triangular_solve(eye + a_kk, jnp.broadcast_to(eye, a_kk.shape),
                                        left_side=True, lower=True, unit_diagonal=True)
    u = t_inv @ (v * beta[..., None])
    w = t_inv @ (k_beta * jnp.exp(g)[..., None])
    a_qk = jnp.einsum('bhncd,bhnsd->bhncs', q, k) * decay
    q_dec = q * jnp.exp(g)[..., None]
    k_dec = k * jnp.exp(g[..., -1:] - g)[..., None]
    d_last = jnp.exp(g[..., -1])

    def step(S, inp):
        q_c, k_c, u_c, w_c, a_c, d_c = inp
        v_new = u_c - w_c @ S
        o = q_c @ S + a_c @ v_new
        S = S * d_c[..., None, None] + jnp.einsum('bhcd,bhce->bhde', k_c, v_new)
        return S, o

    xs = tuple(jnp.moveaxis(t, 2, 0) for t in (q_dec, k_dec, u, w, a_qk, d_last))
    _, o = lax.scan(step, jnp.zeros((B, H, dk, dv), q.dtype), xs)
    return jnp.moveaxis(o, 0, 2).reshape(B, H, T, dv)


def gdn_branch(p, conv_w, a_log, dt_bias, norm_g):
    W = GROUP_WIDTH
    qkv, a, b, z = jnp.split(p, [3 * W, 3 * W + HEADS, 3 * W + 2 * HEADS], axis=-1)
    qkv = jax.nn.silu(causal_conv(qkv, conv_w.astype(jnp.float32)))
    q, k, v = jnp.split(qkv, 3, axis=-1)
    q = l2norm(to_heads(q))
    k = l2norm(to_heads(k))
    v = to_heads(v)
    g = -jnp.exp(a_log.astype(jnp.float32)) * jax.nn.softplus(a + dt_bias.astype(jnp.float32))
    beta = jax.nn.sigmoid(b)
    o = gated_delta_rule(q, k, v, g.transpose(0, 2, 1), beta.transpose(0, 2, 1))
    return from_heads(rmsnorm(o, norm_g)) * jax.nn.silu(z)


def hgrn2_recurrence(q, k, v, log_f):
    B, H, T, dk = q.shape
    dv = v.shape[-1]
    C = HGRN_CHUNK
    N = T // C
    MID = C // 2

    def chunks(t):
        return t.reshape(B, H, N, C, t.shape[-1])

    q, k, v = chunks(q), chunks(k), chunks(v)
    G = jnp.cumsum(chunks(log_f), axis=3)
    G_ref = G[..., MID:MID + 1, :]
    causal = jnp.tril(jnp.ones((C, C), dtype=bool))
    a_qk = jnp.einsum('bhncd,bhnsd->bhncs', q * jnp.exp(G - G_ref), k * jnp.exp(G_ref - G))
    o_intra = jnp.where(causal, a_qk, 0.0) @ v
    q_dec = q * jnp.exp(G)
    k_dec = k * jnp.exp(G[..., -1:, :] - G)
    d_last = jnp.exp(G[..., -1, :])

    def step(S, inp):
        q_c, k_c, v_c, d_c, oi = inp
        o = q_c @ S + oi
        S = S * d_c[..., :, None] + jnp.einsum('bhcd,bhce->bhde', k_c, v_c)
        return S, o

    xs = tuple(jnp.moveaxis(t, 2, 0) for t in (q_dec, k_dec, v, d_last, o_intra))
    _, o = lax.scan(step, jnp.zeros((B, H, dk, dv), q.dtype), xs)
    return jnp.moveaxis(o, 0, 2).reshape(B, H, T, dv)


def hgrn2_branch(p, lb, norm_g):
    q, f, i, z = jnp.split(p, 4, axis=-1)
    log_f = jnp.logaddexp(jnp.log(lb), jnp.log1p(-lb) + jax.nn.log_sigmoid(f))
    k = (1.0 - lb) * jax.nn.sigmoid(-f)
    q = jax.nn.silu(q)
    o = hgrn2_recurrence(to_heads(q), to_heads(k), to_heads(i), to_heads(log_f))
    return from_heads(rmsnorm(o, norm_g)) * jax.nn.silu(z)


def moba_attention(q, k, v, slopes):
    B, H, T, d = q.shape
    Tp = -(-T // MOBA_BLOCK) * MOBA_BLOCK
    pad = ((0, 0), (0, 0), (0, Tp - T), (0, 0))
    q, k, v = jnp.pad(q, pad), jnp.pad(k, pad), jnp.pad(v, pad)
    nb = Tp // MOBA_BLOCK
    kk = min(MOBA_TOPK, nb)
    scale = d ** -0.5
    kb = k.reshape(B, H, nb, MOBA_BLOCK, d)
    vb = v.reshape(B, H, nb, MOBA_BLOCK, d)
    gate = jnp.einsum('bhtd,bhnd->bhtn', q, kb.mean(axis=3))
    past = jnp.arange(nb)[None, :] < (jnp.arange(Tp) // MOBA_BLOCK)[:, None]
    _, sel = lax.top_k(jnp.where(past, gate, -jnp.inf), kk)
    bi = jnp.arange(B)[:, None, None, None]
    hi = jnp.arange(H)[None, :, None, None]
    blk = jnp.arange(MOBA_BLOCK)

    def chunk_fn(c):
        t0 = c * MOBA_Q_CHUNK
        qc = lax.dynamic_slice_in_dim(q, t0, MOBA_Q_CHUNK, axis=2)
        sel_c = lax.dynamic_slice_in_dim(sel, t0, MOBA_Q_CHUNK, axis=2)
        pos_q = t0 + jnp.arange(MOBA_Q_CHUNK)
        own = t0 // MOBA_BLOCK
        k_sel = kb[bi, hi, sel_c]
        v_sel = vb[bi, hi, sel_c]
        s_sel = jnp.einsum('bhqd,bhqnsd->bhqns', qc, k_sel) * scale
        dist_sel = pos_q[None, None, :, None, None] - (sel_c[..., None] * MOBA_BLOCK + blk)
        s_sel = s_sel - slopes[None, :, None, None, None] * dist_sel
        s_sel = jnp.where((sel_c < own)[..., None], s_sel, -jnp.inf)
        k_own = lax.dynamic_slice_in_dim(kb, own, 1, axis=2)[:, :, 0]
        v_own = lax.dynamic_slice_in_dim(vb, own, 1, axis=2)[:, :, 0]
        dist_own = pos_q[:, None] - (own * MOBA_BLOCK + blk)[None, :]
        s_own = jnp.einsum('bhqd,bhsd->bhqs', qc, k_own) * scale - slopes[None, :, None, None] * dist_own
        s_own = jnp.where(dist_own >= 0, s_own, -jnp.inf)
        logits = jnp.concatenate([s_sel.reshape(B, H, MOBA_Q_CHUNK, kk * MOBA_BLOCK), s_own], axis=-1)
        pr = jax.nn.softmax(logits.astype(jnp.float32), axis=-1)
        p_sel = pr[..., :kk * MOBA_BLOCK].reshape(B, H, MOBA_Q_CHUNK, kk, MOBA_BLOCK)
        p_own = pr[..., kk * MOBA_BLOCK:]
        return (jnp.einsum('bhqns,bhqnsd->bhqd', p_sel, v_sel)
                + jnp.einsum('bhqs,bhsd->bhqd', p_own, v_own))

    outs = lax.map(chunk_fn, jnp.arange(Tp // MOBA_Q_CHUNK))
    return jnp.moveaxis(outs, 0, 2).reshape(B, H, Tp, d)[:, :, :T]


def moba_branch(p, slopes):
    q, k, v, z = jnp.split(p, 4, axis=-1)
    o = moba_attention(to_heads(q), to_heads(k), to_heads(v), slopes)
    return from_heads(o) * jax.nn.silu(z)


def differential_attention(q1, q2, k1, k2, v, lam, slopes):
    T, d = q1.shape[2], q1.shape[3]
    scale = d ** -0.5
    outs = []
    for s in range(0, T, ATTN_Q_BLOCK):
        e = s + ATTN_Q_BLOCK
        dist = jnp.arange(s, e)[:, None] - jnp.arange(e)[None, :]
        bias = jnp.where(dist >= 0, -slopes[:, None, None] * dist, -jnp.inf)
        a1 = jax.nn.softmax(jnp.einsum('bhqd,bhkd->bhqk', q1[:, :, s:e], k1[:, :, :e]) * scale + bias, axis=-1)
        a2 = jax.nn.softmax(jnp.einsum('bhqd,bhkd->bhqk', q2[:, :, s:e], k2[:, :, :e]) * scale + bias, axis=-1)
        outs.append(jnp.einsum('bhqk,bhkd->bhqd', a1 - lam * a2, v[:, :, :e]))
    return jnp.concatenate(outs, axis=2)


def diff_branch(p, lq1, lk1, lq2, lk2, norm_g, lam_init, slopes):
    B, T, _ = p.shape
    q, k, v, z = jnp.split(p, 4, axis=-1)
    q = q.reshape(B, T, HEADS, 2, DIFF_DIM).transpose(3, 0, 2, 1, 4)
    k = k.reshape(B, T, HEADS, 2, DIFF_DIM).transpose(3, 0, 2, 1, 4)
    f32 = jnp.float32
    lam = (jnp.exp(jnp.dot(lq1.astype(f32), lk1.astype(f32)))
           - jnp.exp(jnp.dot(lq2.astype(f32), lk2.astype(f32))) + lam_init)
    o = differential_attention(q[0], q[1], k[0], k[1], to_heads(v), lam, slopes)
    o = rmsnorm(o, norm_g) * (1.0 - lam_init)
    return from_heads(o) * jax.nn.silu(z)


def setup_inputs(seed: int = 0) -> dict:
    key = jax.random.key(seed)
    ks = jax.random.split(key, 16)
    f32 = jnp.float32

    def nrm(k, shape, s):
        return s * jax.random.normal(k, shape, f32)

    def gain(k, shape):
        return 1.0 + nrm(k, shape, 0.02)

    dt = jnp.exp(jax.random.uniform(ks[5], (DEPTH, HEADS), f32, math.log(1e-3), math.log(1e-1)))
    return {
        'x': nrm(ks[0], (BATCH, SEQ, D_MODEL), 1.0),
        'pre_norm_g': gain(ks[1], (DEPTH, D_MODEL)),
        'post_norm_g': gain(ks[2], (DEPTH, D_MODEL)),
        'w_in': nrm(ks[3], (DEPTH, D_MODEL, IN_COLS), D_MODEL ** -0.5),
        'conv_w': nrm(ks[4], (DEPTH, GDN_CONV, 3 * GROUP_WIDTH), GDN_CONV ** -0.5),
        'gdn_a_log': jnp.log(jax.random.uniform(ks[6], (DEPTH, HEADS), f32, 1.0, 16.0)),
        'gdn_dt_bias': dt + jnp.log(-jnp.expm1(-dt)),
        'gdn_norm_g': gain(ks[7], (DEPTH, HEAD_DIM)),
        'hgrn_lb': nrm(ks[8], (DEPTH, GROUP_WIDTH), 0.1),
        'hgrn_norm_g': gain(ks[9], (DEPTH, HEAD_DIM)),
        'diff_lq1': nrm(ks[10], (DEPTH, DIFF_DIM), 0.1),
        'diff_lk1': nrm(ks[11], (DEPTH, DIFF_DIM), 0.1),
        'diff_lq2': nrm(ks[12], (DEPTH, DIFF_DIM), 0.1),
        'diff_lk2': nrm(ks[13], (DEPTH, DIFF_DIM), 0.1),
        'diff_norm_g': gain(ks[14], (DEPTH, HEAD_DIM)),
        'w_out': nrm(ks[15], (DEPTH, D_MODEL, D_MODEL), D_MODEL ** -0.5),
    }


def reference(x, pre_norm_g, post_norm_g, w_in, conv_w, gdn_a_log, gdn_dt_bias, gdn_norm_g,
              hgrn_lb, hgrn_norm_g, diff_lq1, diff_lk1, diff_lq2, diff_lk2, diff_norm_g, w_out):
    f32 = jnp.float32
    alibi = 2.0 ** (-jnp.arange(1, 2 * HEADS + 1, dtype=f32))
    slopes_diff = alibi[0::2]
    slopes_moba = alibi[1::2]
    lb_all = jnp.cumsum(jax.nn.softmax(hgrn_lb.astype(f32), axis=0), axis=0)
    lb_all = lb_all - lb_all[0]
    splits = [GDN_COLS, GDN_COLS + HGRN_COLS, GDN_COLS + HGRN_COLS + MOBA_COLS]
    for l in range(DEPTH):
        h = rmsnorm(x, pre_norm_g[l])
        proj = jnp.einsum('btd,dc->btc', h, w_in[l].astype(f32))
        p_gdn, p_hgrn, p_moba, p_diff = jnp.split(proj, splits, axis=-1)
        lam_init = 0.8 - 0.6 * math.exp(-0.3 * l)
        y = jnp.concatenate([
            gdn_branch(p_gdn, conv_w[l], gdn_a_log[l], gdn_dt_bias[l], gdn_norm_g[l]),
            hgrn2_branch(p_hgrn, lb_all[l], hgrn_norm_g[l]),
            moba_branch(p_moba, slopes_moba),
            diff_branch(p_diff, diff_lq1[l], diff_lk1[l], diff_lq2[l], diff_lk2[l], diff_norm_g[l],
                        lam_init, slopes_diff),
        ], axis=-1)
        y = jnp.einsum('btc,cd->btd', y, w_out[l].astype(f32))
        x = x + rmsnorm(y, post_norm_g[l]).astype(x.dtype)
    return x
```

```python
import functools
import math

import jax
import jax.numpy as jnp
from jax import lax
from jax.experimental import pallas as pl
from jax.experimental.pallas import tpu as pltpu

F32 = jnp.float32
BF16 = jnp.bfloat16

HEADS = 4
HEAD_DIM = 64
GROUP_WIDTH = HEADS * HEAD_DIM
HEAD_SHIFT = 6
GDN_CONV = 4
GDN_CHUNK = 64
HGRN_CHUNK = 16
HGRN_CHUNK_SHIFT = 4
HGRN_TILE = 128
MOBA_BLOCK = 256
MOBA_TOPK = 3
DIFF_DIM = HEAD_DIM // 2
ATTN_TILE = 256
NORM_EPS = 1e-6
NEG_INF = float("-inf")

GDN_COLS = 4 * GROUP_WIDTH + 2 * HEADS
SEG_WIDTHS = (4 * GROUP_WIDTH, 2 * GROUP_WIDTH, 4 * GROUP_WIDTH, 4 * GROUP_WIDTH, 4 * GROUP_WIDTH)
VMEM_LIMIT = 56 * 1024 * 1024


def _iota(shape, dim):
    return lax.broadcasted_iota(jnp.int32, shape, dim)


def _dot(a, b):
    return jnp.dot(a, b, preferred_element_type=F32)


def _dot_nt(a, b):
    return lax.dot_general(a, b, (((1,), (1,)), ((), ())), preferred_element_type=F32)


def _dot_tn(a, b):
    return lax.dot_general(a, b, (((0,), (0,)), ((), ())), preferred_element_type=F32)


def _split(x, n):
    parts = []
    r = x
    for i in range(n):
        p = r.astype(BF16)
        parts.append(p)
        if i + 1 < n:
            r = r - p.astype(F32)
    return parts


def _sigmoid(x):
    return 1.0 / (1.0 + jnp.exp(-x))


def _silu(x):
    return x * _sigmoid(x)


def _softplus(x):
    return jnp.maximum(x, 0.0) + jnp.log1p(jnp.exp(-jnp.abs(x)))


def _head_block_mask(rows, cols):
    return (_iota((rows, cols), 0) >> HEAD_SHIFT) == (_iota((rows, cols), 1) >> HEAD_SHIFT)


def _head_sum(x, ones_bd):
    hi, lo = _split(x, 2)
    return _dot(hi, ones_bd) + _dot(lo, ones_bd)


def _inproj_body(x_ref, g_ref, w_ref, *o_refs):
    x = x_ref[...]
    ms = jnp.mean(x * x, axis=-1, keepdims=True)
    h = (x * lax.rsqrt(ms + NORM_EPS) * g_ref[...]).astype(BF16)
    off = 0
    for o in o_refs:
        n = o.shape[-1]
        o[...] = _dot(h, w_ref[:, off:off + n]).astype(o.dtype)
        off += n


def _inproj(x2, g, w, tm=256):
    m, d = x2.shape
    n = w.shape[1]
    assert n == sum(SEG_WIDTHS) and m % tm == 0
    return pl.pallas_call(
        _inproj_body,
        out_shape=[jax.ShapeDtypeStruct((m, s), F32) for s in SEG_WIDTHS],
        grid=(m // tm,),
        in_specs=[pl.BlockSpec((tm, d), lambda i: (i, 0)),
                  pl.BlockSpec((1, d), lambda i: (0, 0)),
                  pl.BlockSpec((d, n), lambda i: (0, 0))],
        out_specs=[pl.BlockSpec((tm, s), lambda i: (i, 0)) for s in SEG_WIDTHS],
        compiler_params=pltpu.CompilerParams(dimension_semantics=("arbitrary",),
                                             vmem_limit_bytes=VMEM_LIMIT),
        name="inproj",
    )(x2, g, w)


def _reorder_w_in(w):
    gw = GROUP_WIDTH
    q, k, v = w[:, 0:gw], w[:, gw:2 * gw], w[:, 2 * gw:3 * gw]
    a = w[:, 3 * gw:3 * gw + HEADS]
    b = w[:, 3 * gw + HEADS:3 * gw + 2 * HEADS]
    z = w[:, 3 * gw + 2 * HEADS:GDN_COLS]
    a_exp = jnp.repeat(a, HEAD_DIM, axis=1)
    b_exp = jnp.repeat(b, HEAD_DIM, axis=1)
    return jnp.concatenate([q, k, v, z, a_exp, b_exp, w[:, GDN_COLS:]], axis=1).astype(BF16)


def _outproj_body(o1, o2, o3, o4, x_ref, w_ref, g_ref, out_ref):
    gw = GROUP_WIDTH
    y = _dot(o1[...], w_ref[0:gw, :])
    y = y + _dot(o2[...], w_ref[gw:2 * gw, :])
    y = y + _dot(o3[...], w_ref[2 * gw:3 * gw, :])
    y = y + _dot(o4[...], w_ref[3 * gw:4 * gw, :])
    ms = jnp.mean(y * y, axis=-1, keepdims=True)
    out_ref[...] = x_ref[...] + y * lax.rsqrt(ms + NORM_EPS) * g_ref[...]


def _outproj(os_, x2, w, g, tm=512):
    m, d = x2.shape
    gw = GROUP_WIDTH
    return pl.pallas_call(
        _outproj_body,
        out_shape=jax.ShapeDtypeStruct((m, d), F32),
        grid=(m // tm,),
        in_specs=[pl.BlockSpec((tm, gw), lambda i: (i, 0))] * 4 + [
            pl.BlockSpec((tm, d), lambda i: (i, 0)),
            pl.BlockSpec((d, d), lambda i: (0, 0)),
            pl.BlockSpec((1, d), lambda i: (0, 0))],
        out_specs=pl.BlockSpec((tm, d), lambda i: (i, 0)),
        compiler_params=pltpu.CompilerParams(dimension_semantics=("arbitrary",),
                                             vmem_limit_bytes=VMEM_LIMIT),
        name="outproj",
    )(*os_, x2, w, g)


def _gdn_body(p_ref, ab_ref, cw_ref, alog_ref, dtb_ref, ng_ref, o_ref,
              qdec_s, kdec_s, w_s, u_s, aqk_s, dl_s, state_s, oraw_s):
    gw = GROUP_WIDTH
    t_len = p_ref.shape[1]
    c_len = GDN_CHUNK
    n_chunks = t_len // c_len

    cw = cw_ref[...]
    a_neg = -jnp.exp(alog_ref[...])
    dtb = dtb_ref[...]
    row = _iota((c_len, gw), 0)
    s_idx = _iota((c_len, gw), 1) & (HEAD_DIM - 1)
    lower = s_idx <= row
    strict = s_idx < row
    eye = s_idx == row
    bd = _head_block_mask(gw, gw)
    ones_bd = jnp.where(bd, 1.0, 0.0).astype(BF16)
    tril = jnp.where(_iota((c_len, c_len), 1) <= _iota((c_len, c_len), 0), 1.0, 0.0).astype(BF16)
    row8 = _iota((8, 3 * gw), 0)

    def block_diag(x):
        return jnp.where(bd, jnp.concatenate([x] * HEADS, axis=0), jnp.zeros((), x.dtype))

    def mm_pairs(left, right):
        l0, l1 = _split(left, 2)
        r0, r1 = _split(right, 2)
        rb0, rb1 = block_diag(r0), block_diag(r1)
        return _dot(l0, rb0) + (_dot(l0, rb1) + _dot(l1, rb0))

    def pre(c, tail):
        r0 = pl.multiple_of(c * c_len, c_len)
        x = p_ref[0, pl.ds(r0, c_len), 0:3 * gw]
        acc = x * cw[GDN_CONV - 1:GDN_CONV, :]
        for j in range(GDN_CONV - 1):
            shift = GDN_CONV - 1 - j
            rolled = pltpu.roll(x, shift, axis=0)
            tail_r = pltpu.roll(tail, shift, axis=0)
            head_rows = jnp.where(row8 < shift, tail_r, rolled[0:8])
            shifted = jnp.concatenate([head_rows, rolled[8:]], axis=0)
            acc = acc + shifted * cw[j:j + 1, :]
        y = _silu(acc)
        q = y[:, 0:gw]
        k = y[:, gw:2 * gw]
        v = y[:, 2 * gw:3 * gw]
        q = q * lax.rsqrt(_head_sum(q * q, ones_bd) + NORM_EPS) * (HEAD_DIM ** -0.5)
        k = k * lax.rsqrt(_head_sum(k * k, ones_bd) + NORM_EPS)

        a = ab_ref[0, pl.ds(r0, c_len), 0:gw]
        b = ab_ref[0, pl.ds(r0, c_len), gw:2 * gw]
        g = a_neg * _softplus(a + dtb)
        beta = _sigmoid(b)
        g3 = _split(g, 3)
        gc = _dot(tril, g3[0]) + _dot(tril, g3[1]) + _dot(tril, g3[2])
        g_row = jnp.sum(jnp.where(eye, gc, 0.0), axis=0, keepdims=True)
        decay = jnp.exp(jnp.where(lower, gc - g_row, NEG_INF))
        g_last = gc[c_len - 1:c_len, :]
        eg = jnp.exp(gc)

        kb = k * beta
        k_bd = block_diag(k.astype(BF16))
        sc = _dot_nt(jnp.concatenate([kb, q], axis=0).astype(BF16), k_bd)
        a_kk = jnp.where(strict, sc[0:c_len] * decay, 0.0)
        a_qk = sc[c_len:2 * c_len] * decay

        m = -a_kk
        t_inv = jnp.where(eye, 1.0, 0.0) + m
        for _ in range(5):
            m = mm_pairs(m, m)
            t_inv = t_inv + mm_pairs(m, t_inv)
        t_b = t_inv.astype(BF16)
        u = _dot(t_b, block_diag((v * beta).astype(BF16)))
        w = _dot(t_b, block_diag((kb * eg).astype(BF16)))

        rows = pl.ds(r0, c_len)
        qdec_s[rows, :] = (q * eg).astype(BF16)
        kdec_s[rows, :] = (k * jnp.exp(g_last - gc)).astype(BF16)
        w_s[rows, :] = w.astype(BF16)
        u_s[rows, :] = u
        aqk_s[rows, :] = a_qk.astype(BF16)
        dl_s[pl.ds(c, 1), :] = jnp.exp(g_last)
        return x[c_len - 8:c_len, :]

    lax.fori_loop(0, n_chunks, pre, jnp.zeros((8, 3 * gw), F32))

    state_s[...] = jnp.zeros_like(state_s)

    def seq(c, carry):
        r0 = pl.multiple_of(c * c_len, c_len)
        rows = pl.ds(r0, c_len)
        state = state_s[...]
        ws = _dot(jnp.concatenate([w_s[rows, :], qdec_s[rows, :]], axis=0), state.astype(BF16))
        v_new = (u_s[rows, :] - ws[0:c_len]).astype(BF16)
        oraw_s[rows, :] = ws[c_len:2 * c_len] + _dot(aqk_s[rows, :], block_diag(v_new))
        kv = _dot_tn(kdec_s[rows, :], v_new)
        state_s[...] = state * dl_s[pl.ds(c, 1), :] + jnp.where(bd, kv, 0.0)
        return carry

    lax.fori_loop(0, n_chunks, seq, 0)

    def post(t, carry):
        r0 = pl.multiple_of(t * HGRN_TILE, HGRN_TILE)
        rows = pl.ds(r0, HGRN_TILE)
        o = oraw_s[rows, :]
        z = p_ref[0, rows, 3 * gw:4 * gw]
        ms = _head_sum(o * o, ones_bd) * (1.0 / HEAD_DIM)
        o_ref[0, rows, :] = (o * lax.rsqrt(ms + NORM_EPS) * ng_ref[...] * _silu(z)).astype(o_ref.dtype)
        return carry

    lax.fori_loop(0, t_len // HGRN_TILE, post, 0)


def _gdn(p, ab, cw, alog_exp, dtb_exp, ng_exp):
    b, t, _ = p.shape
    gw = GROUP_WIDTH
    full = lambda shape: pl.BlockSpec(shape, lambda i: (0,) * len(shape))
    return pl.pallas_call(
        _gdn_body,
        out_shape=jax.ShapeDtypeStruct((b, t, gw), BF16),
        grid=(b,),
        in_specs=[pl.BlockSpec((1, t, 4 * gw), lambda i: (i, 0, 0)),
                  pl.BlockSpec((1, t, 2 * gw), lambda i: (i, 0, 0)),
                  full((GDN_CONV, 3 * gw)), full((1, gw)), full((1, gw)), full((1, gw))],
        out_specs=pl.BlockSpec((1, t, gw), lambda i: (i, 0, 0)),
        scratch_shapes=[pltpu.VMEM((t, gw), BF16), pltpu.VMEM((t, gw), BF16), pltpu.VMEM((t, gw), BF16),
                        pltpu.VMEM((t, gw), F32), pltpu.VMEM((t, gw), BF16),
                        pltpu.VMEM((t // GDN_CHUNK, gw), F32), pltpu.VMEM((gw, gw), F32),
                        pltpu.VMEM((t, gw), F32)],
        compiler_params=pltpu.CompilerParams(dimension_semantics=("arbitrary",),
                                             vmem_limit_bytes=VMEM_LIMIT),
        name="gdn",
    )(p, ab, cw, alog_exp, dtb_exp, ng_exp)


def _hgrn_body(layer, p_ref, lbp_ref, ng_ref, o_ref, qdec_s, kdec_s, oi_s, dl_s, state_s, oraw_s):
    gw = GROUP_WIDTH
    t_len = p_ref.shape[1]
    c_len = HGRN_CHUNK
    tile = HGRN_TILE
    chunks_per_tile = tile // c_len

    lbp = lbp_ref[...]
    e = jnp.exp(lbp - jnp.max(lbp, axis=0, keepdims=True))
    sm = e / jnp.sum(e, axis=0, keepdims=True)
    if layer == 0:
        lb = jnp.zeros((1, gw), F32)
    else:
        lb = jnp.sum(sm[1:layer + 1], axis=0, keepdims=True)
    log_lb = jnp.log(lb)
    log_1m = jnp.log1p(-lb)

    r = _iota((tile, tile), 0)
    c = _iota((tile, tile), 1)
    same = (r >> HGRN_CHUNK_SHIFT) == (c >> HGRN_CHUNK_SHIFT)
    mid = ((r >> HGRN_CHUNK_SHIFT) << HGRN_CHUNK_SHIFT) + c_len // 2
    causal = same & (c <= r)
    m_cum = jnp.where(causal, 1.0, 0.0)
    m_mid = (jnp.where(same & (c > mid) & (c <= r), 1.0, 0.0)
             - jnp.where(same & (c > r) & (c <= mid), 1.0, 0.0))
    m_rest = jnp.where(same & (c > r), 1.0, 0.0)
    m_tot = jnp.where((_iota((chunks_per_tile, tile), 1) >> HGRN_CHUNK_SHIFT)
                      == _iota((chunks_per_tile, tile), 0), 1.0, 0.0)
    mats = jnp.concatenate([m_cum, m_mid, m_rest, m_tot], axis=0).astype(BF16)
    lane_head = _iota((1, gw), 1) >> HEAD_SHIFT
    bd = _head_block_mask(gw, gw)
    ones_bd = jnp.where(bd, 1.0, 0.0).astype(BF16)

    def pre(t, carry):
        r0 = pl.multiple_of(t * tile, tile)
        rows = pl.ds(r0, tile)
        q = p_ref[0, rows, 0:gw]
        f = p_ref[0, rows, gw:2 * gw]
        v = p_ref[0, rows, 2 * gw:3 * gw]
        log_sig = jnp.minimum(f, 0.0) - jnp.log1p(jnp.exp(-jnp.abs(f)))
        b_ = log_1m + log_sig
        log_f = jnp.maximum(log_lb, b_) + jnp.log1p(jnp.exp(-jnp.abs(log_lb - b_)))
        k = (1.0 - lb) * _sigmoid(-f)
        qd = _silu(q)
        f3 = _split(log_f, 3)
        g_all = _dot(mats, f3[0]) + _dot(mats, f3[1]) + _dot(mats, f3[2])
        g_cum = g_all[0:tile]
        g_mid = g_all[tile:2 * tile]
        g_rest = g_all[2 * tile:3 * tile]
        g_tot = g_all[3 * tile:3 * tile + chunks_per_tile]
        qa = qd * jnp.exp(g_mid)
        ka = (k * jnp.exp(-g_mid)).astype(BF16)
        qdec_s[rows, :] = (qd * jnp.exp(g_cum)).astype(BF16)
        kdec_s[rows, :] = (k * jnp.exp(g_rest)).astype(BF16)
        dl_s[pl.ds(pl.multiple_of(t * chunks_per_tile, chunks_per_tile), chunks_per_tile), :] = jnp.exp(g_tot)
        vb = v.astype(BF16)
        oi = jnp.zeros((tile, gw), F32)
        for h in range(HEADS):
            hm = lane_head == h
            a = _dot_nt(jnp.where(hm, qa, 0.0).astype(BF16), ka)
            a = jnp.where(causal, a, 0.0).astype(BF16)
            oi = oi + jnp.where(hm, _dot(a, vb), 0.0)
        oi_s[rows, :] = oi
        return carry

    lax.fori_loop(0, t_len // tile, pre, 0)

    state_s[...] = jnp.zeros_like(state_s)

    def seq(ci, carry):
        r0 = pl.multiple_of(ci * c_len, c_len)
        rows = pl.ds(r0, c_len)
        state_t = state_s[...]
        oraw_s[rows, :] = _dot_nt(qdec_s[rows, :], state_t.astype(BF16)) + oi_s[rows, :]
        vb = p_ref[0, rows, 2 * gw:3 * gw].astype(BF16)
        kv_t = _dot_tn(vb, kdec_s[rows, :])
        state_s[...] = state_t * dl_s[pl.ds(ci, 1), :] + jnp.where(bd, kv_t, 0.0)
        return carry

    lax.fori_loop(0, t_len // c_len, seq, 0)

    def post(t, carry):
        r0 = pl.multiple_of(t * tile, tile)
        rows = pl.ds(r0, tile)
        o = oraw_s[rows, :]
        z = p_ref[0, rows, 3 * gw:4 * gw]
        ms = _head_sum(o * o, ones_bd) * (1.0 / HEAD_DIM)
        o_ref[0, rows, :] = (o * lax.rsqrt(ms + NORM_EPS) * ng_ref[...] * _silu(z)).astype(o_ref.dtype)
        return carry

    lax.fori_loop(0, t_len // tile, post, 0)


def _hgrn(layer, p, lbp, ng_exp):
    b, t, _ = p.shape
    gw = GROUP_WIDTH
    depth = lbp.shape[0]
    full = lambda shape: pl.BlockSpec(shape, lambda i: (0,) * len(shape))
    return pl.pallas_call(
        functools.partial(_hgrn_body, layer),
        out_shape=jax.ShapeDtypeStruct((b, t, gw), BF16),
        grid=(b,),
        in_specs=[pl.BlockSpec((1, t, 4 * gw), lambda i: (i, 0, 0)), full((depth, gw)), full((1, gw))],
        out_specs=pl.BlockSpec((1, t, gw), lambda i: (i, 0, 0)),
        scratch_shapes=[pltpu.VMEM((t, gw), BF16), pltpu.VMEM((t, gw), BF16), pltpu.VMEM((t, gw), F32),
                        pltpu.VMEM((t // HGRN_CHUNK, gw), F32), pltpu.VMEM((gw, gw), F32),
                        pltpu.VMEM((t, gw), F32)],
        compiler_params=pltpu.CompilerParams(dimension_semantics=("arbitrary",),
                                             vmem_limit_bytes=VMEM_LIMIT),
        name="hgrn2",
    )(p, lbp, ng_exp)


def _cast_kv(p_ref, kb_s, vb_s):
    gw = GROUP_WIDTH
    t_len = p_ref.shape[1]

    def body(n, carry):
        rows = pl.ds(pl.multiple_of(n * ATTN_TILE, ATTN_TILE), ATTN_TILE)
        kb_s[rows, :] = p_ref[0, rows, gw:2 * gw].astype(BF16)
        vb_s[rows, :] = p_ref[0, rows, 2 * gw:3 * gw].astype(BF16)
        return carry

    lax.fori_loop(0, t_len // ATTN_TILE, body, 0)


def _flash(qh, kb_s, vb_s, i, slope, sel_fn):
    tile = ATTN_TILE
    c_loc = _iota((1, tile), 1)
    causal = _iota((tile, tile), 1) <= _iota((tile, tile), 0)

    rows_q = pl.ds(pl.multiple_of(i * tile, tile), tile)
    t = _dot_nt(qh, kb_s[rows_q, :]) + slope * c_loc.astype(F32)
    t = jnp.where(causal, t, NEG_INF)
    m = jnp.max(t, axis=-1, keepdims=True)
    p = jnp.exp(t - m)
    l = jnp.sum(p, axis=-1, keepdims=True)
    acc = _dot(p.astype(BF16), vb_s[rows_q, :])

    def past(j, carry):
        m, l, acc = carry
        rows_k = pl.ds(pl.multiple_of(j * tile, tile), tile)
        bias = slope * (c_loc - (i - j) * tile).astype(F32)
        t = _dot_nt(qh, kb_s[rows_k, :]) + bias
        sel = sel_fn(j)
        if sel is not None:
            t = jnp.where(sel > 0.5, t, NEG_INF)
        m_new = jnp.maximum(m, jnp.max(t, axis=-1, keepdims=True))
        alpha = jnp.exp(m - m_new)
        p = jnp.exp(t - m_new)
        l = alpha * l + jnp.sum(p, axis=-1, keepdims=True)
        acc = alpha * acc + _dot(p.astype(BF16), vb_s[rows_k, :])
        return m_new, l, acc

    m, l, acc = lax.fori_loop(0, i, past, (m, l, acc))
    return acc, l


def _moba_body(p_ref, o_ref, kb_s, vb_s, km_s):
    gw = GROUP_WIDTH
    t_len = p_ref.shape[1]
    tile = ATTN_TILE
    nb = t_len // MOBA_BLOCK
    topk = min(MOBA_TOPK, nb)

    _cast_kv(p_ref, kb_s, vb_s)

    def means(n, carry):
        rows = pl.ds(pl.multiple_of(n * MOBA_BLOCK, MOBA_BLOCK), MOBA_BLOCK)
        km_s[pl.ds(n, 1), :] = jnp.sum(p_ref[0, rows, gw:2 * gw], axis=0, keepdims=True) * (1.0 / MOBA_BLOCK)
        return carry

    lax.fori_loop(0, nb, means, 0)
    km3 = _split(km_s[...], 3)

    lane_head = _iota((1, gw), 1) >> HEAD_SHIFT
    n_idx = _iota((tile, nb), 1)

    def qtile(i, carry):
        rows_q = pl.ds(pl.multiple_of(i * tile, tile), tile)
        q = p_ref[0, rows_q, 0:gw]
        out = jnp.zeros((tile, gw), F32)
        for h in range(HEADS):
            hm = lane_head == h
            qm = jnp.where(hm, q, 0.0)
            q3 = _split(qm, 3)
            gate = (_dot_nt(q3[0], km3[0]) + _dot_nt(q3[0], km3[1]) + _dot_nt(q3[1], km3[0])
                    + _dot_nt(q3[1], km3[1]) + _dot_nt(q3[0], km3[2]) + _dot_nt(q3[2], km3[0]))
            past_blk = n_idx < i
            gm = jnp.where(past_blk, gate, NEG_INF)
            cnt = jnp.zeros((tile, nb), F32)
            for n2 in range(nb):
                col = gm[:, n2:n2 + 1]
                beats = (col > gm) | ((col == gm) & (n2 < n_idx))
                cnt = cnt + jnp.where(beats, 1.0, 0.0)
            sel_f = jnp.where(past_blk & (cnt < topk), 1.0, 0.0)

            def sel_fn(j, sel_f=sel_f):
                return jnp.sum(jnp.where(n_idx == j, sel_f, 0.0), axis=-1, keepdims=True)

            qh = (qm * (HEAD_DIM ** -0.5)).astype(BF16)
            acc, l = _flash(qh, kb_s, vb_s, i, 2.0 ** -(2 * h + 2), sel_fn)
            out = out + jnp.where(hm, acc * (1.0 / l), 0.0)
        z = p_ref[0, rows_q, 3 * gw:4 * gw]
        o_ref[0, rows_q, :] = (out * _silu(z)).astype(o_ref.dtype)
        return carry

    lax.fori_loop(0, t_len // tile, qtile, 0)


def _moba(p):
    b, t, _ = p.shape
    gw = GROUP_WIDTH
    assert t % MOBA_BLOCK == 0 and MOBA_BLOCK == ATTN_TILE
    return pl.pallas_call(
        _moba_body,
        out_shape=jax.ShapeDtypeStruct((b, t, gw), BF16),
        grid=(b,),
        in_specs=[pl.BlockSpec((1, t, 4 * gw), lambda i: (i, 0, 0))],
        out_specs=pl.BlockSpec((1, t, gw), lambda i: (i, 0, 0)),
        scratch_shapes=[pltpu.VMEM((t, gw), BF16), pltpu.VMEM((t, gw), BF16),
                        pltpu.VMEM((t // MOBA_BLOCK, gw), F32)],
        compiler_params=pltpu.CompilerParams(dimension_semantics=("arbitrary",),
                                             vmem_limit_bytes=VMEM_LIMIT),
        name="moba",
    )(p)


def _diff_body(layer, p_ref, lq1_ref, lk1_ref, lq2_ref, lk2_ref, ng_ref, o_ref, kb_s, vb_s):
    gw = GROUP_WIDTH
    t_len = p_ref.shape[1]
    tile = ATTN_TILE
    lam_init = 0.8 - 0.6 * math.exp(-0.3 * layer)
    lam = (jnp.exp(jnp.sum(lq1_ref[...] * lk1_ref[...], axis=-1, keepdims=True))
           - jnp.exp(jnp.sum(lq2_ref[...] * lk2_ref[...], axis=-1, keepdims=True)) + lam_init)

    _cast_kv(p_ref, kb_s, vb_s)
    lane_head = _iota((1, gw), 1) >> HEAD_SHIFT
    lane_map = _iota((1, gw), 1) >> (HEAD_SHIFT - 1)

    def qtile(i, carry):
        rows_q = pl.ds(pl.multiple_of(i * tile, tile), tile)
        q = p_ref[0, rows_q, 0:gw] * (DIFF_DIM ** -0.5)
        out = jnp.zeros((tile, gw), F32)
        for h in range(HEADS):
            slope = 2.0 ** -(2 * h + 1)
            o_maps = []
            for mp in range(2):
                qh = jnp.where(lane_map == 2 * h + mp, q, 0.0).astype(BF16)
                acc, l = _flash(qh, kb_s, vb_s, i, slope, lambda j: None)
                o_maps.append(acc * (1.0 / l))
            hm = lane_head == h
            o_h = jnp.where(hm, o_maps[0] - lam * o_maps[1], 0.0)
            ms = jnp.sum(o_h * o_h, axis=-1, keepdims=True) * (1.0 / HEAD_DIM)
            out = out + o_h * lax.rsqrt(ms + NORM_EPS)
        z = p_ref[0, rows_q, 3 * gw:4 * gw]
        o_ref[0, rows_q, :] = (out * ng_ref[...] * (1.0 - lam_init) * _silu(z)).astype(o_ref.dtype)
        return carry

    lax.fori_loop(0, t_len // tile, qtile, 0)


def _diff(layer, p, lq1, lk1, lq2, lk2, ng_exp):
    b, t, _ = p.shape
    gw = GROUP_WIDTH
    assert t % ATTN_TILE == 0
    full = lambda shape: pl.BlockSpec(shape, lambda i: (0,) * len(shape))
    return pl.pallas_call(
        functools.partial(_diff_body, layer),
        out_shape=jax.ShapeDtypeStruct((b, t, gw), BF16),
        grid=(b,),
        in_specs=[pl.BlockSpec((1, t, 4 * gw), lambda i: (i, 0, 0))] + [full((1, DIFF_DIM))] * 4 + [full((1, gw))],
        out_specs=pl.BlockSpec((1, t, gw), lambda i: (i, 0, 0)),
        scratch_shapes=[pltpu.VMEM((t, gw), BF16), pltpu.VMEM((t, gw), BF16)],
        compiler_params=pltpu.CompilerParams(dimension_semantics=("arbitrary",),
                                             vmem_limit_bytes=VMEM_LIMIT),
        name="diffattn",
    )(p, lq1, lk1, lq2, lk2, ng_exp)


def kernel(x, pre_norm_g, post_norm_g, w_in, conv_w, gdn_a_log, gdn_dt_bias, gdn_norm_g,
           hgrn_lb, hgrn_norm_g, diff_lq1, diff_lk1, diff_lq2, diff_lk2, diff_norm_g, w_out):
    b, t, d = x.shape
    depth = w_in.shape[0]
    gw = GROUP_WIDTH
    x2 = x.astype(F32).reshape(b * t, d)
    row = lambda a: a.astype(F32).reshape(1, -1)
    per_head = lambda a: jnp.repeat(a.astype(F32), HEAD_DIM).reshape(1, gw)
    per_dim = lambda a: jnp.tile(a.astype(F32), HEADS).reshape(1, gw)
    lbp = hgrn_lb.astype(F32)
    for l in range(depth):
        p_gdn, p_ab, p_hgrn, p_moba, p_diff = _inproj(x2, row(pre_norm_g[l]), _reorder_w_in(w_in[l]))
        shp = lambda a: a.reshape(b, t, a.shape[-1])
        o_gdn = _gdn(shp(p_gdn), shp(p_ab), conv_w[l].astype(F32), per_head(gdn_a_log[l]),
                     per_head(gdn_dt_bias[l]), per_dim(gdn_norm_g[l]))
        o_hgrn = _hgrn(l, shp(p_hgrn), lbp, per_dim(hgrn_norm_g[l]))
        o_moba = _moba(shp(p_moba))
        o_diff = _diff(l, shp(p_diff), row(diff_lq1[l]), row(diff_lk1[l]), row(diff_lq2[l]),
                       row(diff_lk2[l]), per_dim(diff_norm_g[l]))
        os_ = [o.reshape(b * t, gw) for o in (o_gdn, o_hgrn, o_moba, o_diff)]
        x2 = _outproj(os_, x2, w_out[l].astype(BF16), row(post_norm_g[l]))
    return x2.reshape(b, t, d)
```

```python
import functools
import math

import jax
import jax.numpy as jnp
from jax import lax
from jax.experimental import pallas as pl
from jax.experimental.pallas import tpu as pltpu

F32 = jnp.float32
BF16 = jnp.bfloat16

HEADS = 4
HEAD_DIM = 64
GROUP_WIDTH = HEADS * HEAD_DIM
HEAD_SHIFT = 6
GDN_CONV = 4
GDN_CHUNK = 64
HGRN_CHUNK = 16
HGRN_CHUNK_SHIFT = 4
HGRN_TILE = 128
MOBA_BLOCK = 256
MOBA_TOPK = 3
DIFF_DIM = HEAD_DIM // 2
ATTN_TILE = 256
NORM_EPS = 1e-6
NEG_INF = float("-inf")

GDN_COLS = 4 * GROUP_WIDTH + 2 * HEADS
SEG_WIDTHS = (4 * GROUP_WIDTH, 2 * GROUP_WIDTH, 4 * GROUP_WIDTH, 4 * GROUP_WIDTH, 4 * GROUP_WIDTH)
VMEM_LIMIT = 56 * 1024 * 1024


def _iota(shape, dim):
    return lax.broadcasted_iota(jnp.int32, shape, dim)


def _dot(a, b):
    return jnp.dot(a, b, preferred_element_type=F32)


def _dot_nt(a, b):
    return lax.dot_general(a, b, (((1,), (1,)), ((), ())), preferred_element_type=F32)


def _dot_tn(a, b):
    return lax.dot_general(a, b, (((0,), (0,)), ((), ())), preferred_element_type=F32)


def _split(x, n):
    parts = []
    r = x
    for i in range(n):
        p = r.astype(BF16)
        parts.append(p)
        if i + 1 < n:
            r = r - p.astype(F32)
    return parts


def _sigmoid(x):
    return 1.0 / (1.0 + jnp.exp(-x))


def _silu(x):
    return x * _sigmoid(x)


def _softplus(x):
    return jnp.maximum(x, 0.0) + jnp.log1p(jnp.exp(-jnp.abs(x)))


def _head_block_mask(rows, cols):
    return (_iota((rows, cols), 0) >> HEAD_SHIFT) == (_iota((rows, cols), 1) >> HEAD_SHIFT)


def _head_sum(x, ones_bd):
    hi, lo = _split(x, 2)
    return _dot(hi, ones_bd) + _dot(lo, ones_bd)


def _inproj_body(x_ref, g_ref, w_ref, *o_refs):
    x = x_ref[...]
    ms = jnp.mean(x * x, axis=-1, keepdims=True)
    h = (x * lax.rsqrt(ms + NORM_EPS) * g_ref[...]).astype(BF16)
    off = 0
    for o in o_refs:
        n = o.shape[-1]
        o[...] = _dot(h, w_ref[:, off:off + n]).astype(o.dtype)
        off += n


def _inproj(x2, g, w, tm=256):
    m, d = x2.shape
    n = w.shape[1]
    assert n == sum(SEG_WIDTHS) and m % tm == 0
    return pl.pallas_call(
        _inproj_body,
        out_shape=[jax.ShapeDtypeStruct((m, s), F32) for s in SEG_WIDTHS],
        grid=(m // tm,),
        in_specs=[pl.BlockSpec((tm, d), lambda i: (i, 0)),
                  pl.BlockSpec((1, d), lambda i: (0, 0)),
                  pl.BlockSpec((d, n), lambda i: (0, 0))],
        out_specs=[pl.BlockSpec((tm, s), lambda i: (i, 0)) for s in SEG_WIDTHS],
        compiler_params=pltpu.CompilerParams(dimension_semantics=("arbitrary",),
                                             vmem_limit_bytes=VMEM_LIMIT),
        name="inproj",
    )(x2, g, w)


def _reorder_w_in(w):
    gw = GROUP_WIDTH
    q, k, v = w[:, 0:gw], w[:, gw:2 * gw], w[:, 2 * gw:3 * gw]
    a = w[:, 3 * gw:3 * gw + HEADS]
    b = w[:, 3 * gw + HEADS:3 * gw + 2 * HEADS]
    z = w[:, 3 * gw + 2 * HEADS:GDN_COLS]
    a_exp = jnp.repeat(a, HEAD_DIM, axis=1)
    b_exp = jnp.repeat(b, HEAD_DIM, axis=1)
    return jnp.concatenate([q, k, v, z, a_exp, b_exp, w[:, GDN_COLS:]], axis=1).astype(BF16)


def _outproj_body(o1, o2, o3, o4, x_ref, w_ref, g_ref, out_ref):
    gw = GROUP_WIDTH
    y = _dot(o1[...], w_ref[0:gw, :])
    y = y + _dot(o2[...], w_ref[gw:2 * gw, :])
    y = y + _dot(o3[...], w_ref[2 * gw:3 * gw, :])
    y = y + _dot(o4[...], w_ref[3 * gw:4 * gw, :])
    ms = jnp.mean(y * y, axis=-1, keepdims=True)
    out_ref[...] = x_ref[...] + y * lax.rsqrt(ms + NORM_EPS) * g_ref[...]


def _outproj(os_, x2, w, g, tm=512):
    m, d = x2.shape
    gw = GROUP_WIDTH
    return pl.pallas_call(
        _outproj_body,
        out_shape=jax.ShapeDtypeStruct((m, d), F32),
        grid=(m // tm,),
        in_specs=[pl.BlockSpec((tm, gw), lambda i: (i, 0))] * 4 + [
            pl.BlockSpec((tm, d), lambda i: (i, 0)),
            pl.BlockSpec((d, d), lambda i: (0, 0)),
            pl.BlockSpec((1, d), lambda i: (0, 0))],
        out_specs=pl.BlockSpec((tm, d), lambda i: (i, 0)),
        compiler_params=pltpu.CompilerParams(dimension_semantics=("arbitrary",),
                                             vmem_limit_bytes=VMEM_LIMIT),
        name="outproj",
    )(*os_, x2, w, g)


def _gdn_body(p_ref, ab_ref, cw_ref, alog_ref, dtb_ref, ng_ref, o_ref,
              qdec_s, kdec_s, w_s, u_s, aqk_s, dl_s, state_s, oraw_s):
    gw = GROUP_WIDTH
    t_len = p_ref.shape[1]
    c_len = GDN_CHUNK
    n_chunks = t_len // c_len

    cw = cw_ref[...]
    a_neg = -jnp.exp(alog_ref[...])
    dtb = dtb_ref[...]
    row = _iota((c_len, gw), 0)
    s_idx = _iota((c_len, gw), 1) & (HEAD_DIM - 1)
    lower = s_idx <= row
    strict = s_idx < row
    eye = s_idx == row
    bd = _head_block_mask(gw, gw)
    ones_bd = jnp.where(bd, 1.0, 0.0).astype(BF16)
    tril = jnp.where(_iota((c_len, c_len), 1) <= _iota((c_len, c_len), 0), 1.0, 0.0).astype(BF16)
    row8 = _iota((8, 3 * gw), 0)

    def block_diag(x):
        return jnp.where(bd, jnp.concatenate([x] * HEADS, axis=0), jnp.zeros((), x.dtype))

    def mm_pairs(left, right):
        l0, l1 = _split(left, 2)
        r0, r1 = _split(right, 2)
        rb0, rb1 = block_diag(r0), block_diag(r1)
        return _dot(l0, rb0) + (_dot(l0, rb1) + _dot(l1, rb0))

    def pre(c, tail):
        r0 = pl.multiple_of(c * c_len, c_len)
        x = p_ref[0, pl.ds(r0, c_len), 0:3 * gw]
        acc = x * cw[GDN_CONV - 1:GDN_CONV, :]
        for j in range(GDN_CONV - 1):
            shift = GDN_CONV - 1 - j
            rolled = pltpu.roll(x, shift, axis=0)
            tail_r = pltpu.roll(tail, shift, axis=0)
            head_rows = jnp.where(row8 < shift, tail_r, rolled[0:8])
            shifted = jnp.concatenate([head_rows, rolled[8:]], axis=0)
            acc = acc + shifted * cw[j:j + 1, :]
        y = _silu(acc)
        q = y[:, 0:gw]
        k = y[:, gw:2 * gw]
        v = y[:, 2 * gw:3 * gw]
        q = q * lax.rsqrt(_head_sum(q * q, ones_bd) + NORM_EPS) * (HEAD_DIM ** -0.5)
        k = k * lax.rsqrt(_head_sum(k * k, ones_bd) + NORM_EPS)

        a = ab_ref[0, pl.ds(r0, c_len), 0:gw]
        b = ab_ref[0, pl.ds(r0, c_len), gw:2 * gw]
        g = a_neg * _softplus(a + dtb)
        beta = _sigmoid(b)
        g3 = _split(g, 3)
        gc = _dot(tril, g3[0]) + _dot(tril, g3[1]) + _dot(tril, g3[2])
        g_row = jnp.sum(jnp.where(eye, gc, 0.0), axis=0, keepdims=True)
        decay = jnp.exp(jnp.where(lower, gc - g_row, NEG_INF))
        g_last = gc[c_len - 1:c_len, :]
        eg = jnp.exp(gc)

        kb = k * beta
        k_bd = block_diag(k.astype(BF16))
        sc = _dot_nt(jnp.concatenate([kb, q], axis=0).astype(BF16), k_bd)
        a_kk = jnp.where(strict, sc[0:c_len] * decay, 0.0)
        a_qk = sc[c_len:2 * c_len] * decay

        m = -a_kk
        t_inv = jnp.where(eye, 1.0, 0.0) + m
        for _ in range(5):
            m = mm_pairs(m, m)
            t_inv = t_inv + mm_pairs(m, t_inv)
        t_b = t_inv.astype(BF16)
        u = _dot(t_b, block_diag((v * beta).astype(BF16)))
        w = _dot(t_b, block_diag((kb * eg).astype(BF16)))

        rows = pl.ds(r0, c_len)
        qdec_s[rows, :] = (q * eg).astype(BF16)
        kdec_s[rows, :] = (k * jnp.exp(g_last - gc)).astype(BF16)
        w_s[rows, :] = w.astype(BF16)
        u_s[rows, :] = u
        aqk_s[rows, :] = a_qk.astype(BF16)
        dl_s[pl.ds(c, 1), :] = jnp.exp(g_last)
        return x[c_len - 8:c_len, :]

    lax.fori_loop(0, n_chunks, pre, jnp.zeros((8, 3 * gw), F32))

    state_s[...] = jnp.zeros_like(state_s)

    def seq(c, carry):
        r0 = pl.multiple_of(c * c_len, c_len)
        rows = pl.ds(r0, c_len)
        state = state_s[...]
        ws = _dot(jnp.concatenate([w_s[rows, :], qdec_s[rows, :]], axis=0), state.astype(BF16))
        v_new = (u_s[rows, :] - ws[0:c_len]).astype(BF16)
        oraw_s[rows, :] = ws[c_len:2 * c_len] + _dot(aqk_s[rows, :], block_diag(v_new))
        kv = _dot_tn(kdec_s[rows, :], v_new)
        state_s[...] = state * dl_s[pl.ds(c, 1), :] + jnp.where(bd, kv, 0.0)
        return carry

    lax.fori_loop(0, n_chunks, seq, 0)

    def post(t, carry):
        r0 = pl.multiple_of(t * HGRN_TILE, HGRN_TILE)
        rows = pl.ds(r0, HGRN_TILE)
        o = oraw_s[rows, :]
        z = p_ref[0, rows, 3 * gw:4 * gw]
        ms = _head_sum(o * o, ones_bd) * (1.0 / HEAD_DIM)
        o_ref[0, rows, :] = (o * lax.rsqrt(ms + NORM_EPS) * ng_ref[...] * _silu(z)).astype(o_ref.dtype)
        return carry

    lax.fori_loop(0, t_len // HGRN_TILE, post, 0)


def _gdn(p, ab, cw, alog_exp, dtb_exp, ng_exp):
    b, t, _ = p.shape
    gw = GROUP_WIDTH
    full = lambda shape: pl.BlockSpec(shape, lambda i: (0,) * len(shape))
    return pl.pallas_call(
        _gdn_body,
        out_shape=jax.ShapeDtypeStruct((b, t, gw), BF16),
        grid=(b,),
        in_specs=[pl.BlockSpec((1, t, 4 * gw), lambda i: (i, 0, 0)),
                  pl.BlockSpec((1, t, 2 * gw), lambda i: (i, 0, 0)),
                  full((GDN_CONV, 3 * gw)), full((1, gw)), full((1, gw)), full((1, gw))],
        out_specs=pl.BlockSpec((1, t, gw), lambda i: (i, 0, 0)),
        scratch_shapes=[pltpu.VMEM((t, gw), BF16), pltpu.VMEM((t, gw), BF16), pltpu.VMEM((t, gw), BF16),
                        pltpu.VMEM((t, gw), F32), pltpu.VMEM((t, gw), BF16),
                        pltpu.VMEM((t // GDN_CHUNK, gw), F32), pltpu.VMEM((gw, gw), F32),
                        pltpu.VMEM((t, gw), F32)],
        compiler_params=pltpu.CompilerParams(dimension_semantics=("arbitrary",),
                                             vmem_limit_bytes=VMEM_LIMIT),
        name="gdn",
    )(p, ab, cw, alog_exp, dtb_exp, ng_exp)


def _hgrn_body(layer, p_ref, lbp_ref, ng_ref, o_ref, qdec_s, kdec_s, oi_s, dl_s, state_s, oraw_s):
    gw = GROUP_WIDTH
    t_len = p_ref.shape[1]
    c_len = HGRN_CHUNK
    tile = HGRN_TILE
    chunks_per_tile = tile // c_len

    lbp = lbp_ref[...]
    e = jnp.exp(lbp - jnp.max(lbp, axis=0, keepdims=True))
    sm = e / jnp.sum(e, axis=0, keepdims=True)
    if layer == 0:
        lb = jnp.zeros((1, gw), F32)
    else:
        lb = jnp.sum(sm[1:layer + 1], axis=0, keepdims=True)
    log_lb = jnp.log(lb)
    log_1m = jnp.log1p(-lb)

    r = _iota((tile, tile), 0)
    c = _iota((tile, tile), 1)
    same = (r >> HGRN_CHUNK_SHIFT) == (c >> HGRN_CHUNK_SHIFT)
    mid = ((r >> HGRN_CHUNK_SHIFT) << HGRN_CHUNK_SHIFT) + c_len // 2
    causal = same & (c <= r)
    m_cum = jnp.where(causal, 1.0, 0.0)
    m_mid = (jnp.where(same & (c > mid) & (c <= r), 1.0, 0.0)
             - jnp.where(same & (c > r) & (c <= mid), 1.0, 0.0))
    m_rest = jnp.where(same & (c > r), 1.0, 0.0)
    m_tot = jnp.where((_iota((chunks_per_tile, tile), 1) >> HGRN_CHUNK_SHIFT)
                      == _iota((chunks_per_tile, tile), 0), 1.0, 0.0)
    mats = jnp.concatenate([m_cum, m_mid, m_rest, m_tot], axis=0).astype(BF16)
    lane_head = _iota((1, gw), 1) >> HEAD_SHIFT
    bd = _head_block_mask(gw, gw)
    ones_bd = jnp.where(bd, 1.0, 0.0).astype(BF16)

    def pre(t, carry):
        r0 = pl.multiple_of(t * tile, tile)
        rows = pl.ds(r0, tile)
        q = p_ref[0, rows, 0:gw]
        f = p_ref[0, rows, gw:2 * gw]
        v = p_ref[0, rows, 2 * gw:3 * gw]
        log_sig = jnp.minimum(f, 0.0) - jnp.log1p(jnp.exp(-jnp.abs(f)))
        b_ = log_1m + log_sig
        log_f = jnp.maximum(log_lb, b_) + jnp.log1p(jnp.exp(-jnp.abs(log_lb - b_)))
        k = (1.0 - lb) * _sigmoid(-f)
        qd = _silu(q)
        f3 = _split(log_f, 3)
        g_all = _dot(mats, f3[0]) + _dot(mats, f3[1]) + _dot(mats, f3[2])
        g_cum = g_all[0:tile]
        g_mid = g_all[tile:2 * tile]
        g_rest = g_all[2 * tile:3 * tile]
        g_tot = g_all[3 * tile:3 * tile + chunks_per_tile]
        qa = qd * jnp.exp(g_mid)
        ka = (k * jnp.exp(-g_mid)).astype(BF16)
        qdec_s[rows, :] = (qd * jnp.exp(g_cum)).astype(BF16)
        kdec_s[rows, :] = (k * jnp.exp(g_rest)).astype(BF16)
        dl_s[pl.ds(pl.multiple_of(t * chunks_per_tile, chunks_per_tile), chunks_per_tile), :] = jnp.exp(g_tot)
        vb = v.astype(BF16)
        oi = jnp.zeros((tile, gw), F32)
        for h in range(HEADS):
            hm = lane_head == h
            a = _dot_nt(jnp.where(hm, qa, 0.0).astype(BF16), ka)
            a = jnp.where(causal, a, 0.0).astype(BF16)
            oi = oi + jnp.where(hm, _dot(a, vb), 0.0)
        oi_s[rows, :] = oi
        return carry

    lax.fori_loop(0, t_len // tile, pre, 0)

    state_s[...] = jnp.zeros_like(state_s)

    def seq(ci, carry):
        r0 = pl.multiple_of(ci * c_len, c_len)
        rows = pl.ds(r0, c_len)
        state_t = state_s[...]
        oraw_s[rows, :] = _dot_nt(qdec_s[rows, :], state_t.astype(BF16)) + oi_s[rows, :]
        vb = p_ref[0, rows, 2 * gw:3 * gw].astype(BF16)
        kv_t = _dot_tn(vb, kdec_s[rows, :])
        state_s[...] = state_t * dl_s[pl.ds(ci, 1), :] + jnp.where(bd, kv_t, 0.0)
        return carry

    lax.fori_loop(0, t_len // c_len, seq, 0)

    def post(t, carry):
        r0 = pl.multiple_of(t * tile, tile)
        rows = pl.ds(r0, tile)
        o = oraw_s[rows, :]
        z = p_ref[0, rows, 3 * gw:4 * gw]
        ms = _head_sum(o * o, ones_bd) * (1.0 / HEAD_DIM)
        o_ref[0, rows, :] = (o * lax.rsqrt(ms + NORM_EPS) * ng_ref[...] * _silu(z)).astype(o_ref.dtype)
        return carry

    lax.fori_loop(0, t_len // tile, post, 0)


def _hgrn(layer, p, lbp, ng_exp):
    b, t, _ = p.shape
    gw = GROUP_WIDTH
    depth = lbp.shape[0]
    full = lambda shape: pl.BlockSpec(shape, lambda i: (0,) * len(shape))
    return pl.pallas_call(
        functools.partial(_hgrn_body, layer),
        out_shape=jax.ShapeDtypeStruct((b, t, gw), BF16),
        grid=(b,),
        in_specs=[pl.BlockSpec((1, t, 4 * gw), lambda i: (i, 0, 0)), full((depth, gw)), full((1, gw))],
        out_specs=pl.BlockSpec((1, t, gw), lambda i: (i, 0, 0)),
        scratch_shapes=[pltpu.VMEM((t, gw), BF16), pltpu.VMEM((t, gw), BF16), pltpu.VMEM((t, gw), F32),
                        pltpu.VMEM((t // HGRN_CHUNK, gw), F32), pltpu.VMEM((gw, gw), F32),
                        pltpu.VMEM((t, gw), F32)],
        compiler_params=pltpu.CompilerParams(dimension_semantics=("arbitrary",),
                                             vmem_limit_bytes=VMEM_LIMIT),
        name="hgrn2",
    )(p, lbp, ng_exp)


VT_ROWS = HEAD_DIM + 16
FLASH_STREAMS = 4
LOG2E = 1.4426950408889634


def _eye_bf16(n):
    return jnp.where(_iota((n, n), 0) == _iota((n, n), 1), 1.0, 0.0).astype(BF16)


def _attn_setup(p_ref, kb_s, vta_s, bias_s, slopes):
    gw = GROUP_WIDTH
    t_len = p_ref.shape[1]
    tile = ATTN_TILE
    eye = _eye_bf16(gw)
    ones_rows = jnp.ones((VT_ROWS - HEAD_DIM, tile), BF16)

    def body(n, carry):
        rows = pl.ds(pl.multiple_of(n * tile, tile), tile)
        kb_s[rows, :] = p_ref[0, rows, gw:2 * gw].astype(BF16)
        v_t = _dot_nt(eye, p_ref[0, rows, 2 * gw:3 * gw].astype(BF16)).astype(BF16)
        for h in range(HEADS):
            vta_s[n, h, 0:HEAD_DIM, :] = v_t[h * HEAD_DIM:(h + 1) * HEAD_DIM]
            vta_s[n, h, HEAD_DIM:VT_ROWS, :] = ones_rows
        return carry

    lax.fori_loop(0, t_len // tile, body, 0)
    key_local = _iota((tile, tile), 0).astype(F32)
    for h in range(HEADS):
        bias_s[h] = key_local * (slopes[h] * LOG2E)


def _flash_t(qts, heads, slopes, kb_s, vta_s, bias_s, ts_s, i, sel_row):
    tile = ATTN_TILE
    n = len(qts)
    causal = _iota((tile, tile), 0) <= _iota((tile, tile), 1)

    def scores(b, slot):
        k_b = kb_s[pl.ds(pl.multiple_of(b * tile, tile), tile), :]
        for c in range(n):
            ts_s[slot][c] = _dot(k_b, qts[c])

    def absorb(b, slot, ms, accs, own):
        dist = jnp.full((1, tile), (i - b) * tile, jnp.int32).astype(F32)
        new_ms, alphas, ps = [], [], []
        for c in range(n):
            t = ts_s[slot][c] + bias_s[heads[c]]
            if own:
                t = jnp.where(causal, t, NEG_INF)
            else:
                sel = sel_row(c, b)
                if sel is not None:
                    t = jnp.where(sel > 0.5, t, NEG_INF)
            off = dist * (-slopes[heads[c]] * LOG2E)
            m_new = jnp.maximum(ms[c], jnp.max(t, axis=0, keepdims=True) + off)
            alphas.append(jnp.exp2(ms[c] - m_new))
            ps.append(jnp.exp2(t - (m_new - off)).astype(BF16))
            new_ms.append(m_new)
        new_accs = [alphas[c] * accs[c] + _dot(vta_s[b, heads[c]], ps[c]) for c in range(n)]
        return tuple(new_ms), tuple(new_accs)

    def pair(p, carry):
        b = 2 * p
        scores(b + 1, 1)
        carry = absorb(b, 0, carry[0], carry[1], own=False)
        scores(b + 2, 0)
        return absorb(b + 1, 1, carry[0], carry[1], own=False)

    def odd_one(r, carry):
        carry = absorb(i - 1, 0, carry[0], carry[1], own=False)
        scores(i, 0)
        return carry

    scores(0, 0)
    carry = (tuple(jnp.full((1, tile), -1e30, F32) for _ in range(n)),
             tuple(jnp.zeros((VT_ROWS, tile), F32) for _ in range(n)))
    carry = lax.fori_loop(0, i >> 1, pair, carry)
    carry = lax.fori_loop(0, i & 1, odd_one, carry)
    _, accs = absorb(i, 0, carry[0], carry[1], own=True)
    return accs


def _moba_body(p_ref, o_ref, kb_s, vta_s, bias_s, ts_a, ts_b, km_s, sel_s):
    ts_s = (ts_a, ts_b)
    gw = GROUP_WIDTH
    t_len = p_ref.shape[1]
    tile = ATTN_TILE
    nb = t_len // MOBA_BLOCK
    topk = min(MOBA_TOPK, nb)
    slopes = [2.0 ** -(2 * h + 2) for h in range(HEADS)]

    _attn_setup(p_ref, kb_s, vta_s, bias_s, slopes)

    lane_head = _iota((1, gw), 1) >> HEAD_SHIFT

    def means(n, carry):
        rows = pl.ds(pl.multiple_of(n * MOBA_BLOCK, MOBA_BLOCK), MOBA_BLOCK)
        mean = jnp.sum(p_ref[0, rows, gw:2 * gw], axis=0, keepdims=True) * (1.0 / MOBA_BLOCK)
        for h in range(HEADS):
            km_s[pl.ds(h * nb + n, 1), :] = jnp.where(lane_head == h, mean, 0.0)
        return carry

    lax.fori_loop(0, nb, means, 0)
    km3 = _split(km_s[...], 3)
    eye = _eye_bf16(gw)
    row_head = _iota((gw, 1), 0) >> HEAD_SHIFT
    n_idx = _iota((nb, tile), 0)

    def qtile(i, carry):
        rows_q = pl.ds(pl.multiple_of(i * tile, tile), tile)
        q = p_ref[0, rows_q, 0:gw]
        q3 = _split(q, 3)
        gate = (_dot_nt(km3[0], q3[0]) + _dot_nt(km3[0], q3[1]) + _dot_nt(km3[1], q3[0])
                + _dot_nt(km3[1], q3[1]) + _dot_nt(km3[0], q3[2]) + _dot_nt(km3[2], q3[0]))
        past_blk = n_idx < i
        for h in range(HEADS):
            gm = jnp.where(past_blk, gate[h * nb:(h + 1) * nb], NEG_INF)
            cnt = jnp.zeros((nb, tile), F32)
            for n2 in range(nb):
                other = gm[n2:n2 + 1, :]
                beats = (other > gm) | ((other == gm) & (n2 < n_idx))
                cnt = cnt + jnp.where(beats, 1.0, 0.0)
            sel_s[h * nb:(h + 1) * nb, :] = jnp.where(past_blk & (cnt < topk), 1.0, 0.0)

        q_t = _dot_nt(eye, (q * (HEAD_DIM ** -0.5 * LOG2E)).astype(BF16)).astype(BF16)
        qts = [jnp.where(row_head == h, q_t, jnp.zeros((), BF16)) for h in range(HEADS)]
        accs = _flash_t(qts, list(range(HEADS)), slopes, kb_s, vta_s, bias_s, ts_s, i,
                        lambda c, j: sel_s[pl.ds(c * nb + j, 1), :])
        o_t = jnp.concatenate([a[0:HEAD_DIM] * (1.0 / a[HEAD_DIM:HEAD_DIM + 1]) for a in accs], axis=0)
        out = _dot_nt(eye, o_t.astype(BF16))
        z = p_ref[0, rows_q, 3 * gw:4 * gw]
        o_ref[0, rows_q, :] = (out * _silu(z)).astype(o_ref.dtype)
        return carry

    lax.fori_loop(0, t_len // tile, qtile, 0)


def _moba(p):
    b, t, _ = p.shape
    gw = GROUP_WIDTH
    assert t % MOBA_BLOCK == 0 and MOBA_BLOCK == ATTN_TILE
    return pl.pallas_call(
        _moba_body,
        out_shape=jax.ShapeDtypeStruct((b, t, gw), BF16),
        grid=(b,),
        in_specs=[pl.BlockSpec((1, t, 4 * gw), lambda i: (i, 0, 0))],
        out_specs=pl.BlockSpec((1, t, gw), lambda i: (i, 0, 0)),
        scratch_shapes=[pltpu.VMEM((t, gw), BF16),
                        pltpu.VMEM((t // ATTN_TILE, HEADS, VT_ROWS, ATTN_TILE), BF16),
                        pltpu.VMEM((HEADS, ATTN_TILE, ATTN_TILE), F32),
                        pltpu.VMEM((FLASH_STREAMS, ATTN_TILE, ATTN_TILE), F32),
                        pltpu.VMEM((FLASH_STREAMS, ATTN_TILE, ATTN_TILE), F32),
                        pltpu.VMEM((HEADS * (t // MOBA_BLOCK), gw), F32),
                        pltpu.VMEM((HEADS * (t // MOBA_BLOCK), ATTN_TILE), F32)],
        compiler_params=pltpu.CompilerParams(dimension_semantics=("arbitrary",),
                                             vmem_limit_bytes=VMEM_LIMIT),
        name="moba",
    )(p)


def _diff_body(layer, p_ref, lq1_ref, lk1_ref, lq2_ref, lk2_ref, ng_ref, o_ref, kb_s, vta_s, bias_s,
               ts_a, ts_b):
    ts_s = (ts_a, ts_b)
    gw = GROUP_WIDTH
    t_len = p_ref.shape[1]
    tile = ATTN_TILE
    lam_init = 0.8 - 0.6 * math.exp(-0.3 * layer)
    lam = (jnp.exp(jnp.sum(lq1_ref[...] * lk1_ref[...], axis=-1, keepdims=True))
           - jnp.exp(jnp.sum(lq2_ref[...] * lk2_ref[...], axis=-1, keepdims=True)) + lam_init)
    slopes = [2.0 ** -(2 * h + 1) for h in range(HEADS)]

    _attn_setup(p_ref, kb_s, vta_s, bias_s, slopes)
    row_map = _iota((gw, 1), 0) >> (HEAD_SHIFT - 1)
    eye = _eye_bf16(gw)

    def qtile(i, carry):
        rows_q = pl.ds(pl.multiple_of(i * tile, tile), tile)
        q = p_ref[0, rows_q, 0:gw] * (DIFF_DIM ** -0.5 * LOG2E)
        q_t = _dot_nt(eye, q.astype(BF16)).astype(BF16)
        ys = []
        for pair in range(HEADS // 2):
            streams = [(2 * pair + hh, mp) for hh in range(2) for mp in range(2)]
            qts = [jnp.where(row_map == 2 * h + mp, q_t, jnp.zeros((), BF16)) for h, mp in streams]
            accs = _flash_t(qts, [h for h, _ in streams], slopes, kb_s, vta_s, bias_s, ts_s, i,
                            lambda c, j: None)
            o_m = [a[0:HEAD_DIM] * (1.0 / a[HEAD_DIM:HEAD_DIM + 1]) for a in accs]
            for hh in range(2):
                o_h = o_m[2 * hh] - lam * o_m[2 * hh + 1]
                ms = jnp.mean(o_h * o_h, axis=0, keepdims=True)
                ys.append(o_h * lax.rsqrt(ms + NORM_EPS))
        out = _dot_nt(eye, jnp.concatenate(ys, axis=0).astype(BF16))
        z = p_ref[0, rows_q, 3 * gw:4 * gw]
        o_ref[0, rows_q, :] = (out * ng_ref[...] * (1.0 - lam_init) * _silu(z)).astype(o_ref.dtype)
        return carry

    lax.fori_loop(0, t_len // tile, qtile, 0)


def _diff(layer, p, lq1, lk1, lq2, lk2, ng_exp):
    b, t, _ = p.shape
    gw = GROUP_WIDTH
    assert t % ATTN_TILE == 0
    full = lambda shape: pl.BlockSpec(shape, lambda i: (0,) * len(shape))
    return pl.pallas_call(
        functools.partial(_diff_body, layer),
        out_shape=jax.ShapeDtypeStruct((b, t, gw), BF16),
        grid=(b,),
        in_specs=[pl.BlockSpec((1, t, 4 * gw), lambda i: (i, 0, 0))] + [full((1, DIFF_DIM))] * 4 + [full((1, gw))],
        out_specs=pl.BlockSpec((1, t, gw), lambda i: (i, 0, 0)),
        scratch_shapes=[pltpu.VMEM((t, gw), BF16),
                        pltpu.VMEM((t // ATTN_TILE, HEADS, VT_ROWS, ATTN_TILE), BF16),
                        pltpu.VMEM((HEADS, ATTN_TILE, ATTN_TILE), F32),
                        pltpu.VMEM((FLASH_STREAMS, ATTN_TILE, ATTN_TILE), F32),
                        pltpu.VMEM((FLASH_STREAMS, ATTN_TILE, ATTN_TILE), F32)],
        compiler_params=pltpu.CompilerParams(dimension_semantics=("arbitrary",),
                                             vmem_limit_bytes=VMEM_LIMIT),
        name="diffattn",
    )(p, lq1, lk1, lq2, lk2, ng_exp)


def kernel(x, pre_norm_g, post_norm_g, w_in, conv_w, gdn_a_log, gdn_dt_bias, gdn_norm_g,
           hgrn_lb, hgrn_norm_g, diff_lq1, diff_lk1, diff_lq2, diff_lk2, diff_norm_g, w_out):
    b, t, d = x.shape
    depth = w_in.shape[0]
    gw = GROUP_WIDTH
    x2 = x.astype(F32).reshape(b * t, d)
    row = lambda a: a.astype(F32).reshape(1, -1)
    per_head = lambda a: jnp.repeat(a.astype(F32), HEAD_DIM).reshape(1, gw)
    per_dim = lambda a: jnp.tile(a.astype(F32), HEADS).reshape(1, gw)
    lbp = hgrn_lb.astype(F32)
    for l in range(depth):
        p_gdn, p_ab, p_hgrn, p_moba, p_diff = _inproj(x2, row(pre_norm_g[l]), _reorder_w_in(w_in[l]))
        shp = lambda a: a.reshape(b, t, a.shape[-1])
        o_gdn = _gdn(shp(p_gdn), shp(p_ab), conv_w[l].astype(F32), per_head(gdn_a_log[l]),
                     per_head(gdn_dt_bias[l]), per_dim(gdn_norm_g[l]))
        o_hgrn = _hgrn(l, shp(p_hgrn), lbp, per_dim(hgrn_norm_g[l]))
        o_moba = _moba(shp(p_moba))
        o_diff = _diff(l, shp(p_diff), row(diff_lq1[l]), row(diff_lk1[l]), row(diff_lq2[l]),
                       row(diff_lk2[l]), per_dim(diff_norm_g[l]))
        os_ = [o.reshape(b * t, gw) for o in (o_gdn, o_hgrn, o_moba, o_diff)]
        x2 = _outproj(os_, x2, w_out[l].astype(BF16), row(post_norm_g[l]))
    return x2.reshape(b, t, d)
```

```python
import functools
import math

import jax
import jax.numpy as jnp
from jax import lax
from jax.experimental import pallas as pl
from jax.experimental.pallas import tpu as pltpu

F32 = jnp.float32
BF16 = jnp.bfloat16

HEADS = 4
HEAD_DIM = 64
GROUP_WIDTH = HEADS * HEAD_DIM
HEAD_SHIFT = 6
GDN_CONV = 4
GDN_CHUNK = 64
GDN_CHUNK_SHIFT = 6
GDN_GROUP = 4
HGRN_CHUNK = 16
HGRN_CHUNK_SHIFT = 4
HGRN_TILE = 128
HGRN_GROUP = 2
MOBA_BLOCK = 256
MOBA_TOPK = 3
DIFF_DIM = HEAD_DIM // 2
ATTN_TILE = 256
NORM_EPS = 1e-6
NEG_INF = float("-inf")

GDN_COLS = 4 * GROUP_WIDTH + 2 * HEADS
SEG_WIDTHS = (4 * GROUP_WIDTH, 2 * GROUP_WIDTH, 4 * GROUP_WIDTH, 4 * GROUP_WIDTH, 4 * GROUP_WIDTH)
VMEM_LIMIT = 56 * 1024 * 1024


def _iota(shape, dim):
    return lax.broadcasted_iota(jnp.int32, shape, dim)


def _dot(a, b):
    return jnp.dot(a, b, preferred_element_type=F32)


def _dot_nt(a, b):
    return lax.dot_general(a, b, (((1,), (1,)), ((), ())), preferred_element_type=F32)


def _dot_tn(a, b):
    return lax.dot_general(a, b, (((0,), (0,)), ((), ())), preferred_element_type=F32)


def _split(x, n):
    parts = []
    r = x
    for i in range(n):
        p = r.astype(BF16)
        parts.append(p)
        if i + 1 < n:
            r = r - p.astype(F32)
    return parts


def _sigmoid(x):
    return 1.0 / (1.0 + jnp.exp(-x))


def _silu(x):
    return x * _sigmoid(x)


def _softplus(x):
    return jnp.maximum(x, 0.0) + jnp.log1p(jnp.exp(-jnp.abs(x)))


def _head_block_mask(rows, cols):
    return (_iota((rows, cols), 0) >> HEAD_SHIFT) == (_iota((rows, cols), 1) >> HEAD_SHIFT)


def _head_sum(x, ones_bd):
    hi, lo = _split(x, 2)
    return _dot(hi, ones_bd) + _dot(lo, ones_bd)


def _inproj_body(x_ref, g_ref, w_ref, *o_refs):
    x = x_ref[...]
    ms = jnp.mean(x * x, axis=-1, keepdims=True)
    h = (x * lax.rsqrt(ms + NORM_EPS) * g_ref[...]).astype(BF16)
    off = 0
    for o in o_refs:
        n = o.shape[-1]
        o[...] = _dot(h, w_ref[:, off:off + n]).astype(o.dtype)
        off += n


def _inproj(x2, g, w, tm=256):
    m, d = x2.shape
    n = w.shape[1]
    assert n == sum(SEG_WIDTHS) and m % tm == 0
    return pl.pallas_call(
        _inproj_body,
        out_shape=[jax.ShapeDtypeStruct((m, s), F32) for s in SEG_WIDTHS],
        grid=(m // tm,),
        in_specs=[pl.BlockSpec((tm, d), lambda i: (i, 0)),
                  pl.BlockSpec((1, d), lambda i: (0, 0)),
                  pl.BlockSpec((d, n), lambda i: (0, 0))],
        out_specs=[pl.BlockSpec((tm, s), lambda i: (i, 0)) for s in SEG_WIDTHS],
        compiler_params=pltpu.CompilerParams(dimension_semantics=("arbitrary",),
                                             vmem_limit_bytes=VMEM_LIMIT),
        name="inproj",
    )(x2, g, w)


def _reorder_w_in(w):
    gw = GROUP_WIDTH
    q, k, v = w[:, 0:gw], w[:, gw:2 * gw], w[:, 2 * gw:3 * gw]
    a = w[:, 3 * gw:3 * gw + HEADS]
    b = w[:, 3 * gw + HEADS:3 * gw + 2 * HEADS]
    z = w[:, 3 * gw + 2 * HEADS:GDN_COLS]
    a_exp = jnp.repeat(a, HEAD_DIM, axis=1)
    b_exp = jnp.repeat(b, HEAD_DIM, axis=1)
    return jnp.concatenate([q, k, v, z, a_exp, b_exp, w[:, GDN_COLS:]], axis=1).astype(BF16)


def _outproj_body(o1, o2, o3, o4, x_ref, w_ref, g_ref, out_ref):
    gw = GROUP_WIDTH
    y = _dot(o1[...], w_ref[0:gw, :])
    y = y + _dot(o2[...], w_ref[gw:2 * gw, :])
    y = y + _dot(o3[...], w_ref[2 * gw:3 * gw, :])
    y = y + _dot(o4[...], w_ref[3 * gw:4 * gw, :])
    ms = jnp.mean(y * y, axis=-1, keepdims=True)
    out_ref[...] = x_ref[...] + y * lax.rsqrt(ms + NORM_EPS) * g_ref[...]


def _outproj(os_, x2, w, g, tm=512):
    m, d = x2.shape
    gw = GROUP_WIDTH
    return pl.pallas_call(
        _outproj_body,
        out_shape=jax.ShapeDtypeStruct((m, d), F32),
        grid=(m // tm,),
        in_specs=[pl.BlockSpec((tm, gw), lambda i: (i, 0))] * 4 + [
            pl.BlockSpec((tm, d), lambda i: (i, 0)),
            pl.BlockSpec((d, d), lambda i: (0, 0)),
            pl.BlockSpec((1, d), lambda i: (0, 0))],
        out_specs=pl.BlockSpec((tm, d), lambda i: (i, 0)),
        compiler_params=pltpu.CompilerParams(dimension_semantics=("arbitrary",),
                                             vmem_limit_bytes=VMEM_LIMIT),
        name="outproj",
    )(*os_, x2, w, g)


def _gdn_body(p_ref, ab_ref, cw_ref, alog_ref, dtb_ref, ng_ref, o_ref,
              qdec_s, kdec_s, w_s, u_s, aqk_s, dl_s, state_s, oraw_s):
    gw = GROUP_WIDTH
    t_len = p_ref.shape[1]
    c_len = GDN_CHUNK
    n_chunks = t_len // c_len

    cw = cw_ref[...]
    a_neg = -jnp.exp(alog_ref[...])
    dtb = dtb_ref[...]
    row = _iota((c_len, gw), 0)
    s_idx = _iota((c_len, gw), 1) & (HEAD_DIM - 1)
    lower = s_idx <= row
    strict = s_idx < row
    eye = s_idx == row
    bd = _head_block_mask(gw, gw)
    ones_bd = jnp.where(bd, 1.0, 0.0).astype(BF16)
    row8 = _iota((8, 3 * gw), 0)

    def block_diag(x):
        return jnp.where(bd, jnp.concatenate([x] * HEADS, axis=0), jnp.zeros((), x.dtype))

    def mm_pairs(left, right):
        return _dot(left.astype(BF16), block_diag(right.astype(BF16)))

    group = GDN_GROUP
    g_rows = group * c_len
    r_g = _iota((g_rows, g_rows), 0)
    c_g = _iota((g_rows, g_rows), 1)
    tril_g = jnp.where((c_g <= r_g) & ((r_g >> GDN_CHUNK_SHIFT) == (c_g >> GDN_CHUNK_SHIFT)),
                       1.0, 0.0).astype(BF16)

    def pre(t, tail):
        r0 = pl.multiple_of(t * g_rows, g_rows)
        x = p_ref[0, pl.ds(r0, g_rows), 0:3 * gw]
        acc = x * cw[GDN_CONV - 1:GDN_CONV, :]
        for j in range(GDN_CONV - 1):
            shift = GDN_CONV - 1 - j
            rolled = pltpu.roll(x, shift, axis=0)
            tail_r = pltpu.roll(tail, shift, axis=0)
            head_rows = jnp.where(row8 < shift, tail_r, rolled[0:8])
            shifted = jnp.concatenate([head_rows, rolled[8:]], axis=0)
            acc = acc + shifted * cw[j:j + 1, :]
        y = _silu(acc)
        q = y[:, 0:gw]
        k = y[:, gw:2 * gw]
        v = y[:, 2 * gw:3 * gw]
        q = q * lax.rsqrt(_head_sum(q * q, ones_bd) + NORM_EPS) * (HEAD_DIM ** -0.5)
        k = k * lax.rsqrt(_head_sum(k * k, ones_bd) + NORM_EPS)

        a = ab_ref[0, pl.ds(r0, g_rows), 0:gw]
        b = ab_ref[0, pl.ds(r0, g_rows), gw:2 * gw]
        g = a_neg * _softplus(a + dtb)
        beta = _sigmoid(b)
        g3 = _split(g, 3)
        gc = _dot(tril_g, g3[0]) + _dot(tril_g, g3[1]) + _dot(tril_g, g3[2])
        eg = jnp.exp(gc)
        kb = k * beta
        vb = (v * beta).astype(BF16)
        kbg = (kb * eg).astype(BF16)
        qdec_s[pl.ds(r0, g_rows), :] = (q * eg).astype(BF16)
        lhs = kb.astype(BF16), q.astype(BF16), k.astype(BF16)

        ms, ts, aqks = [], [], []
        for j in range(group):
            sl = slice(j * c_len, (j + 1) * c_len)
            gc_j = gc[sl]
            g_row = jnp.sum(jnp.where(eye, gc_j, 0.0), axis=0, keepdims=True)
            decay = jnp.exp(jnp.where(lower, gc_j - g_row, NEG_INF))
            g_last = gc_j[c_len - 1:c_len, :]
            sc = _dot_nt(jnp.concatenate([lhs[0][sl], lhs[1][sl]], axis=0), block_diag(lhs[2][sl]))
            m = -jnp.where(strict, sc[0:c_len] * decay, 0.0)
            ms.append(m)
            ts.append(jnp.where(eye, 1.0, 0.0) + m)
            rows = pl.ds(r0 + j * c_len, c_len)
            aqk_s[rows, :] = (sc[c_len:2 * c_len] * decay).astype(BF16)
            kdec_s[rows, :] = (k[sl] * jnp.exp(g_last - gc_j)).astype(BF16)
            dl_s[pl.ds(t * group + j, 1), :] = jnp.exp(g_last)

        for _ in range(5):
            ms = [mm_pairs(m, m) for m in ms]
            ts = [t_inv + mm_pairs(m, t_inv) for m, t_inv in zip(ms, ts)]
        for j in range(group):
            sl = slice(j * c_len, (j + 1) * c_len)
            rows = pl.ds(r0 + j * c_len, c_len)
            t_b = ts[j].astype(BF16)
            u_s[rows, :] = _dot(t_b, block_diag(vb[sl]))
            w_s[rows, :] = _dot(t_b, block_diag(kbg[sl])).astype(BF16)
        return x[g_rows - 8:g_rows, :]

    lax.fori_loop(0, n_chunks // group, pre, jnp.zeros((8, 3 * gw), F32))

    state_s[...] = jnp.zeros_like(state_s)

    def seq(c, carry):
        r0 = pl.multiple_of(c * c_len, c_len)
        rows = pl.ds(r0, c_len)
        state = state_s[...]
        ws = _dot(jnp.concatenate([w_s[rows, :], qdec_s[rows, :]], axis=0), state.astype(BF16))
        v_new = (u_s[rows, :] - ws[0:c_len]).astype(BF16)
        oraw_s[rows, :] = ws[c_len:2 * c_len] + _dot(aqk_s[rows, :], block_diag(v_new))
        kv = _dot_tn(kdec_s[rows, :], v_new)
        state_s[...] = state * dl_s[pl.ds(c, 1), :] + jnp.where(bd, kv, 0.0)
        return carry

    lax.fori_loop(0, n_chunks, seq, 0)

    def post(t, carry):
        r0 = pl.multiple_of(t * HGRN_TILE, HGRN_TILE)
        rows = pl.ds(r0, HGRN_TILE)
        o = oraw_s[rows, :]
        z = p_ref[0, rows, 3 * gw:4 * gw]
        ms = _head_sum(o * o, ones_bd) * (1.0 / HEAD_DIM)
        o_ref[0, rows, :] = (o * lax.rsqrt(ms + NORM_EPS) * ng_ref[...] * _silu(z)).astype(o_ref.dtype)
        return carry

    lax.fori_loop(0, t_len // HGRN_TILE, post, 0)


def _gdn(p, ab, cw, alog_exp, dtb_exp, ng_exp):
    b, t, _ = p.shape
    gw = GROUP_WIDTH
    full = lambda shape: pl.BlockSpec(shape, lambda i: (0,) * len(shape))
    return pl.pallas_call(
        _gdn_body,
        out_shape=jax.ShapeDtypeStruct((b, t, gw), BF16),
        grid=(b,),
        in_specs=[pl.BlockSpec((1, t, 4 * gw), lambda i: (i, 0, 0)),
                  pl.BlockSpec((1, t, 2 * gw), lambda i: (i, 0, 0)),
                  full((GDN_CONV, 3 * gw)), full((1, gw)), full((1, gw)), full((1, gw))],
        out_specs=pl.BlockSpec((1, t, gw), lambda i: (i, 0, 0)),
        scratch_shapes=[pltpu.VMEM((t, gw), BF16), pltpu.VMEM((t, gw), BF16), pltpu.VMEM((t, gw), BF16),
                        pltpu.VMEM((t, gw), F32), pltpu.VMEM((t, gw), BF16),
                        pltpu.VMEM((t // GDN_CHUNK, gw), F32), pltpu.VMEM((gw, gw), F32),
                        pltpu.VMEM((t, gw), F32)],
        compiler_params=pltpu.CompilerParams(dimension_semantics=("arbitrary",),
                                             vmem_limit_bytes=VMEM_LIMIT),
        name="gdn",
    )(p, ab, cw, alog_exp, dtb_exp, ng_exp)


def _hgrn_body(layer, p_ref, lbp_ref, ng_ref, o_ref, qdec_s, kdec_s, oi_s, dl_s, state_s, oraw_s):
    gw = GROUP_WIDTH
    t_len = p_ref.shape[1]
    c_len = HGRN_CHUNK
    tile = HGRN_TILE
    chunks_per_tile = tile // c_len

    lbp = lbp_ref[...]
    e = jnp.exp(lbp - jnp.max(lbp, axis=0, keepdims=True))
    sm = e / jnp.sum(e, axis=0, keepdims=True)
    if layer == 0:
        lb = jnp.zeros((1, gw), F32)
    else:
        lb = jnp.sum(sm[1:layer + 1], axis=0, keepdims=True)
    log_lb = jnp.log(lb)
    log_1m = jnp.log1p(-lb)

    r = _iota((tile, tile), 0)
    c = _iota((tile, tile), 1)
    same = (r >> HGRN_CHUNK_SHIFT) == (c >> HGRN_CHUNK_SHIFT)
    mid = ((r >> HGRN_CHUNK_SHIFT) << HGRN_CHUNK_SHIFT) + c_len // 2
    causal = same & (c <= r)
    m_cum = jnp.where(causal, 1.0, 0.0)
    m_mid = (jnp.where(same & (c > mid) & (c <= r), 1.0, 0.0)
             - jnp.where(same & (c > r) & (c <= mid), 1.0, 0.0))
    m_rest = jnp.where(same & (c > r), 1.0, 0.0)
    m_tot = jnp.where((_iota((chunks_per_tile, tile), 1) >> HGRN_CHUNK_SHIFT)
                      == _iota((chunks_per_tile, tile), 0), 1.0, 0.0)
    mats = jnp.concatenate([m_cum, m_mid, m_rest, m_tot], axis=0).astype(BF16)
    lane_head = _iota((1, gw), 1) >> HEAD_SHIFT
    bd = _head_block_mask(gw, gw)
    ones_bd = jnp.where(bd, 1.0, 0.0).astype(BF16)

    group = HGRN_GROUP

    def pre(t, carry):
        tiles = []
        for j in range(group):
            rows = pl.ds(pl.multiple_of((t * group + j) * tile, tile), tile)
            q = p_ref[0, rows, 0:gw]
            f = p_ref[0, rows, gw:2 * gw]
            log_sig = jnp.minimum(f, 0.0) - jnp.log1p(jnp.exp(-jnp.abs(f)))
            b_ = log_1m + log_sig
            log_f = jnp.maximum(log_lb, b_) + jnp.log1p(jnp.exp(-jnp.abs(log_lb - b_)))
            tiles.append(dict(rows=rows, k=(1.0 - lb) * _sigmoid(-f), qd=_silu(q), f3=_split(log_f, 3),
                              vb=p_ref[0, rows, 2 * gw:3 * gw].astype(BF16)))
        for j, tl in enumerate(tiles):
            f3 = tl["f3"]
            g_all = _dot(mats, f3[0]) + _dot(mats, f3[1]) + _dot(mats, f3[2])
            g_mid = g_all[tile:2 * tile]
            tl["qa"] = tl["qd"] * jnp.exp(g_mid)
            tl["ka"] = (tl["k"] * jnp.exp(-g_mid)).astype(BF16)
            qdec_s[tl["rows"], :] = (tl["qd"] * jnp.exp(g_all[0:tile])).astype(BF16)
            kdec_s[tl["rows"], :] = (tl["k"] * jnp.exp(g_all[2 * tile:3 * tile])).astype(BF16)
            c0 = pl.multiple_of((t * group + j) * chunks_per_tile, chunks_per_tile)
            dl_s[pl.ds(c0, chunks_per_tile), :] = jnp.exp(g_all[3 * tile:3 * tile + chunks_per_tile])
        scores = [[_dot_nt(jnp.where(lane_head == h, tl["qa"], 0.0).astype(BF16), tl["ka"])
                   for h in range(HEADS)] for tl in tiles]
        for tl, sc in zip(tiles, scores):
            oi = jnp.zeros((tile, gw), F32)
            for h in range(HEADS):
                a = jnp.where(causal, sc[h], 0.0).astype(BF16)
                oi = oi + jnp.where(lane_head == h, _dot(a, tl["vb"]), 0.0)
            oi_s[tl["rows"], :] = oi
        return carry

    lax.fori_loop(0, t_len // (tile * group), pre, 0)

    state_s[...] = jnp.zeros_like(state_s)

    def seq(t, carry):
        rows = pl.ds(pl.multiple_of(t * tile, tile), tile)
        kd = kdec_s[rows, :]
        vb = p_ref[0, rows, 2 * gw:3 * gw].astype(BF16)
        qd = qdec_s[rows, :]
        oi = oi_s[rows, :]
        dl = dl_s[pl.ds(pl.multiple_of(t * chunks_per_tile, chunks_per_tile), chunks_per_tile), :]
        chunk = lambda x, ci: x[ci * c_len:(ci + 1) * c_len]
        kvs = [_dot_tn(chunk(vb, ci), chunk(kd, ci)) for ci in range(chunks_per_tile)]
        state_t = state_s[...]
        states = []
        for ci in range(chunks_per_tile):
            states.append(state_t.astype(BF16))
            state_t = state_t * dl[ci:ci + 1, :] + jnp.where(bd, kvs[ci], 0.0)
        state_s[...] = state_t
        outs = [_dot_nt(chunk(qd, ci), states[ci]) + chunk(oi, ci) for ci in range(chunks_per_tile)]
        oraw_s[rows, :] = jnp.concatenate(outs, axis=0)
        return carry

    lax.fori_loop(0, t_len // tile, seq, 0)

    def post(t, carry):
        r0 = pl.multiple_of(t * tile, tile)
        rows = pl.ds(r0, tile)
        o = oraw_s[rows, :]
        z = p_ref[0, rows, 3 * gw:4 * gw]
        ms = _head_sum(o * o, ones_bd) * (1.0 / HEAD_DIM)
        o_ref[0, rows, :] = (o * lax.rsqrt(ms + NORM_EPS) * ng_ref[...] * _silu(z)).astype(o_ref.dtype)
        return carry

    lax.fori_loop(0, t_len // tile, post, 0)


def _hgrn(layer, p, lbp, ng_exp):
    b, t, _ = p.shape
    gw = GROUP_WIDTH
    depth = lbp.shape[0]
    full = lambda shape: pl.BlockSpec(shape, lambda i: (0,) * len(shape))
    return pl.pallas_call(
        functools.partial(_hgrn_body, layer),
        out_shape=jax.ShapeDtypeStruct((b, t, gw), BF16),
        grid=(b,),
        in_specs=[pl.BlockSpec((1, t, 4 * gw), lambda i: (i, 0, 0)), full((depth, gw)), full((1, gw))],
        out_specs=pl.BlockSpec((1, t, gw), lambda i: (i, 0, 0)),
        scratch_shapes=[pltpu.VMEM((t, gw), BF16), pltpu.VMEM((t, gw), BF16), pltpu.VMEM((t, gw), F32),
                        pltpu.VMEM((t // HGRN_CHUNK, gw), F32), pltpu.VMEM((gw, gw), F32),
                        pltpu.VMEM((t, gw), F32)],
        compiler_params=pltpu.CompilerParams(dimension_semantics=("arbitrary",),
                                             vmem_limit_bytes=VMEM_LIMIT),
        name="hgrn2",
    )(p, lbp, ng_exp)


VT_ROWS = HEAD_DIM + 16
FLASH_STREAMS = 4
LOG2E = 1.4426950408889634


def _eye_bf16(n):
    return jnp.where(_iota((n, n), 0) == _iota((n, n), 1), 1.0, 0.0).astype(BF16)


def _attn_setup(p_ref, kb_s, vta_s, bias_s, slopes):
    gw = GROUP_WIDTH
    t_len = p_ref.shape[1]
    tile = ATTN_TILE
    eye = _eye_bf16(gw)
    ones_rows = jnp.ones((VT_ROWS - HEAD_DIM, tile), BF16)

    def body(n, carry):
        rows = pl.ds(pl.multiple_of(n * tile, tile), tile)
        kb_s[rows, :] = p_ref[0, rows, gw:2 * gw].astype(BF16)
        v_t = _dot_nt(eye, p_ref[0, rows, 2 * gw:3 * gw].astype(BF16)).astype(BF16)
        for h in range(HEADS):
            vta_s[n, h, 0:HEAD_DIM, :] = v_t[h * HEAD_DIM:(h + 1) * HEAD_DIM]
            vta_s[n, h, HEAD_DIM:VT_ROWS, :] = ones_rows
        return carry

    lax.fori_loop(0, t_len // tile, body, 0)
    key_local = _iota((tile, tile), 0).astype(F32)
    for h in range(HEADS):
        bias_s[h] = key_local * (slopes[h] * LOG2E)


def _flash_t(qts, heads, slopes, kb_s, vta_s, bias_s, ts_s, i, sel_row):
    tile = ATTN_TILE
    n = len(qts)
    causal = _iota((tile, tile), 0) <= _iota((tile, tile), 1)

    def scores(b, slot):
        k_b = kb_s[pl.ds(pl.multiple_of(b * tile, tile), tile), :]
        for c in range(n):
            ts_s[slot][c] = _dot(k_b, qts[c])

    def absorb(b, slot, ms, accs, own):
        dist = jnp.full((1, tile), (i - b) * tile, jnp.int32).astype(F32)
        new_ms, alphas, ps = [], [], []
        for c in range(n):
            t = ts_s[slot][c] + bias_s[heads[c]]
            if own:
                t = jnp.where(causal, t, NEG_INF)
            else:
                sel = sel_row(c, b)
                if sel is not None:
                    t = jnp.where(sel > 0.5, t, NEG_INF)
            off = dist * (-slopes[heads[c]] * LOG2E)
            m_new = jnp.maximum(ms[c], jnp.max(t, axis=0, keepdims=True) + off)
            alphas.append(jnp.exp2(ms[c] - m_new))
            ps.append(jnp.exp2(t - (m_new - off)).astype(BF16))
            new_ms.append(m_new)
        new_accs = [alphas[c] * accs[c] + _dot(vta_s[b, heads[c]], ps[c]) for c in range(n)]
        return tuple(new_ms), tuple(new_accs)

    def pair(p, carry):
        b = 2 * p
        scores(b + 1, 1)
        carry = absorb(b, 0, carry[0], carry[1], own=False)
        scores(b + 2, 0)
        return absorb(b + 1, 1, carry[0], carry[1], own=False)

    def odd_one(r, carry):
        carry = absorb(i - 1, 0, carry[0], carry[1], own=False)
        scores(i, 0)
        return carry

    scores(0, 0)
    carry = (tuple(jnp.full((1, tile), -1e30, F32) for _ in range(n)),
             tuple(jnp.zeros((VT_ROWS, tile), F32) for _ in range(n)))
    carry = lax.fori_loop(0, i >> 1, pair, carry)
    carry = lax.fori_loop(0, i & 1, odd_one, carry)
    _, accs = absorb(i, 0, carry[0], carry[1], own=True)
    return accs


def _moba_body(p_ref, o_ref, kb_s, vta_s, bias_s, ts_a, ts_b, km_s, sel_s):
    ts_s = (ts_a, ts_b)
    gw = GROUP_WIDTH
    t_len = p_ref.shape[1]
    tile = ATTN_TILE
    nb = t_len // MOBA_BLOCK
    topk = min(MOBA_TOPK, nb)
    slopes = [2.0 ** -(2 * h + 2) for h in range(HEADS)]

    _attn_setup(p_ref, kb_s, vta_s, bias_s, slopes)

    lane_head = _iota((1, gw), 1) >> HEAD_SHIFT

    def means(n, carry):
        rows = pl.ds(pl.multiple_of(n * MOBA_BLOCK, MOBA_BLOCK), MOBA_BLOCK)
        mean = jnp.sum(p_ref[0, rows, gw:2 * gw], axis=0, keepdims=True) * (1.0 / MOBA_BLOCK)
        for h in range(HEADS):
            km_s[pl.ds(h * nb + n, 1), :] = jnp.where(lane_head == h, mean, 0.0)
        return carry

    lax.fori_loop(0, nb, means, 0)
    km3 = _split(km_s[...], 3)
    eye = _eye_bf16(gw)
    row_head = _iota((gw, 1), 0) >> HEAD_SHIFT
    n_idx = _iota((nb, tile), 0)

    def qtile(i, carry):
        rows_q = pl.ds(pl.multiple_of(i * tile, tile), tile)
        q = p_ref[0, rows_q, 0:gw]
        q3 = _split(q, 3)
        gate = (_dot_nt(km3[0], q3[0]) + _dot_nt(km3[0], q3[1]) + _dot_nt(km3[1], q3[0])
                + _dot_nt(km3[1], q3[1]) + _dot_nt(km3[0], q3[2]) + _dot_nt(km3[2], q3[0]))
        past_blk = n_idx < i
        for h in range(HEADS):
            gm = jnp.where(past_blk, gate[h * nb:(h + 1) * nb], NEG_INF)
            cnt = jnp.zeros((nb, tile), F32)
            for n2 in range(nb):
                other = gm[n2:n2 + 1, :]
                beats = (other > gm) | ((other == gm) & (n2 < n_idx))
                cnt = cnt + jnp.where(beats, 1.0, 0.0)
            sel_s[h * nb:(h + 1) * nb, :] = jnp.where(past_blk & (cnt < topk), 1.0, 0.0)

        q_t = _dot_nt(eye, (q * (HEAD_DIM ** -0.5 * LOG2E)).astype(BF16)).astype(BF16)
        qts = [jnp.where(row_head == h, q_t, jnp.zeros((), BF16)) for h in range(HEADS)]
        accs = _flash_t(qts, list(range(HEADS)), slopes, kb_s, vta_s, bias_s, ts_s, i,
                        lambda c, j: sel_s[pl.ds(c * nb + j, 1), :])
        o_t = jnp.concatenate([a[0:HEAD_DIM] * (1.0 / a[HEAD_DIM:HEAD_DIM + 1]) for a in accs], axis=0)
        out = _dot_nt(eye, o_t.astype(BF16))
        z = p_ref[0, rows_q, 3 * gw:4 * gw]
        o_ref[0, rows_q, :] = (out * _silu(z)).astype(o_ref.dtype)
        return carry

    lax.fori_loop(0, t_len // tile, qtile, 0)


def _moba(p):
    b, t, _ = p.shape
    gw = GROUP_WIDTH
    assert t % MOBA_BLOCK == 0 and MOBA_BLOCK == ATTN_TILE
    return pl.pallas_call(
        _moba_body,
        out_shape=jax.ShapeDtypeStruct((b, t, gw), BF16),
        grid=(b,),
        in_specs=[pl.BlockSpec((1, t, 4 * gw), lambda i: (i, 0, 0))],
        out_specs=pl.BlockSpec((1, t, gw), lambda i: (i, 0, 0)),
        scratch_shapes=[pltpu.VMEM((t, gw), BF16),
                        pltpu.VMEM((t // ATTN_TILE, HEADS, VT_ROWS, ATTN_TILE), BF16),
                        pltpu.VMEM((HEADS, ATTN_TILE, ATTN_TILE), F32),
                        pltpu.VMEM((FLASH_STREAMS, ATTN_TILE, ATTN_TILE), F32),
                        pltpu.VMEM((FLASH_STREAMS, ATTN_TILE, ATTN_TILE), F32),
                        pltpu.VMEM((HEADS * (t // MOBA_BLOCK), gw), F32),
                        pltpu.VMEM((HEADS * (t // MOBA_BLOCK), ATTN_TILE), F32)],
        compiler_params=pltpu.CompilerParams(dimension_semantics=("arbitrary",),
                                             vmem_limit_bytes=VMEM_LIMIT),
        name="moba",
    )(p)


def _diff_body(layer, p_ref, lq1_ref, lk1_ref, lq2_ref, lk2_ref, ng_ref, o_ref, kb_s, vta_s, bias_s,
               ts_a, ts_b):
    ts_s = (ts_a, ts_b)
    gw = GROUP_WIDTH
    t_len = p_ref.shape[1]
    tile = ATTN_TILE
    lam_init = 0.8 - 0.6 * math.exp(-0.3 * layer)
    lam = (jnp.exp(jnp.sum(lq1_ref[...] * lk1_ref[...], axis=-1, keepdims=True))
           - jnp.exp(jnp.sum(lq2_ref[...] * lk2_ref[...], axis=-1, keepdims=True)) + lam_init)
    slopes = [2.0 ** -(2 * h + 1) for h in range(HEADS)]

    _attn_setup(p_ref, kb_s, vta_s, bias_s, slopes)
    row_map = _iota((gw, 1), 0) >> (HEAD_SHIFT - 1)
    eye = _eye_bf16(gw)

    def qtile(i, carry):
        rows_q = pl.ds(pl.multiple_of(i * tile, tile), tile)
        q = p_ref[0, rows_q, 0:gw] * (DIFF_DIM ** -0.5 * LOG2E)
        q_t = _dot_nt(eye, q.astype(BF16)).astype(BF16)
        ys = []
        for pair in range(HEADS // 2):
            streams = [(2 * pair + hh, mp) for hh in range(2) for mp in range(2)]
            qts = [jnp.where(row_map == 2 * h + mp, q_t, jnp.zeros((), BF16)) for h, mp in streams]
            accs = _flash_t(qts, [h for h, _ in streams], slopes, kb_s, vta_s, bias_s, ts_s, i,
                            lambda c, j: None)
            o_m = [a[0:HEAD_DIM] * (1.0 / a[HEAD_DIM:HEAD_DIM + 1]) for a in accs]
            for hh in range(2):
                o_h = o_m[2 * hh] - lam * o_m[2 * hh + 1]
                ms = jnp.mean(o_h * o_h, axis=0, keepdims=True)
                ys.append(o_h * lax.rsqrt(ms + NORM_EPS))
        out = _dot_nt(eye, jnp.concatenate(ys, axis=0).astype(BF16))
        z = p_ref[0, rows_q, 3 * gw:4 * gw]
        o_ref[0, rows_q, :] = (out * ng_ref[...] * (1.0 - lam_init) * _silu(z)).astype(o_ref.dtype)
        return carry

    lax.fori_loop(0, t_len // tile, qtile, 0)


def _diff(layer, p, lq1, lk1, lq2, lk2, ng_exp):
    b, t, _ = p.shape
    gw = GROUP_WIDTH
    assert t % ATTN_TILE == 0
    full = lambda shape: pl.BlockSpec(shape, lambda i: (0,) * len(shape))
    return pl.pallas_call(
        functools.partial(_diff_body, layer),
        out_shape=jax.ShapeDtypeStruct((b, t, gw), BF16),
        grid=(b,),
        in_specs=[pl.BlockSpec((1, t, 4 * gw), lambda i: (i, 0, 0))] + [full((1, DIFF_DIM))] * 4 + [full((1, gw))],
        out_specs=pl.BlockSpec((1, t, gw), lambda i: (i, 0, 0)),
        scratch_shapes=[pltpu.VMEM((t, gw), BF16),
                        pltpu.VMEM((t // ATTN_TILE, HEADS, VT_ROWS, ATTN_TILE), BF16),
                        pltpu.VMEM((HEADS, ATTN_TILE, ATTN_TILE), F32),
                        pltpu.VMEM((FLASH_STREAMS, ATTN_TILE, ATTN_TILE), F32),
                        pltpu.VMEM((FLASH_STREAMS, ATTN_TILE, ATTN_TILE), F32)],
        compiler_params=pltpu.CompilerParams(dimension_semantics=("arbitrary",),
                                             vmem_limit_bytes=VMEM_LIMIT),
        name="diffattn",
    )(p, lq1, lk1, lq2, lk2, ng_exp)


def kernel(x, pre_norm_g, post_norm_g, w_in, conv_w, gdn_a_log, gdn_dt_bias, gdn_norm_g,
           hgrn_lb, hgrn_norm_g, diff_lq1, diff_lk1, diff_lq2, diff_lk2, diff_norm_g, w_out):
    b, t, d = x.shape
    depth = w_in.shape[0]
    gw = GROUP_WIDTH
    x2 = x.astype(F32).reshape(b * t, d)
    row = lambda a: a.astype(F32).reshape(1, -1)
    per_head = lambda a: jnp.repeat(a.astype(F32), HEAD_DIM).reshape(1, gw)
    per_dim = lambda a: jnp.tile(a.astype(F32), HEADS).reshape(1, gw)
    lbp = hgrn_lb.astype(F32)
    for l in range(depth):
        p_gdn, p_ab, p_hgrn, p_moba, p_diff = _inproj(x2, row(pre_norm_g[l]), _reorder_w_in(w_in[l]))
        shp = lambda a: a.reshape(b, t, a.shape[-1])
        o_gdn = _gdn(shp(p_gdn), shp(p_ab), conv_w[l].astype(F32), per_head(gdn_a_log[l]),
                     per_head(gdn_dt_bias[l]), per_dim(gdn_norm_g[l]))
        o_hgrn = _hgrn(l, shp(p_hgrn), lbp, per_dim(hgrn_norm_g[l]))
        o_moba = _moba(shp(p_moba))
        o_diff = _diff(l, shp(p_diff), row(diff_lq1[l]), row(diff_lk1[l]), row(diff_lq2[l]),
                       row(diff_lk2[l]), per_dim(diff_norm_g[l]))
        os_ = [o.reshape(b * t, gw) for o in (o_gdn, o_hgrn, o_moba, o_diff)]
        x2 = _outproj(os_, x2, w_out[l].astype(BF16), row(post_norm_g[l]))
    return x2.reshape(b, t, d)
```

```python
import functools
import math

import jax
import jax.numpy as jnp
from jax import lax
from jax.experimental import pallas as pl
from jax.experimental.pallas import tpu as pltpu

F32 = jnp.float32
BF16 = jnp.bfloat16

HEADS = 4
HEAD_DIM = 64
GROUP_WIDTH = HEADS * HEAD_DIM
HEAD_SHIFT = 6
GDN_CONV = 4
GDN_CHUNK = 64
GDN_CHUNK_SHIFT = 6
GDN_GROUP = 4
HGRN_CHUNK = 16
HGRN_CHUNK_SHIFT = 4
HGRN_TILE = 128
HGRN_GROUP = 2
MOBA_BLOCK = 256
MOBA_TOPK = 3
DIFF_DIM = HEAD_DIM // 2
ATTN_TILE = 256
NORM_EPS = 1e-6
NEG_INF = float("-inf")

GDN_COLS = 4 * GROUP_WIDTH + 2 * HEADS
SEG_WIDTHS = (4 * GROUP_WIDTH, 2 * GROUP_WIDTH, 4 * GROUP_WIDTH, 4 * GROUP_WIDTH, 4 * GROUP_WIDTH)
VMEM_LIMIT = 56 * 1024 * 1024


def _iota(shape, dim):
    return lax.broadcasted_iota(jnp.int32, shape, dim)


def _dot(a, b):
    return jnp.dot(a, b, preferred_element_type=F32)


def _dot_nt(a, b):
    return lax.dot_general(a, b, (((1,), (1,)), ((), ())), preferred_element_type=F32)


def _dot_tn(a, b):
    return lax.dot_general(a, b, (((0,), (0,)), ((), ())), preferred_element_type=F32)


def _split(x, n):
    parts = []
    r = x
    for i in range(n):
        p = r.astype(BF16)
        parts.append(p)
        if i + 1 < n:
            r = r - p.astype(F32)
    return parts


def _sigmoid(x):
    return 1.0 / (1.0 + jnp.exp(-x))


def _silu(x):
    return x * _sigmoid(x)


def _softplus(x):
    return jnp.maximum(x, 0.0) + jnp.log1p(jnp.exp(-jnp.abs(x)))


def _head_block_mask(rows, cols):
    return (_iota((rows, cols), 0) >> HEAD_SHIFT) == (_iota((rows, cols), 1) >> HEAD_SHIFT)


def _head_sum(x, ones_bd):
    hi, lo = _split(x, 2)
    return _dot(hi, ones_bd) + _dot(lo, ones_bd)


def _inproj_body(x_ref, g_ref, w_ref, *o_refs):
    x = x_ref[...]
    ms = jnp.mean(x * x, axis=-1, keepdims=True)
    h = (x * lax.rsqrt(ms + NORM_EPS) * g_ref[...]).astype(BF16)
    off = 0
    for o in o_refs:
        n = o.shape[-1]
        o[...] = _dot(h, w_ref[:, off:off + n]).astype(o.dtype)
        off += n


def _inproj(x2, g, w, tm=256):
    m, d = x2.shape
    n = w.shape[1]
    assert n == sum(SEG_WIDTHS) and m % tm == 0
    return pl.pallas_call(
        _inproj_body,
        out_shape=[jax.ShapeDtypeStruct((m, s), F32) for s in SEG_WIDTHS],
        grid=(m // tm,),
        in_specs=[pl.BlockSpec((tm, d), lambda i: (i, 0)),
                  pl.BlockSpec((1, d), lambda i: (0, 0)),
                  pl.BlockSpec((d, n), lambda i: (0, 0))],
        out_specs=[pl.BlockSpec((tm, s), lambda i: (i, 0)) for s in SEG_WIDTHS],
        compiler_params=pltpu.CompilerParams(dimension_semantics=("arbitrary",),
                                             vmem_limit_bytes=VMEM_LIMIT),
        name="inproj",
    )(x2, g, w)


def _reorder_w_in(w):
    gw = GROUP_WIDTH
    q, k, v = w[:, 0:gw], w[:, gw:2 * gw], w[:, 2 * gw:3 * gw]
    a = w[:, 3 * gw:3 * gw + HEADS]
    b = w[:, 3 * gw + HEADS:3 * gw + 2 * HEADS]
    z = w[:, 3 * gw + 2 * HEADS:GDN_COLS]
    a_exp = jnp.repeat(a, HEAD_DIM, axis=1)
    b_exp = jnp.repeat(b, HEAD_DIM, axis=1)
    return jnp.concatenate([q, k, v, z, a_exp, b_exp, w[:, GDN_COLS:]], axis=1).astype(BF16)


def _outproj_body(o1, o2, o3, o4, x_ref, w_ref, g_ref, out_ref):
    gw = GROUP_WIDTH
    y = _dot(o1[...], w_ref[0:gw, :])
    y = y + _dot(o2[...], w_ref[gw:2 * gw, :])
    y = y + _dot(o3[...], w_ref[2 * gw:3 * gw, :])
    y = y + _dot(o4[...], w_ref[3 * gw:4 * gw, :])
    ms = jnp.mean(y * y, axis=-1, keepdims=True)
    out_ref[...] = x_ref[...] + y * lax.rsqrt(ms + NORM_EPS) * g_ref[...]


def _outproj(os_, x2, w, g, tm=512):
    m, d = x2.shape
    gw = GROUP_WIDTH
    return pl.pallas_call(
        _outproj_body,
        out_shape=jax.ShapeDtypeStruct((m, d), F32),
        grid=(m // tm,),
        in_specs=[pl.BlockSpec((tm, gw), lambda i: (i, 0))] * 4 + [
            pl.BlockSpec((tm, d), lambda i: (i, 0)),
            pl.BlockSpec((d, d), lambda i: (0, 0)),
            pl.BlockSpec((1, d), lambda i: (0, 0))],
        out_specs=pl.BlockSpec((tm, d), lambda i: (i, 0)),
        compiler_params=pltpu.CompilerParams(dimension_semantics=("arbitrary",),
                                             vmem_limit_bytes=VMEM_LIMIT),
        name="outproj",
    )(*os_, x2, w, g)


def _gdn_body(p_ref, ab_ref, cw_ref, alog_ref, dtb_ref, ng_ref, o_ref,
              qdec_s, kdec_s, w_s, u_s, aqk_s, dl_s, state_s, oraw_s):
    gw = GROUP_WIDTH
    t_len = p_ref.shape[1]
    c_len = GDN_CHUNK
    n_chunks = t_len // c_len

    cw = cw_ref[...]
    a_neg = -jnp.exp(alog_ref[...])
    dtb = dtb_ref[...]
    row = _iota((c_len, gw), 0)
    s_idx = _iota((c_len, gw), 1) & (HEAD_DIM - 1)
    lower = s_idx <= row
    strict = s_idx < row
    eye = s_idx == row
    bd = _head_block_mask(gw, gw)
    ones_bd = jnp.where(bd, 1.0, 0.0).astype(BF16)
    row8 = _iota((8, 3 * gw), 0)

    def block_diag(x):
        return jnp.where(bd, jnp.concatenate([x] * HEADS, axis=0), jnp.zeros((), x.dtype))

    def mm_pairs(left, right):
        return _dot(left.astype(BF16), block_diag(right.astype(BF16)))

    group = GDN_GROUP
    g_rows = group * c_len
    r_g = _iota((g_rows, g_rows), 0)
    c_g = _iota((g_rows, g_rows), 1)
    tril_g = jnp.where((c_g <= r_g) & ((r_g >> GDN_CHUNK_SHIFT) == (c_g >> GDN_CHUNK_SHIFT)),
                       1.0, 0.0).astype(BF16)

    def pre(t, tail):
        r0 = pl.multiple_of(t * g_rows, g_rows)
        x = p_ref[0, pl.ds(r0, g_rows), 0:3 * gw]
        acc = x * cw[GDN_CONV - 1:GDN_CONV, :]
        for j in range(GDN_CONV - 1):
            shift = GDN_CONV - 1 - j
            rolled = pltpu.roll(x, shift, axis=0)
            tail_r = pltpu.roll(tail, shift, axis=0)
            head_rows = jnp.where(row8 < shift, tail_r, rolled[0:8])
            shifted = jnp.concatenate([head_rows, rolled[8:]], axis=0)
            acc = acc + shifted * cw[j:j + 1, :]
        y = _silu(acc)
        q = y[:, 0:gw]
        k = y[:, gw:2 * gw]
        v = y[:, 2 * gw:3 * gw]
        q = q * lax.rsqrt(_head_sum(q * q, ones_bd) + NORM_EPS) * (HEAD_DIM ** -0.5)
        k = k * lax.rsqrt(_head_sum(k * k, ones_bd) + NORM_EPS)

        a = ab_ref[0, pl.ds(r0, g_rows), 0:gw]
        b = ab_ref[0, pl.ds(r0, g_rows), gw:2 * gw]
        g = a_neg * _softplus(a + dtb)
        beta = _sigmoid(b)
        g3 = _split(g, 3)
        gc = _dot(tril_g, g3[0]) + _dot(tril_g, g3[1]) + _dot(tril_g, g3[2])
        eg = jnp.exp(gc)
        kb = k * beta
        vb = (v * beta).astype(BF16)
        kbg = (kb * eg).astype(BF16)
        qdec_s[pl.ds(r0, g_rows), :] = (q * eg).astype(BF16)
        lhs = kb.astype(BF16), q.astype(BF16), k.astype(BF16)

        ms, ts, aqks = [], [], []
        for j in range(group):
            sl = slice(j * c_len, (j + 1) * c_len)
            gc_j = gc[sl]
            g_row = jnp.sum(jnp.where(eye, gc_j, 0.0), axis=0, keepdims=True)
            decay = jnp.exp(jnp.where(lower, gc_j - g_row, NEG_INF))
            g_last = gc_j[c_len - 1:c_len, :]
            sc = _dot_nt(jnp.concatenate([lhs[0][sl], lhs[1][sl]], axis=0), block_diag(lhs[2][sl]))
            m = -jnp.where(strict, sc[0:c_len] * decay, 0.0)
            ms.append(m)
            ts.append(jnp.where(eye, 1.0, 0.0) + m)
            rows = pl.ds(r0 + j * c_len, c_len)
            aqk_s[rows, :] = (sc[c_len:2 * c_len] * decay).astype(BF16)
            kdec_s[rows, :] = (k[sl] * jnp.exp(g_last - gc_j)).astype(BF16)
            dl_s[pl.ds(t * group + j, 1), :] = jnp.exp(g_last)

        for _ in range(5):
            ms = [mm_pairs(m, m) for m in ms]
            ts = [t_inv + mm_pairs(m, t_inv) for m, t_inv in zip(ms, ts)]
        for j in range(group):
            sl = slice(j * c_len, (j + 1) * c_len)
            rows = pl.ds(r0 + j * c_len, c_len)
            t_b = ts[j].astype(BF16)
            u_s[rows, :] = _dot(t_b, block_diag(vb[sl]))
            w_s[rows, :] = _dot(t_b, block_diag(kbg[sl])).astype(BF16)
        return x[g_rows - 8:g_rows, :]

    lax.fori_loop(0, n_chunks // group, pre, jnp.zeros((8, 3 * gw), F32))

    state_s[...] = jnp.zeros_like(state_s)

    def seq(c, carry):
        finish(jnp.maximum(c - 1, 0))
        r0 = pl.multiple_of(c * c_len, c_len)
        rows = pl.ds(r0, c_len)
        state = state_s[...]
        ws = _dot(jnp.concatenate([w_s[rows, :], qdec_s[rows, :]], axis=0), state.astype(BF16))
        v_new = (u_s[rows, :] - ws[0:c_len]).astype(BF16)
        oraw_s[rows, :] = ws[c_len:2 * c_len] + _dot(aqk_s[rows, :], block_diag(v_new))
        kv = _dot_tn(kdec_s[rows, :], v_new)
        state_s[...] = state * dl_s[pl.ds(c, 1), :] + jnp.where(bd, kv, 0.0)
        return carry

    def finish(c):
        rows = pl.ds(pl.multiple_of(c * c_len, c_len), c_len)
        o = oraw_s[rows, :]
        z = p_ref[0, rows, 3 * gw:4 * gw]
        ms = _head_sum(o * o, ones_bd) * (1.0 / HEAD_DIM)
        o_ref[0, rows, :] = (o * lax.rsqrt(ms + NORM_EPS) * ng_ref[...] * _silu(z)).astype(o_ref.dtype)

    oraw_s[0:c_len, :] = jnp.zeros((c_len, gw), F32)
    lax.fori_loop(0, n_chunks, seq, 0)
    finish(n_chunks - 1)


def _gdn(p, ab, cw, alog_exp, dtb_exp, ng_exp):
    b, t, _ = p.shape
    gw = GROUP_WIDTH
    full = lambda shape: pl.BlockSpec(shape, lambda i: (0,) * len(shape))
    return pl.pallas_call(
        _gdn_body,
        out_shape=jax.ShapeDtypeStruct((b, t, gw), BF16),
        grid=(b,),
        in_specs=[pl.BlockSpec((1, t, 4 * gw), lambda i: (i, 0, 0)),
                  pl.BlockSpec((1, t, 2 * gw), lambda i: (i, 0, 0)),
                  full((GDN_CONV, 3 * gw)), full((1, gw)), full((1, gw)), full((1, gw))],
        out_specs=pl.BlockSpec((1, t, gw), lambda i: (i, 0, 0)),
        scratch_shapes=[pltpu.VMEM((t, gw), BF16), pltpu.VMEM((t, gw), BF16), pltpu.VMEM((t, gw), BF16),
                        pltpu.VMEM((t, gw), F32), pltpu.VMEM((t, gw), BF16),
                        pltpu.VMEM((t // GDN_CHUNK, gw), F32), pltpu.VMEM((gw, gw), F32),
                        pltpu.VMEM((t, gw), F32)],
        compiler_params=pltpu.CompilerParams(dimension_semantics=("arbitrary",),
                                             vmem_limit_bytes=VMEM_LIMIT),
        name="gdn",
    )(p, ab, cw, alog_exp, dtb_exp, ng_exp)


def _hgrn_body(layer, p_ref, lbp_ref, ng_ref, o_ref, qdec_s, kdec_s, oi_s, dl_s, state_s, oraw_s):
    gw = GROUP_WIDTH
    t_len = p_ref.shape[1]
    c_len = HGRN_CHUNK
    tile = HGRN_TILE
    chunks_per_tile = tile // c_len

    lbp = lbp_ref[...]
    e = jnp.exp(lbp - jnp.max(lbp, axis=0, keepdims=True))
    sm = e / jnp.sum(e, axis=0, keepdims=True)
    if layer == 0:
        lb = jnp.zeros((1, gw), F32)
    else:
        lb = jnp.sum(sm[1:layer + 1], axis=0, keepdims=True)
    log_lb = jnp.log(lb)
    log_1m = jnp.log1p(-lb)

    r = _iota((tile, tile), 0)
    c = _iota((tile, tile), 1)
    same = (r >> HGRN_CHUNK_SHIFT) == (c >> HGRN_CHUNK_SHIFT)
    mid = ((r >> HGRN_CHUNK_SHIFT) << HGRN_CHUNK_SHIFT) + c_len // 2
    causal = same & (c <= r)
    m_cum = jnp.where(causal, 1.0, 0.0)
    m_mid = (jnp.where(same & (c > mid) & (c <= r), 1.0, 0.0)
             - jnp.where(same & (c > r) & (c <= mid), 1.0, 0.0))
    m_rest = jnp.where(same & (c > r), 1.0, 0.0)
    m_tot = jnp.where((_iota((chunks_per_tile, tile), 1) >> HGRN_CHUNK_SHIFT)
                      == _iota((chunks_per_tile, tile), 0), 1.0, 0.0)
    mats = jnp.concatenate([m_cum, m_mid, m_rest, m_tot], axis=0).astype(BF16)
    lane_head = _iota((1, gw), 1) >> HEAD_SHIFT
    bd = _head_block_mask(gw, gw)
    ones_bd = jnp.where(bd, 1.0, 0.0).astype(BF16)

    group = HGRN_GROUP

    def pre(t, carry):
        tiles = []
        for j in range(group):
            rows = pl.ds(pl.multiple_of((t * group + j) * tile, tile), tile)
            q = p_ref[0, rows, 0:gw]
            f = p_ref[0, rows, gw:2 * gw]
            log_sig = jnp.minimum(f, 0.0) - jnp.log1p(jnp.exp(-jnp.abs(f)))
            b_ = log_1m + log_sig
            log_f = jnp.maximum(log_lb, b_) + jnp.log1p(jnp.exp(-jnp.abs(log_lb - b_)))
            tiles.append(dict(rows=rows, k=(1.0 - lb) * _sigmoid(-f), qd=_silu(q), f3=_split(log_f, 3),
                              vb=p_ref[0, rows, 2 * gw:3 * gw].astype(BF16)))
        for j, tl in enumerate(tiles):
            f3 = tl["f3"]
            g_all = _dot(mats, f3[0]) + _dot(mats, f3[1]) + _dot(mats, f3[2])
            g_mid = g_all[tile:2 * tile]
            tl["qa"] = tl["qd"] * jnp.exp(g_mid)
            tl["ka"] = (tl["k"] * jnp.exp(-g_mid)).astype(BF16)
            qdec_s[tl["rows"], :] = (tl["qd"] * jnp.exp(g_all[0:tile])).astype(BF16)
            kdec_s[tl["rows"], :] = (tl["k"] * jnp.exp(g_all[2 * tile:3 * tile])).astype(BF16)
            c0 = pl.multiple_of((t * group + j) * chunks_per_tile, chunks_per_tile)
            dl_s[pl.ds(c0, chunks_per_tile), :] = jnp.exp(g_all[3 * tile:3 * tile + chunks_per_tile])
        scores = [[_dot_nt(jnp.where(lane_head == h, tl["qa"], 0.0).astype(BF16), tl["ka"])
                   for h in range(HEADS)] for tl in tiles]
        for tl, sc in zip(tiles, scores):
            oi = jnp.zeros((tile, gw), F32)
            for h in range(HEADS):
                a = jnp.where(causal, sc[h], 0.0).astype(BF16)
                oi = oi + jnp.where(lane_head == h, _dot(a, tl["vb"]), 0.0)
            oi_s[tl["rows"], :] = oi
        return carry

    lax.fori_loop(0, t_len // (tile * group), pre, 0)

    state_s[...] = jnp.zeros_like(state_s)

    def seq(t, carry):
        finish(jnp.maximum(t - 1, 0))
        rows = pl.ds(pl.multiple_of(t * tile, tile), tile)
        kd = kdec_s[rows, :]
        vb = p_ref[0, rows, 2 * gw:3 * gw].astype(BF16)
        qd = qdec_s[rows, :]
        oi = oi_s[rows, :]
        dl = dl_s[pl.ds(pl.multiple_of(t * chunks_per_tile, chunks_per_tile), chunks_per_tile), :]
        chunk = lambda x, ci: x[ci * c_len:(ci + 1) * c_len]
        kvs = [_dot_tn(chunk(vb, ci), chunk(kd, ci)) for ci in range(chunks_per_tile)]
        state_t = state_s[...]
        states = []
        for ci in range(chunks_per_tile):
            states.append(state_t.astype(BF16))
            state_t = state_t * dl[ci:ci + 1, :] + jnp.where(bd, kvs[ci], 0.0)
        state_s[...] = state_t
        outs = [_dot_nt(chunk(qd, ci), states[ci]) + chunk(oi, ci) for ci in range(chunks_per_tile)]
        oraw_s[rows, :] = jnp.concatenate(outs, axis=0)
        return carry

    def finish(t):
        rows = pl.ds(pl.multiple_of(t * tile, tile), tile)
        o = oraw_s[rows, :]
        z = p_ref[0, rows, 3 * gw:4 * gw]
        ms = _head_sum(o * o, ones_bd) * (1.0 / HEAD_DIM)
        o_ref[0, rows, :] = (o * lax.rsqrt(ms + NORM_EPS) * ng_ref[...] * _silu(z)).astype(o_ref.dtype)

    oraw_s[0:tile, :] = jnp.zeros((tile, gw), F32)
    lax.fori_loop(0, t_len // tile, seq, 0)
    finish(t_len // tile - 1)


def _hgrn(layer, p, lbp, ng_exp):
    b, t, _ = p.shape
    gw = GROUP_WIDTH
    depth = lbp.shape[0]
    full = lambda shape: pl.BlockSpec(shape, lambda i: (0,) * len(shape))
    return pl.pallas_call(
        functools.partial(_hgrn_body, layer),
        out_shape=jax.ShapeDtypeStruct((b, t, gw), BF16),
        grid=(b,),
        in_specs=[pl.BlockSpec((1, t, 4 * gw), lambda i: (i, 0, 0)), full((depth, gw)), full((1, gw))],
        out_specs=pl.BlockSpec((1, t, gw), lambda i: (i, 0, 0)),
        scratch_shapes=[pltpu.VMEM((t, gw), BF16), pltpu.VMEM((t, gw), BF16), pltpu.VMEM((t, gw), F32),
                        pltpu.VMEM((t // HGRN_CHUNK, gw), F32), pltpu.VMEM((gw, gw), F32),
                        pltpu.VMEM((t, gw), F32)],
        compiler_params=pltpu.CompilerParams(dimension_semantics=("arbitrary",),
                                             vmem_limit_bytes=VMEM_LIMIT),
        name="hgrn2",
    )(p, lbp, ng_exp)


VT_ROWS = HEAD_DIM + 16
LOG2E = 1.4426950408889634


def _eye_bf16(n):
    return jnp.where(_iota((n, n), 0) == _iota((n, n), 1), 1.0, 0.0).astype(BF16)


def _attn_setup(p_ref, kb_s, vta_s, bias_s, slopes):
    gw = GROUP_WIDTH
    t_len = p_ref.shape[1]
    tile = ATTN_TILE
    eye = _eye_bf16(gw)
    ones_rows = jnp.ones((VT_ROWS - HEAD_DIM, tile), BF16)

    def body(n, carry):
        rows = pl.ds(pl.multiple_of(n * tile, tile), tile)
        kb_s[rows, :] = p_ref[0, rows, gw:2 * gw].astype(BF16)
        v_t = _dot_nt(eye, p_ref[0, rows, 2 * gw:3 * gw].astype(BF16)).astype(BF16)
        for h in range(HEADS):
            vta_s[n, h, 0:HEAD_DIM, :] = v_t[h * HEAD_DIM:(h + 1) * HEAD_DIM]
            vta_s[n, h, HEAD_DIM:VT_ROWS, :] = ones_rows
        return carry

    lax.fori_loop(0, t_len // tile, body, 0)
    key_local = _iota((tile, tile), 0).astype(F32)
    for h in range(HEADS):
        bias_s[h] = key_local * (slopes[h] * LOG2E)


def _flash_t(qts, heads, slopes, kb_s, vta_s, bias_s, ts_s, i, sel_row):
    tile = ATTN_TILE
    half = tile // 2
    n = len(qts)
    causal = _iota((tile, tile), 0) <= _iota((tile, tile), 1)

    def scores(b, slot):
        k_b = kb_s[pl.ds(pl.multiple_of(b * tile, tile), tile), :]
        for c in range(n):
            ts_s[slot][c] = _dot(k_b, qts[c]) + bias_s[heads[c]]

    def absorb(b, slot, ms, accs, own):
        dist = jnp.full((1, half), (i - b) * tile, jnp.int32).astype(F32)
        new_ms, new_accs = [], []
        for c in range(n):
            off = dist * (-slopes[heads[c]] * LOG2E)
            alphas, p_halves = [], []
            for hf in range(2):
                lanes = slice(hf * half, (hf + 1) * half)
                t = ts_s[slot][c, :, lanes]
                if own:
                    t = jnp.where(causal[:, lanes], t, NEG_INF)
                col_max = jnp.max(t, axis=0, keepdims=True)
                sel = None if own else sel_row(c, b, hf)
                if sel is not None:
                    col_max = jnp.where(sel > 0.5, col_max, NEG_INF)
                m_old = ms[2 * c + hf]
                m_new = jnp.maximum(m_old, col_max + off)
                sub = m_new - off
                if sel is not None:
                    sub = jnp.where(sel > 0.5, sub, float("inf"))
                new_ms.append(m_new)
                alphas.append(jnp.exp2(m_old - m_new))
                p_halves.append(jnp.exp2(t - sub).astype(BF16))
            pv = _dot(vta_s[b, heads[c]], jnp.concatenate(p_halves, axis=1))
            for hf in range(2):
                new_accs.append(alphas[hf] * accs[2 * c + hf] + pv[:, hf * half:(hf + 1) * half])
        return tuple(new_ms), tuple(new_accs)

    def pair(p, carry):
        b = 2 * p
        scores(b + 1, 1)
        carry = absorb(b, 0, carry[0], carry[1], own=False)
        scores(b + 2, 0)
        return absorb(b + 1, 1, carry[0], carry[1], own=False)

    def odd_tail(r, carry):
        scores(i, 1)
        carry = absorb(i - 1, 0, carry[0], carry[1], own=False)
        return absorb(i, 1, carry[0], carry[1], own=True)

    def even_tail(r, carry):
        return absorb(i, 0, carry[0], carry[1], own=True)

    scores(0, 0)
    carry = (tuple(jnp.full((1, half), -1e30, F32) for _ in range(2 * n)),
             tuple(jnp.zeros((VT_ROWS, half), F32) for _ in range(2 * n)))
    carry = lax.fori_loop(0, i >> 1, pair, carry)
    carry = lax.fori_loop(0, i & 1, odd_tail, carry)
    carry = lax.fori_loop(0, 1 - (i & 1), even_tail, carry)
    accs = carry[1]
    return [jnp.concatenate([accs[2 * c], accs[2 * c + 1]], axis=1) for c in range(n)]


def _moba_body(p_ref, o_ref, kb_s, vta_s, bias_s, ts_a, ts_b, km_s, sel_s):
    ts_s = (ts_a, ts_b)
    gw = GROUP_WIDTH
    t_len = p_ref.shape[1]
    tile = ATTN_TILE
    nb = t_len // MOBA_BLOCK
    topk = min(MOBA_TOPK, nb)
    slopes = [2.0 ** -(2 * h + 2) for h in range(HEADS)]

    _attn_setup(p_ref, kb_s, vta_s, bias_s, slopes)

    lane_head = _iota((1, gw), 1) >> HEAD_SHIFT

    def means(n, carry):
        rows = pl.ds(pl.multiple_of(n * MOBA_BLOCK, MOBA_BLOCK), MOBA_BLOCK)
        mean = jnp.sum(p_ref[0, rows, gw:2 * gw], axis=0, keepdims=True) * (1.0 / MOBA_BLOCK)
        for h in range(HEADS):
            km_s[pl.ds(h * nb + n, 1), :] = jnp.where(lane_head == h, mean, 0.0)
        return carry

    lax.fori_loop(0, nb, means, 0)
    km3 = _split(km_s[...], 3)
    eye = _eye_bf16(gw)
    row_head = _iota((gw, 1), 0) >> HEAD_SHIFT
    n_idx = _iota((nb, tile), 0)

    def qtile(i, carry):
        rows_q = pl.ds(pl.multiple_of(i * tile, tile), tile)
        q = p_ref[0, rows_q, 0:gw]
        q3 = _split(q, 3)
        gate = (_dot_nt(km3[0], q3[0]) + _dot_nt(km3[0], q3[1]) + _dot_nt(km3[1], q3[0])
                + _dot_nt(km3[1], q3[1]) + _dot_nt(km3[0], q3[2]) + _dot_nt(km3[2], q3[0]))
        past_blk = n_idx < i
        for h in range(HEADS):
            gm = jnp.where(past_blk, gate[h * nb:(h + 1) * nb], NEG_INF)
            cnt = jnp.zeros((nb, tile), F32)
            for n2 in range(nb):
                other = gm[n2:n2 + 1, :]
                beats = (other > gm) | ((other == gm) & (n2 < n_idx))
                cnt = cnt + jnp.where(beats, 1.0, 0.0)
            sel = jnp.where(past_blk & (cnt < topk), 1.0, 0.0)
            for hf in range(2):
                sel_s[hf, h * nb:(h + 1) * nb, :] = sel[:, hf * (tile // 2):(hf + 1) * (tile // 2)]

        q_t = _dot_nt(eye, (q * (HEAD_DIM ** -0.5 * LOG2E)).astype(BF16)).astype(BF16)
        qts = [jnp.where(row_head == h, q_t, jnp.zeros((), BF16)) for h in range(HEADS)]
        accs = _flash_t(qts, list(range(HEADS)), slopes, kb_s, vta_s, bias_s, ts_s, i,
                        lambda c, j, hf: sel_s[hf, pl.ds(c * nb + j, 1), :])
        o_t = jnp.concatenate([a[0:HEAD_DIM] * (1.0 / a[HEAD_DIM:HEAD_DIM + 1]) for a in accs], axis=0)
        out = _dot_nt(eye, o_t.astype(BF16))
        z = p_ref[0, rows_q, 3 * gw:4 * gw]
        o_ref[0, rows_q, :] = (out * _silu(z)).astype(o_ref.dtype)
        return carry

    lax.fori_loop(0, t_len // tile, qtile, 0)


def _moba(p):
    b, t, _ = p.shape
    gw = GROUP_WIDTH
    assert t % MOBA_BLOCK == 0 and MOBA_BLOCK == ATTN_TILE
    return pl.pallas_call(
        _moba_body,
        out_shape=jax.ShapeDtypeStruct((b, t, gw), BF16),
        grid=(b,),
        in_specs=[pl.BlockSpec((1, t, 4 * gw), lambda i: (i, 0, 0))],
        out_specs=pl.BlockSpec((1, t, gw), lambda i: (i, 0, 0)),
        scratch_shapes=[pltpu.VMEM((t, gw), BF16),
                        pltpu.VMEM((t // ATTN_TILE, HEADS, VT_ROWS, ATTN_TILE), BF16),
                        pltpu.VMEM((HEADS, ATTN_TILE, ATTN_TILE), F32),
                        pltpu.VMEM((HEADS, ATTN_TILE, ATTN_TILE), F32),
                        pltpu.VMEM((HEADS, ATTN_TILE, ATTN_TILE), F32),
                        pltpu.VMEM((HEADS * (t // MOBA_BLOCK), gw), F32),
                        pltpu.VMEM((2, HEADS * (t // MOBA_BLOCK), ATTN_TILE // 2), F32)],
        compiler_params=pltpu.CompilerParams(dimension_semantics=("arbitrary",),
                                             vmem_limit_bytes=VMEM_LIMIT),
        name="moba",
    )(p)


def _diff_body(layer, p_ref, lq1_ref, lk1_ref, lq2_ref, lk2_ref, ng_ref, o_ref, kb_s, vta_s, bias_s,
               ts_a, ts_b):
    ts_s = (ts_a, ts_b)
    gw = GROUP_WIDTH
    t_len = p_ref.shape[1]
    tile = ATTN_TILE
    lam_init = 0.8 - 0.6 * math.exp(-0.3 * layer)
    lam = (jnp.exp(jnp.sum(lq1_ref[...] * lk1_ref[...], axis=-1, keepdims=True))
           - jnp.exp(jnp.sum(lq2_ref[...] * lk2_ref[...], axis=-1, keepdims=True)) + lam_init)
    slopes = [2.0 ** -(2 * h + 1) for h in range(HEADS)]

    _attn_setup(p_ref, kb_s, vta_s, bias_s, slopes)
    row_map = _iota((gw, 1), 0) >> (HEAD_SHIFT - 1)
    eye = _eye_bf16(gw)

    def qtile(i, carry):
        rows_q = pl.ds(pl.multiple_of(i * tile, tile), tile)
        q = p_ref[0, rows_q, 0:gw] * (DIFF_DIM ** -0.5 * LOG2E)
        q_t = _dot_nt(eye, q.astype(BF16)).astype(BF16)
        streams = [(h, mp) for h in range(HEADS) for mp in range(2)]
        qts = [jnp.where(row_map == 2 * h + mp, q_t, jnp.zeros((), BF16)) for h, mp in streams]
        accs = _flash_t(qts, [h for h, _ in streams], slopes, kb_s, vta_s, bias_s, ts_s, i,
                        lambda c, j, hf: None)
        o_m = [a[0:HEAD_DIM] * (1.0 / a[HEAD_DIM:HEAD_DIM + 1]) for a in accs]
        ys = []
        for h in range(HEADS):
            o_h = o_m[2 * h] - lam * o_m[2 * h + 1]
            ms = jnp.mean(o_h * o_h, axis=0, keepdims=True)
            ys.append(o_h * lax.rsqrt(ms + NORM_EPS))
        out = _dot_nt(eye, jnp.concatenate(ys, axis=0).astype(BF16))
        z = p_ref[0, rows_q, 3 * gw:4 * gw]
        o_ref[0, rows_q, :] = (out * ng_ref[...] * (1.0 - lam_init) * _silu(z)).astype(o_ref.dtype)
        return carry

    lax.fori_loop(0, t_len // tile, qtile, 0)


def _diff(layer, p, lq1, lk1, lq2, lk2, ng_exp):
    b, t, _ = p.shape
    gw = GROUP_WIDTH
    assert t % ATTN_TILE == 0
    full = lambda shape: pl.BlockSpec(shape, lambda i: (0,) * len(shape))
    return pl.pallas_call(
        functools.partial(_diff_body, layer),
        out_shape=jax.ShapeDtypeStruct((b, t, gw), BF16),
        grid=(b,),
        in_specs=[pl.BlockSpec((1, t, 4 * gw), lambda i: (i, 0, 0))] + [full((1, DIFF_DIM))] * 4 + [full((1, gw))],
        out_specs=pl.BlockSpec((1, t, gw), lambda i: (i, 0, 0)),
        scratch_shapes=[pltpu.VMEM((t, gw), BF16),
                        pltpu.VMEM((t // ATTN_TILE, HEADS, VT_ROWS, ATTN_TILE), BF16),
                        pltpu.VMEM((HEADS, ATTN_TILE, ATTN_TILE), F32),
                        pltpu.VMEM((2 * HEADS, ATTN_TILE, ATTN_TILE), F32),
                        pltpu.VMEM((2 * HEADS, ATTN_TILE, ATTN_TILE), F32)],
        compiler_params=pltpu.CompilerParams(dimension_semantics=("arbitrary",),
                                             vmem_limit_bytes=VMEM_LIMIT),
        name="diffattn",
    )(p, lq1, lk1, lq2, lk2, ng_exp)


def kernel(x, pre_norm_g, post_norm_g, w_in, conv_w, gdn_a_log, gdn_dt_bias, gdn_norm_g,
           hgrn_lb, hgrn_norm_g, diff_lq1, diff_lk1, diff_lq2, diff_lk2, diff_norm_g, w_out):
    b, t, d = x.shape
    depth = w_in.shape[0]
    gw = GROUP_WIDTH
    x2 = x.astype(F32).reshape(b * t, d)
    row = lambda a: a.astype(F32).reshape(1, -1)
    per_head = lambda a: jnp.repeat(a.astype(F32), HEAD_DIM).reshape(1, gw)
    per_dim = lambda a: jnp.tile(a.astype(F32), HEADS).reshape(1, gw)
    lbp = hgrn_lb.astype(F32)
    for l in range(depth):
        p_gdn, p_ab, p_hgrn, p_moba, p_diff = _inproj(x2, row(pre_norm_g[l]), _reorder_w_in(w_in[l]))
        shp = lambda a: a.reshape(b, t, a.shape[-1])
        o_gdn = _gdn(shp(p_gdn), shp(p_ab), conv_w[l].astype(F32), per_head(gdn_a_log[l]),
                     per_head(gdn_dt_bias[l]), per_dim(gdn_norm_g[l]))
        o_hgrn = _hgrn(l, shp(p_hgrn), lbp, per_dim(hgrn_norm_g[l]))
        o_moba = _moba(shp(p_moba))
        o_diff = _diff(l, shp(p_diff), row(diff_lq1[l]), row(diff_lk1[l]), row(diff_lq2[l]),
                       row(diff_lk2[l]), per_dim(diff_norm_g[l]))
        os_ = [o.reshape(b * t, gw) for o in (o_gdn, o_hgrn, o_moba, o_diff)]
        x2 = _outproj(os_, x2, w_out[l].astype(BF16), row(post_norm_g[l]))
    return x2.reshape(b, t, d)
```

```python
import functools
import math

import jax
import jax.numpy as jnp
from jax import lax
from jax.experimental import pallas as pl
from jax.experimental.pallas import tpu as pltpu

F32 = jnp.float32
BF16 = jnp.bfloat16

HEADS = 4
HEAD_DIM = 64
GROUP_WIDTH = HEADS * HEAD_DIM
HEAD_SHIFT = 6
GDN_CONV = 4
GDN_CHUNK = 64
GDN_CHUNK_SHIFT = 6
GDN_GROUP = 4
HGRN_CHUNK = 16
HGRN_CHUNK_SHIFT = 4
HGRN_TILE = 128
HGRN_GROUP = 2
MOBA_BLOCK = 256
MOBA_TOPK = 3
DIFF_DIM = HEAD_DIM // 2
ATTN_TILE = 256
NORM_EPS = 1e-6
NEG_INF = float("-inf")

GDN_COLS = 4 * GROUP_WIDTH + 2 * HEADS
ROW_SEG_WIDTHS = (4 * GROUP_WIDTH, 2 * GROUP_WIDTH, 4 * GROUP_WIDTH, 2 * GROUP_WIDTH, 2 * GROUP_WIDTH)
COL_SEG_WIDTHS = (2 * GROUP_WIDTH, 2 * GROUP_WIDTH)
VMEM_LIMIT = 56 * 1024 * 1024


def _iota(shape, dim):
    return lax.broadcasted_iota(jnp.int32, shape, dim)


def _dot(a, b):
    return jnp.dot(a, b, preferred_element_type=F32)


def _dot_nt(a, b):
    return lax.dot_general(a, b, (((1,), (1,)), ((), ())), preferred_element_type=F32)


def _dot_tn(a, b):
    return lax.dot_general(a, b, (((0,), (0,)), ((), ())), preferred_element_type=F32)


def _split(x, n):
    parts = []
    r = x
    for i in range(n):
        p = r.astype(BF16)
        parts.append(p)
        if i + 1 < n:
            r = r - p.astype(F32)
    return parts


def _sigmoid(x):
    return 1.0 / (1.0 + jnp.exp(-x))


def _silu(x):
    return x * _sigmoid(x)


def _softplus(x):
    return jnp.maximum(x, 0.0) + jnp.log1p(jnp.exp(-jnp.abs(x)))


def _head_block_mask(rows, cols):
    return (_iota((rows, cols), 0) >> HEAD_SHIFT) == (_iota((rows, cols), 1) >> HEAD_SHIFT)


def _head_sum(x, ones_bd):
    hi, lo = _split(x, 2)
    return _dot(hi, ones_bd) + _dot(lo, ones_bd)


def _inproj_body(x_ref, g_ref, w_ref, wt_ref, *o_refs):
    n_t = len(ROW_SEG_WIDTHS)
    x = x_ref[...]
    ms = jnp.mean(x * x, axis=-1, keepdims=True)
    h = (x * lax.rsqrt(ms + NORM_EPS) * g_ref[...]).astype(BF16)
    off = 0
    for o in o_refs[:n_t]:
        n = o.shape[-1]
        o[...] = _dot(h, w_ref[:, off:off + n])
        off += n
    off = 0
    for o in o_refs[n_t:]:
        n = o.shape[1]
        o[0] = _dot_nt(wt_ref[off:off + n, :], h)
        off += n


def _inproj(x2, g, w, wt):
    m, d = x2.shape
    tm = ATTN_TILE
    assert w.shape[1] == sum(ROW_SEG_WIDTHS) and wt.shape[0] == sum(COL_SEG_WIDTHS) and m % tm == 0
    return pl.pallas_call(
        _inproj_body,
        out_shape=([jax.ShapeDtypeStruct((m, s), F32) for s in ROW_SEG_WIDTHS]
                   + [jax.ShapeDtypeStruct((m // tm, s, tm), F32) for s in COL_SEG_WIDTHS]),
        grid=(m // tm,),
        in_specs=[pl.BlockSpec((tm, d), lambda i: (i, 0)),
                  pl.BlockSpec((1, d), lambda i: (0, 0)),
                  pl.BlockSpec(w.shape, lambda i: (0, 0)),
                  pl.BlockSpec(wt.shape, lambda i: (0, 0))],
        out_specs=([pl.BlockSpec((tm, s), lambda i: (i, 0)) for s in ROW_SEG_WIDTHS]
                   + [pl.BlockSpec((1, s, tm), lambda i: (i, 0, 0)) for s in COL_SEG_WIDTHS]),
        compiler_params=pltpu.CompilerParams(dimension_semantics=("arbitrary",),
                                             vmem_limit_bytes=VMEM_LIMIT),
        name="inproj",
    )(x2, g, w, wt)


def _reorder_w_in(w):
    gw = GROUP_WIDTH
    q, k, v = w[:, 0:gw], w[:, gw:2 * gw], w[:, 2 * gw:3 * gw]
    a = w[:, 3 * gw:3 * gw + HEADS]
    b = w[:, 3 * gw + HEADS:3 * gw + 2 * HEADS]
    z = w[:, 3 * gw + 2 * HEADS:GDN_COLS]
    a_exp = jnp.repeat(a, HEAD_DIM, axis=1)
    b_exp = jnp.repeat(b, HEAD_DIM, axis=1)
    hgrn = w[:, GDN_COLS:GDN_COLS + 4 * gw]
    moba = w[:, GDN_COLS + 4 * gw:GDN_COLS + 8 * gw]
    diff = w[:, GDN_COLS + 8 * gw:GDN_COLS + 12 * gw]
    kz = lambda s: jnp.concatenate([s[:, gw:2 * gw], s[:, 3 * gw:4 * gw]], axis=1)
    qv = lambda s: jnp.concatenate([s[:, 0:gw], s[:, 2 * gw:3 * gw]], axis=1)
    rows = jnp.concatenate([q, k, v, z, a_exp, b_exp, hgrn, kz(moba), kz(diff)], axis=1).astype(BF16)
    cols = jnp.concatenate([qv(moba), qv(diff)], axis=1).T.astype(BF16)
    return rows, cols


def _outproj_body(o1, o2, o3, o4, x_ref, w_ref, g_ref, out_ref):
    gw = GROUP_WIDTH
    y = _dot(o1[...], w_ref[0:gw, :])
    y = y + _dot(o2[...], w_ref[gw:2 * gw, :])
    y = y + _dot(o3[...], w_ref[2 * gw:3 * gw, :])
    y = y + _dot(o4[...], w_ref[3 * gw:4 * gw, :])
    ms = jnp.mean(y * y, axis=-1, keepdims=True)
    out_ref[...] = x_ref[...] + y * lax.rsqrt(ms + NORM_EPS) * g_ref[...]


def _outproj(os_, x2, w, g, tm=512):
    m, d = x2.shape
    gw = GROUP_WIDTH
    return pl.pallas_call(
        _outproj_body,
        out_shape=jax.ShapeDtypeStruct((m, d), F32),
        grid=(m // tm,),
        in_specs=[pl.BlockSpec((tm, gw), lambda i: (i, 0))] * 4 + [
            pl.BlockSpec((tm, d), lambda i: (i, 0)),
            pl.BlockSpec((d, d), lambda i: (0, 0)),
            pl.BlockSpec((1, d), lambda i: (0, 0))],
        out_specs=pl.BlockSpec((tm, d), lambda i: (i, 0)),
        compiler_params=pltpu.CompilerParams(dimension_semantics=("arbitrary",),
                                             vmem_limit_bytes=VMEM_LIMIT),
        name="outproj",
    )(*os_, x2, w, g)


def _gdn_body(p_ref, ab_ref, cw_ref, alog_ref, dtb_ref, ng_ref, o_ref,
              qdec_s, kdec_s, w_s, u_s, aqk_s, dl_s, state_s, oraw_s):
    gw = GROUP_WIDTH
    t_len = p_ref.shape[1]
    c_len = GDN_CHUNK
    n_chunks = t_len // c_len

    cw = cw_ref[...]
    a_neg = -jnp.exp(alog_ref[...])
    dtb = dtb_ref[...]
    row = _iota((c_len, gw), 0)
    s_idx = _iota((c_len, gw), 1) & (HEAD_DIM - 1)
    lower = s_idx <= row
    strict = s_idx < row
    eye = s_idx == row
    bd = _head_block_mask(gw, gw)
    ones_bd = jnp.where(bd, 1.0, 0.0).astype(BF16)
    row8 = _iota((8, 3 * gw), 0)

    def block_diag(x):
        return jnp.where(bd, jnp.concatenate([x] * HEADS, axis=0), jnp.zeros((), x.dtype))

    def mm_pairs(left, right):
        return _dot(left.astype(BF16), block_diag(right.astype(BF16)))

    group = GDN_GROUP
    g_rows = group * c_len
    r_g = _iota((g_rows, g_rows), 0)
    c_g = _iota((g_rows, g_rows), 1)
    tril_g = jnp.where((c_g <= r_g) & ((r_g >> GDN_CHUNK_SHIFT) == (c_g >> GDN_CHUNK_SHIFT)),
                       1.0, 0.0).astype(BF16)

    def pre(t, tail):
        r0 = pl.multiple_of(t * g_rows, g_rows)
        x = p_ref[0, pl.ds(r0, g_rows), 0:3 * gw]
        acc = x * cw[GDN_CONV - 1:GDN_CONV, :]
        for j in range(GDN_CONV - 1):
            shift = GDN_CONV - 1 - j
            rolled = pltpu.roll(x, shift, axis=0)
            tail_r = pltpu.roll(tail, shift, axis=0)
            head_rows = jnp.where(row8 < shift, tail_r, rolled[0:8])
            shifted = jnp.concatenate([head_rows, rolled[8:]], axis=0)
            acc = acc + shifted * cw[j:j + 1, :]
        y = _silu(acc)
        q = y[:, 0:gw]
        k = y[:, gw:2 * gw]
        v = y[:, 2 * gw:3 * gw]
        q = q * lax.rsqrt(_head_sum(q * q, ones_bd) + NORM_EPS) * (HEAD_DIM ** -0.5)
        k = k * lax.rsqrt(_head_sum(k * k, ones_bd) + NORM_EPS)

        a = ab_ref[0, pl.ds(r0, g_rows), 0:gw]
        b = ab_ref[0, pl.ds(r0, g_rows), gw:2 * gw]
        g = a_neg * _softplus(a + dtb)
        beta = _sigmoid(b)
        g3 = _split(g, 3)
        gc = _dot(tril_g, g3[0]) + _dot(tril_g, g3[1]) + _dot(tril_g, g3[2])
        eg = jnp.exp(gc)
        kb = k * beta
        vb = (v * beta).astype(BF16)
        kbg = (kb * eg).astype(BF16)
        qdec_s[pl.ds(r0, g_rows), :] = (q * eg).astype(BF16)
        lhs = kb.astype(BF16), q.astype(BF16), k.astype(BF16)

        ms, ts, aqks = [], [], []
        for j in range(group):
            sl = slice(j * c_len, (j + 1) * c_len)
            gc_j = gc[sl]
            g_row = jnp.sum(jnp.where(eye, gc_j, 0.0), axis=0, keepdims=True)
            decay = jnp.exp(jnp.where(lower, gc_j - g_row, NEG_INF))
            g_last = gc_j[c_len - 1:c_len, :]
            sc = _dot_nt(jnp.concatenate([lhs[0][sl], lhs[1][sl]], axis=0), block_diag(lhs[2][sl]))
            m = -jnp.where(strict, sc[0:c_len] * decay, 0.0)
            ms.append(m)
            ts.append(jnp.where(eye, 1.0, 0.0) + m)
            rows = pl.ds(r0 + j * c_len, c_len)
            aqk_s[rows, :] = (sc[c_len:2 * c_len] * decay).astype(BF16)
            kdec_s[rows, :] = (k[sl] * jnp.exp(g_last - gc_j)).astype(BF16)
            dl_s[pl.ds(t * group + j, 1), :] = jnp.exp(g_last)

        for _ in range(5):
            ms = [mm_pairs(m, m) for m in ms]
            ts = [t_inv + mm_pairs(m, t_inv) for m, t_inv in zip(ms, ts)]
        for j in range(group):
            sl = slice(j * c_len, (j + 1) * c_len)
            rows = pl.ds(r0 + j * c_len, c_len)
            t_b = ts[j].astype(BF16)
            u_s[rows, :] = _dot(t_b, block_diag(vb[sl]))
            w_s[rows, :] = _dot(t_b, block_diag(kbg[sl])).astype(BF16)
        return x[g_rows - 8:g_rows, :]

    lax.fori_loop(0, n_chunks // group, pre, jnp.zeros((8, 3 * gw), F32))

    state_s[...] = jnp.zeros_like(state_s)

    def seq(c, carry):
        finish(jnp.maximum(c - 1, 0))
        r0 = pl.multiple_of(c * c_len, c_len)
        rows = pl.ds(r0, c_len)
        state = state_s[...]
        ws = _dot(jnp.concatenate([w_s[rows, :], qdec_s[rows, :]], axis=0), state.astype(BF16))
        v_new = (u_s[rows, :] - ws[0:c_len]).astype(BF16)
        oraw_s[rows, :] = ws[c_len:2 * c_len] + _dot(aqk_s[rows, :], block_diag(v_new))
        kv = _dot_tn(kdec_s[rows, :], v_new)
        state_s[...] = state * dl_s[pl.ds(c, 1), :] + jnp.where(bd, kv, 0.0)
        return carry

    def finish(c):
        rows = pl.ds(pl.multiple_of(c * c_len, c_len), c_len)
        o = oraw_s[rows, :]
        z = p_ref[0, rows, 3 * gw:4 * gw]
        ms = _head_sum(o * o, ones_bd) * (1.0 / HEAD_DIM)
        o_ref[0, rows, :] = (o * lax.rsqrt(ms + NORM_EPS) * ng_ref[...] * _silu(z)).astype(o_ref.dtype)

    oraw_s[0:c_len, :] = jnp.zeros((c_len, gw), F32)
    lax.fori_loop(0, n_chunks, seq, 0)
    finish(n_chunks - 1)


def _gdn(p, ab, cw, alog_exp, dtb_exp, ng_exp):
    b, t, _ = p.shape
    gw = GROUP_WIDTH
    full = lambda shape: pl.BlockSpec(shape, lambda i: (0,) * len(shape))
    return pl.pallas_call(
        _gdn_body,
        out_shape=jax.ShapeDtypeStruct((b, t, gw), BF16),
        grid=(b,),
        in_specs=[pl.BlockSpec((1, t, 4 * gw), lambda i: (i, 0, 0)),
                  pl.BlockSpec((1, t, 2 * gw), lambda i: (i, 0, 0)),
                  full((GDN_CONV, 3 * gw)), full((1, gw)), full((1, gw)), full((1, gw))],
        out_specs=pl.BlockSpec((1, t, gw), lambda i: (i, 0, 0)),
        scratch_shapes=[pltpu.VMEM((t, gw), BF16), pltpu.VMEM((t, gw), BF16), pltpu.VMEM((t, gw), BF16),
                        pltpu.VMEM((t, gw), F32), pltpu.VMEM((t, gw), BF16),
                        pltpu.VMEM((t // GDN_CHUNK, gw), F32), pltpu.VMEM((gw, gw), F32),
                        pltpu.VMEM((t, gw), F32)],
        compiler_params=pltpu.CompilerParams(dimension_semantics=("arbitrary",),
                                             vmem_limit_bytes=VMEM_LIMIT),
        name="gdn",
    )(p, ab, cw, alog_exp, dtb_exp, ng_exp)


def _hgrn_body(layer, p_ref, lbp_ref, ng_ref, o_ref, qdec_s, kdec_s, oi_s, dl_s, state_s, oraw_s):
    gw = GROUP_WIDTH
    t_len = p_ref.shape[1]
    c_len = HGRN_CHUNK
    tile = HGRN_TILE
    chunks_per_tile = tile // c_len

    lbp = lbp_ref[...]
    e = jnp.exp(lbp - jnp.max(lbp, axis=0, keepdims=True))
    sm = e / jnp.sum(e, axis=0, keepdims=True)
    if layer == 0:
        lb = jnp.zeros((1, gw), F32)
    else:
        lb = jnp.sum(sm[1:layer + 1], axis=0, keepdims=True)
    log_lb = jnp.log(lb)
    log_1m = jnp.log1p(-lb)

    r = _iota((tile, tile), 0)
    c = _iota((tile, tile), 1)
    same = (r >> HGRN_CHUNK_SHIFT) == (c >> HGRN_CHUNK_SHIFT)
    mid = ((r >> HGRN_CHUNK_SHIFT) << HGRN_CHUNK_SHIFT) + c_len // 2
    causal = same & (c <= r)
    m_cum = jnp.where(causal, 1.0, 0.0)
    m_mid = (jnp.where(same & (c > mid) & (c <= r), 1.0, 0.0)
             - jnp.where(same & (c > r) & (c <= mid), 1.0, 0.0))
    m_rest = jnp.where(same & (c > r), 1.0, 0.0)
    m_tot = jnp.where((_iota((chunks_per_tile, tile), 1) >> HGRN_CHUNK_SHIFT)
                      == _iota((chunks_per_tile, tile), 0), 1.0, 0.0)
    mats = jnp.concatenate([m_cum, m_mid, m_rest, m_tot], axis=0).astype(BF16)
    lane_head = _iota((1, gw), 1) >> HEAD_SHIFT
    bd = _head_block_mask(gw, gw)
    ones_bd = jnp.where(bd, 1.0, 0.0).astype(BF16)

    group = HGRN_GROUP

    def pre(t, carry):
        tiles = []
        for j in range(group):
            rows = pl.ds(pl.multiple_of((t * group + j) * tile, tile), tile)
            q = p_ref[0, rows, 0:gw]
            f = p_ref[0, rows, gw:2 * gw]
            log_sig = jnp.minimum(f, 0.0) - jnp.log1p(jnp.exp(-jnp.abs(f)))
            b_ = log_1m + log_sig
            log_f = jnp.maximum(log_lb, b_) + jnp.log1p(jnp.exp(-jnp.abs(log_lb - b_)))
            tiles.append(dict(rows=rows, k=(1.0 - lb) * _sigmoid(-f), qd=_silu(q), f3=_split(log_f, 3),
                              vb=p_ref[0, rows, 2 * gw:3 * gw].astype(BF16)))
        for j, tl in enumerate(tiles):
            f3 = tl["f3"]
            g_all = _dot(mats, f3[0]) + _dot(mats, f3[1]) + _dot(mats, f3[2])
            g_mid = g_all[tile:2 * tile]
            tl["qa"] = tl["qd"] * jnp.exp(g_mid)
            tl["ka"] = (tl["k"] * jnp.exp(-g_mid)).astype(BF16)
            qdec_s[tl["rows"], :] = (tl["qd"] * jnp.exp(g_all[0:tile])).astype(BF16)
            kdec_s[tl["rows"], :] = (tl["k"] * jnp.exp(g_all[2 * tile:3 * tile])).astype(BF16)
            c0 = pl.multiple_of((t * group + j) * chunks_per_tile, chunks_per_tile)
            dl_s[pl.ds(c0, chunks_per_tile), :] = jnp.exp(g_all[3 * tile:3 * tile + chunks_per_tile])
        scores = [[_dot_nt(jnp.where(lane_head == h, tl["qa"], 0.0).astype(BF16), tl["ka"])
                   for h in range(HEADS)] for tl in tiles]
        for tl, sc in zip(tiles, scores):
            oi = jnp.zeros((tile, gw), F32)
            for h in range(HEADS):
                a = jnp.where(causal, sc[h], 0.0).astype(BF16)
                oi = oi + jnp.where(lane_head == h, _dot(a, tl["vb"]), 0.0)
            oi_s[tl["rows"], :] = oi
        return carry

    lax.fori_loop(0, t_len // (tile * group), pre, 0)

    state_s[...] = jnp.zeros_like(state_s)

    def seq(t, carry):
        finish(jnp.maximum(t - 1, 0))
        rows = pl.ds(pl.multiple_of(t * tile, tile), tile)
        kd = kdec_s[rows, :]
        vb = p_ref[0, rows, 2 * gw:3 * gw].astype(BF16)
        qd = qdec_s[rows, :]
        oi = oi_s[rows, :]
        dl = dl_s[pl.ds(pl.multiple_of(t * chunks_per_tile, chunks_per_tile), chunks_per_tile), :]
        chunk = lambda x, ci: x[ci * c_len:(ci + 1) * c_len]
        kvs = [_dot_tn(chunk(vb, ci), chunk(kd, ci)) for ci in range(chunks_per_tile)]
        state_t = state_s[...]
        states = []
        for ci in range(chunks_per_tile):
            states.append(state_t.astype(BF16))
            state_t = state_t * dl[ci:ci + 1, :] + jnp.where(bd, kvs[ci], 0.0)
        state_s[...] = state_t
        outs = [_dot_nt(chunk(qd, ci), states[ci]) + chunk(oi, ci) for ci in range(chunks_per_tile)]
        oraw_s[rows, :] = jnp.concatenate(outs, axis=0)
        return carry

    def finish(t):
        rows = pl.ds(pl.multiple_of(t * tile, tile), tile)
        o = oraw_s[rows, :]
        z = p_ref[0, rows, 3 * gw:4 * gw]
        ms = _head_sum(o * o, ones_bd) * (1.0 / HEAD_DIM)
        o_ref[0, rows, :] = (o * lax.rsqrt(ms + NORM_EPS) * ng_ref[...] * _silu(z)).astype(o_ref.dtype)

    oraw_s[0:tile, :] = jnp.zeros((tile, gw), F32)
    lax.fori_loop(0, t_len // tile, seq, 0)
    finish(t_len // tile - 1)


def _hgrn(layer, p, lbp, ng_exp):
    b, t, _ = p.shape
    gw = GROUP_WIDTH
    depth = lbp.shape[0]
    full = lambda shape: pl.BlockSpec(shape, lambda i: (0,) * len(shape))
    return pl.pallas_call(
        functools.partial(_hgrn_body, layer),
        out_shape=jax.ShapeDtypeStruct((b, t, gw), BF16),
        grid=(b,),
        in_specs=[pl.BlockSpec((1, t, 4 * gw), lambda i: (i, 0, 0)), full((depth, gw)), full((1, gw))],
        out_specs=pl.BlockSpec((1, t, gw), lambda i: (i, 0, 0)),
        scratch_shapes=[pltpu.VMEM((t, gw), BF16), pltpu.VMEM((t, gw), BF16), pltpu.VMEM((t, gw), F32),
                        pltpu.VMEM((t // HGRN_CHUNK, gw), F32), pltpu.VMEM((gw, gw), F32),
                        pltpu.VMEM((t, gw), F32)],
        compiler_params=pltpu.CompilerParams(dimension_semantics=("arbitrary",),
                                             vmem_limit_bytes=VMEM_LIMIT),
        name="hgrn2",
    )(p, lbp, ng_exp)


VT_ROWS = HEAD_DIM + 16
LOG2E = 1.4426950408889634


def _eye_bf16(n):
    return jnp.where(_iota((n, n), 0) == _iota((n, n), 1), 1.0, 0.0).astype(BF16)


def _attn_setup(kz_ref, qvt_ref, kb_s, vta_s, bias_s, slopes):
    gw = GROUP_WIDTH
    t_len = kz_ref.shape[1]
    tile = ATTN_TILE
    ones_rows = jnp.ones((VT_ROWS - HEAD_DIM, tile), BF16)

    def body(n, carry):
        rows = pl.ds(pl.multiple_of(n * tile, tile), tile)
        kb_s[rows, :] = kz_ref[0, rows, 0:gw].astype(BF16)
        for h in range(HEADS):
            vta_s[n, h, 0:HEAD_DIM, :] = qvt_ref[n, gw + h * HEAD_DIM:gw + (h + 1) * HEAD_DIM, :].astype(BF16)
            vta_s[n, h, HEAD_DIM:VT_ROWS, :] = ones_rows
        return carry

    lax.fori_loop(0, t_len // tile, body, 0)
    key_local = _iota((tile, tile), 0).astype(F32)
    for h in range(HEADS):
        bias_s[h] = key_local * (slopes[h] * LOG2E)


def _flash_t(qt_s, heads, slopes, kb_s, vta_s, bias_s, ts_s, acc_s, sel_row):
    tile = ATTN_TILE
    half = tile // 2
    n = len(heads)
    causal = _iota((tile, tile), 0) <= _iota((tile, tile), 1)

    def scores(b, slot):
        k_b = kb_s[pl.ds(pl.multiple_of(b * tile, tile), tile), :]
        for c in range(n):
            ts_s[slot][c] = _dot(k_b, qt_s[c]) + bias_s[heads[c]]

    def run(i):
        return _flash_run(i, n, heads, slopes, vta_s, ts_s, acc_s, sel_row, scores, causal)

    return (lambda: scores(0, 0)), run


def _flash_run(i, n, heads, slopes, vta_s, ts_s, acc_s, sel_row, scores, causal):
    tile = ATTN_TILE
    half = tile // 2

    def absorb(b, slot, ms, own):
        dist = jnp.full((1, half), (i - b) * tile, jnp.int32).astype(F32)
        new_ms = []
        for c in range(n):
            off = dist * (-slopes[heads[c]] * LOG2E)
            alphas, p_halves = [], []
            for hf in range(2):
                lanes = slice(hf * half, (hf + 1) * half)
                t = ts_s[slot][c, :, lanes]
                if own:
                    t = jnp.where(causal[:, lanes], t, NEG_INF)
                col_max = jnp.max(t, axis=0, keepdims=True)
                sel = None if own else sel_row(c, b, hf)
                if sel is not None:
                    col_max = jnp.where(sel > 0.5, col_max, NEG_INF)
                m_old = ms[2 * c + hf]
                m_new = jnp.maximum(m_old, col_max + off)
                sub = m_new - off
                if sel is not None:
                    sub = jnp.where(sel > 0.5, sub, float("inf"))
                new_ms.append(m_new)
                alphas.append(jnp.exp2(m_old - m_new))
                p_halves.append(jnp.exp2(t - sub).astype(BF16))
            pv = _dot(vta_s[b, heads[c]], jnp.concatenate(p_halves, axis=1))
            for hf in range(2):
                lanes = slice(hf * half, (hf + 1) * half)
                acc_s[c, :, lanes] = alphas[hf] * acc_s[c, :, lanes] + pv[:, lanes]
        return tuple(new_ms)

    def pair(p, ms):
        b = 2 * p
        scores(b + 1, 1)
        ms = absorb(b, 0, ms, own=False)
        scores(b + 2, 0)
        return absorb(b + 1, 1, ms, own=False)

    def odd_tail(r, ms):
        scores(i, 1)
        ms = absorb(i - 1, 0, ms, own=False)
        return absorb(i, 1, ms, own=True)

    def even_tail(r, ms):
        return absorb(i, 0, ms, own=True)

    acc_s[...] = jnp.zeros_like(acc_s)
    ms = tuple(jnp.full((1, half), -1e30, F32) for _ in range(2 * n))
    ms = lax.fori_loop(0, i >> 1, pair, ms)
    ms = lax.fori_loop(0, i & 1, odd_tail, ms)
    lax.fori_loop(0, 1 - (i & 1), even_tail, ms)
    return [acc_s[c] for c in range(n)]


def _moba_body(kz_ref, qvt_ref, o_ref, kb_s, vta_s, bias_s, ts_a, ts_b, qt_s, acc_s, km_s, sel_s):
    ts_s = (ts_a, ts_b)
    gw = GROUP_WIDTH
    t_len = kz_ref.shape[1]
    tile = ATTN_TILE
    nb = t_len // MOBA_BLOCK
    topk = min(MOBA_TOPK, nb)
    slopes = [2.0 ** -(2 * h + 2) for h in range(HEADS)]

    _attn_setup(kz_ref, qvt_ref, kb_s, vta_s, bias_s, slopes)

    lane_head = _iota((1, gw), 1) >> HEAD_SHIFT

    def means(n, carry):
        rows = pl.ds(pl.multiple_of(n * MOBA_BLOCK, MOBA_BLOCK), MOBA_BLOCK)
        mean = jnp.sum(kz_ref[0, rows, 0:gw], axis=0, keepdims=True) * (1.0 / MOBA_BLOCK)
        for h in range(HEADS):
            km_s[pl.ds(h * nb + n, 1), :] = jnp.where(lane_head == h, mean, 0.0)
        return carry

    lax.fori_loop(0, nb, means, 0)
    km2 = _split(km_s[...], 2)
    eye = _eye_bf16(gw)
    row_head = _iota((gw, 1), 0) >> HEAD_SHIFT
    n_idx = _iota((nb, tile), 0)

    first_scores, run = _flash_t(qt_s, list(range(HEADS)), slopes, kb_s, vta_s, bias_s, ts_s, acc_s,
                                 lambda c, j, hf: sel_s[hf, pl.ds(c * nb + j, 1), :])

    def prologue(i):
        q_t = qvt_ref[i, 0:gw, :]
        q2 = _split(q_t, 2)
        gate = _dot(km2[0], q2[0]) + (_dot(km2[0], q2[1]) + _dot(km2[1], q2[0]))
        past_blk = n_idx < i
        for h in range(HEADS):
            gm = jnp.where(past_blk, gate[h * nb:(h + 1) * nb], NEG_INF)
            cnt = jnp.zeros((nb, tile), F32)
            for n2 in range(nb):
                other = gm[n2:n2 + 1, :]
                beats = (other > gm) | ((other == gm) & (n2 < n_idx))
                cnt = cnt + jnp.where(beats, 1.0, 0.0)
            sel = jnp.where(past_blk & (cnt < topk), 1.0, 0.0)
            for hf in range(2):
                sel_s[hf, h * nb:(h + 1) * nb, :] = sel[:, hf * (tile // 2):(hf + 1) * (tile // 2)]

        q_b = (q_t * (HEAD_DIM ** -0.5 * LOG2E)).astype(BF16)
        for h in range(HEADS):
            qt_s[h] = jnp.where(row_head == h, q_b, jnp.zeros((), BF16))
        first_scores()

    n_tiles = t_len // tile

    def qtile(i, carry):
        accs = run(i)
        rows_q = pl.ds(pl.multiple_of(i * tile, tile), tile)
        o_t = jnp.concatenate([a[0:HEAD_DIM] * (1.0 / a[HEAD_DIM:HEAD_DIM + 1]) for a in accs], axis=0)
        out = _dot_nt(eye, o_t.astype(BF16))
        z = kz_ref[0, rows_q, gw:2 * gw]
        o_ref[0, rows_q, :] = (out * _silu(z)).astype(o_ref.dtype)
        prologue(jnp.minimum(i + 1, n_tiles - 1))
        return carry

    prologue(0)
    lax.fori_loop(0, n_tiles, qtile, 0)


def _attn_in_specs(t):
    gw = GROUP_WIDTH
    return [pl.BlockSpec((1, t, 2 * gw), lambda i: (i, 0, 0)),
            pl.BlockSpec((t // ATTN_TILE, 2 * gw, ATTN_TILE), lambda i: (i, 0, 0))]


def _moba(kz, qvt):
    b, t, _ = kz.shape
    gw = GROUP_WIDTH
    assert t % MOBA_BLOCK == 0 and MOBA_BLOCK == ATTN_TILE
    return pl.pallas_call(
        _moba_body,
        out_shape=jax.ShapeDtypeStruct((b, t, gw), BF16),
        grid=(b,),
        in_specs=_attn_in_specs(t),
        out_specs=pl.BlockSpec((1, t, gw), lambda i: (i, 0, 0)),
        scratch_shapes=[pltpu.VMEM((t, gw), BF16),
                        pltpu.VMEM((t // ATTN_TILE, HEADS, VT_ROWS, ATTN_TILE), BF16),
                        pltpu.VMEM((HEADS, ATTN_TILE, ATTN_TILE), F32),
                        pltpu.VMEM((HEADS, ATTN_TILE, ATTN_TILE), F32),
                        pltpu.VMEM((HEADS, ATTN_TILE, ATTN_TILE), F32),
                        pltpu.VMEM((HEADS, gw, ATTN_TILE), BF16),
                        pltpu.VMEM((HEADS, VT_ROWS, ATTN_TILE), F32),
                        pltpu.VMEM((HEADS * (t // MOBA_BLOCK), gw), F32),
                        pltpu.VMEM((2, HEADS * (t // MOBA_BLOCK), ATTN_TILE // 2), F32)],
        compiler_params=pltpu.CompilerParams(dimension_semantics=("arbitrary",),
                                             vmem_limit_bytes=VMEM_LIMIT),
        name="moba",
    )(kz, qvt)


def _diff_body(layer, kz_ref, qvt_ref, lq1_ref, lk1_ref, lq2_ref, lk2_ref, ng_ref, o_ref, kb_s, vta_s, bias_s,
               ts_a, ts_b, qt_s, acc_s):
    ts_s = (ts_a, ts_b)
    gw = GROUP_WIDTH
    t_len = kz_ref.shape[1]
    tile = ATTN_TILE
    lam_init = 0.8 - 0.6 * math.exp(-0.3 * layer)
    lam = (jnp.exp(jnp.sum(lq1_ref[...] * lk1_ref[...], axis=-1, keepdims=True))
           - jnp.exp(jnp.sum(lq2_ref[...] * lk2_ref[...], axis=-1, keepdims=True)) + lam_init)
    slopes = [2.0 ** -(2 * h + 1) for h in range(HEADS)]

    _attn_setup(kz_ref, qvt_ref, kb_s, vta_s, bias_s, slopes)
    row_map = _iota((gw, 1), 0) >> (HEAD_SHIFT - 1)
    eye = _eye_bf16(gw)

    streams = [(h, mp) for h in range(HEADS) for mp in range(2)]
    first_scores, run = _flash_t(qt_s, [h for h, _ in streams], slopes, kb_s, vta_s, bias_s, ts_s, acc_s,
                                 lambda c, j, hf: None)

    def prologue(i):
        q_t = (qvt_ref[i, 0:gw, :] * (DIFF_DIM ** -0.5 * LOG2E)).astype(BF16)
        for c, (h, mp) in enumerate(streams):
            qt_s[c] = jnp.where(row_map == 2 * h + mp, q_t, jnp.zeros((), BF16))
        first_scores()

    n_tiles = t_len // tile

    def qtile(i, carry):
        accs = run(i)
        rows_q = pl.ds(pl.multiple_of(i * tile, tile), tile)
        o_m = [a[0:HEAD_DIM] * (1.0 / a[HEAD_DIM:HEAD_DIM + 1]) for a in accs]
        ys = []
        for h in range(HEADS):
            o_h = o_m[2 * h] - lam * o_m[2 * h + 1]
            ms = jnp.mean(o_h * o_h, axis=0, keepdims=True)
            ys.append(o_h * lax.rsqrt(ms + NORM_EPS))
        out = _dot_nt(eye, jnp.concatenate(ys, axis=0).astype(BF16))
        z = kz_ref[0, rows_q, gw:2 * gw]
        o_ref[0, rows_q, :] = (out * ng_ref[...] * (1.0 - lam_init) * _silu(z)).astype(o_ref.dtype)
        prologue(jnp.minimum(i + 1, n_tiles - 1))
        return carry

    prologue(0)
    lax.fori_loop(0, n_tiles, qtile, 0)


def _diff(layer, kz, qvt, lq1, lk1, lq2, lk2, ng_exp):
    b, t, _ = kz.shape
    gw = GROUP_WIDTH
    assert t % ATTN_TILE == 0
    full = lambda shape: pl.BlockSpec(shape, lambda i: (0,) * len(shape))
    return pl.pallas_call(
        functools.partial(_diff_body, layer),
        out_shape=jax.ShapeDtypeStruct((b, t, gw), BF16),
        grid=(b,),
        in_specs=_attn_in_specs(t) + [full((1, DIFF_DIM))] * 4 + [full((1, gw))],
        out_specs=pl.BlockSpec((1, t, gw), lambda i: (i, 0, 0)),
        scratch_shapes=[pltpu.VMEM((t, gw), BF16),
                        pltpu.VMEM((t // ATTN_TILE, HEADS, VT_ROWS, ATTN_TILE), BF16),
                        pltpu.VMEM((HEADS, ATTN_TILE, ATTN_TILE), F32),
                        pltpu.VMEM((2 * HEADS, ATTN_TILE, ATTN_TILE), F32),
                        pltpu.VMEM((2 * HEADS, ATTN_TILE, ATTN_TILE), F32),
                        pltpu.VMEM((2 * HEADS, gw, ATTN_TILE), BF16),
                        pltpu.VMEM((2 * HEADS, VT_ROWS, ATTN_TILE), F32)],
        compiler_params=pltpu.CompilerParams(dimension_semantics=("arbitrary",),
                                             vmem_limit_bytes=VMEM_LIMIT),
        name="diffattn",
    )(kz, qvt, lq1, lk1, lq2, lk2, ng_exp)


def kernel(x, pre_norm_g, post_norm_g, w_in, conv_w, gdn_a_log, gdn_dt_bias, gdn_norm_g,
           hgrn_lb, hgrn_norm_g, diff_lq1, diff_lk1, diff_lq2, diff_lk2, diff_norm_g, w_out):
    b, t, d = x.shape
    depth = w_in.shape[0]
    gw = GROUP_WIDTH
    x2 = x.astype(F32).reshape(b * t, d)
    row = lambda a: a.astype(F32).reshape(1, -1)
    per_head = lambda a: jnp.repeat(a.astype(F32), HEAD_DIM).reshape(1, gw)
    per_dim = lambda a: jnp.tile(a.astype(F32), HEADS).reshape(1, gw)
    lbp = hgrn_lb.astype(F32)
    for l in range(depth):
        p_gdn, p_ab, p_hgrn, kz_moba, kz_diff, qvt_moba, qvt_diff = _inproj(
            x2, row(pre_norm_g[l]), *_reorder_w_in(w_in[l]))
        shp = lambda a: a.reshape(b, t, a.shape[-1])
        o_gdn = _gdn(shp(p_gdn), shp(p_ab), conv_w[l].astype(F32), per_head(gdn_a_log[l]),
                     per_head(gdn_dt_bias[l]), per_dim(gdn_norm_g[l]))
        o_hgrn = _hgrn(l, shp(p_hgrn), lbp, per_dim(hgrn_norm_g[l]))
        o_moba = _moba(shp(kz_moba), qvt_moba)
        o_diff = _diff(l, shp(kz_diff), qvt_diff, row(diff_lq1[l]), row(diff_lk1[l]), row(diff_lq2[l]),
                       row(diff_lk2[l]), per_dim(diff_norm_g[l]))
        os_ = [o.reshape(b * t, gw) for o in (o_gdn, o_hgrn, o_moba, o_diff)]
        x2 = _outproj(os_, x2, w_out[l].astype(BF16), row(post_norm_g[l]))
    return x2.reshape(b, t, d)
```

```python
import functools
import math

import jax
import jax.numpy as jnp
from jax import lax
from jax.experimental import pallas as pl
from jax.experimental.pallas import tpu as pltpu

F32 = jnp.float32
BF16 = jnp.bfloat16

HEADS = 4
HEAD_DIM = 64
GROUP_WIDTH = HEADS * HEAD_DIM
HEAD_SHIFT = 6
GDN_CONV = 4
GDN_CHUNK = 64
GDN_CHUNK_SHIFT = 6
GDN_GROUP = 4
HGRN_CHUNK = 16
HGRN_CHUNK_SHIFT = 4
HGRN_TILE = 128
HGRN_GROUP = 2
MOBA_BLOCK = 256
MOBA_TOPK = 3
DIFF_DIM = HEAD_DIM // 2
ATTN_TILE = 256
NORM_EPS = 1e-6
NEG_INF = float("-inf")

GDN_COLS = 4 * GROUP_WIDTH + 2 * HEADS
ROW_SEG_WIDTHS = (4 * GROUP_WIDTH, 2 * GROUP_WIDTH, 4 * GROUP_WIDTH, 2 * GROUP_WIDTH, 2 * GROUP_WIDTH)
COL_SEG_WIDTHS = (2 * GROUP_WIDTH, 2 * GROUP_WIDTH)
VMEM_LIMIT = 56 * 1024 * 1024


def _iota(shape, dim):
    return lax.broadcasted_iota(jnp.int32, shape, dim)


def _dot(a, b):
    return jnp.dot(a, b, preferred_element_type=F32)


def _dot_nt(a, b):
    return lax.dot_general(a, b, (((1,), (1,)), ((), ())), preferred_element_type=F32)


def _dot_tn(a, b):
    return lax.dot_general(a, b, (((0,), (0,)), ((), ())), preferred_element_type=F32)


def _split(x, n):
    parts = []
    r = x
    for i in range(n):
        p = r.astype(BF16)
        parts.append(p)
        if i + 1 < n:
            r = r - p.astype(F32)
    return parts


def _sigmoid(x):
    return 0.5 * jnp.tanh(0.5 * x) + 0.5


def _silu(x):
    return x * _sigmoid(x)


def _log1p_exp_neg_abs(x):
    return jnp.log(1.0 + jnp.exp(-jnp.abs(x)))


def _softplus(x):
    return jnp.maximum(x, 0.0) + _log1p_exp_neg_abs(x)


def _head_block_mask(rows, cols):
    return (_iota((rows, cols), 0) >> HEAD_SHIFT) == (_iota((rows, cols), 1) >> HEAD_SHIFT)


def _head_sum(x, ones_bd):
    return _dot(x.astype(BF16), ones_bd)


def _inproj_body(x_ref, g_ref, w_ref, wt_ref, *o_refs):
    n_t = len(ROW_SEG_WIDTHS)
    x = x_ref[...]
    ms = jnp.mean(x * x, axis=-1, keepdims=True)
    h = (x * lax.rsqrt(ms + NORM_EPS) * g_ref[...]).astype(BF16)
    off = 0
    for o in o_refs[:n_t]:
        n = o.shape[-1]
        o[...] = _dot(h, w_ref[:, off:off + n])
        off += n
    off = 0
    for o in o_refs[n_t:]:
        n = o.shape[1]
        o[0] = _dot_nt(wt_ref[off:off + n, :], h)
        off += n


def _inproj(x2, g, w, wt):
    m, d = x2.shape
    tm = ATTN_TILE
    assert w.shape[1] == sum(ROW_SEG_WIDTHS) and wt.shape[0] == sum(COL_SEG_WIDTHS) and m % tm == 0
    return pl.pallas_call(
        _inproj_body,
        out_shape=([jax.ShapeDtypeStruct((m, s), F32) for s in ROW_SEG_WIDTHS]
                   + [jax.ShapeDtypeStruct((m // tm, s, tm), F32) for s in COL_SEG_WIDTHS]),
        grid=(m // tm,),
        in_specs=[pl.BlockSpec((tm, d), lambda i: (i, 0)),
                  pl.BlockSpec((1, d), lambda i: (0, 0)),
                  pl.BlockSpec(w.shape, lambda i: (0, 0)),
                  pl.BlockSpec(wt.shape, lambda i: (0, 0))],
        out_specs=([pl.BlockSpec((tm, s), lambda i: (i, 0)) for s in ROW_SEG_WIDTHS]
                   + [pl.BlockSpec((1, s, tm), lambda i: (i, 0, 0)) for s in COL_SEG_WIDTHS]),
        compiler_params=pltpu.CompilerParams(dimension_semantics=("arbitrary",),
                                             vmem_limit_bytes=VMEM_LIMIT),
        name="inproj",
    )(x2, g, w, wt)


def _reorder_w_in(w):
    gw = GROUP_WIDTH
    q, k, v = w[:, 0:gw], w[:, gw:2 * gw], w[:, 2 * gw:3 * gw]
    a = w[:, 3 * gw:3 * gw + HEADS]
    b = w[:, 3 * gw + HEADS:3 * gw + 2 * HEADS]
    z = w[:, 3 * gw + 2 * HEADS:GDN_COLS]
    a_exp = jnp.repeat(a, HEAD_DIM, axis=1)
    b_exp = jnp.repeat(b, HEAD_DIM, axis=1)
    hgrn = w[:, GDN_COLS:GDN_COLS + 4 * gw]
    moba = w[:, GDN_COLS + 4 * gw:GDN_COLS + 8 * gw]
    diff = w[:, GDN_COLS + 8 * gw:GDN_COLS + 12 * gw]
    kz = lambda s: jnp.concatenate([s[:, gw:2 * gw], s[:, 3 * gw:4 * gw]], axis=1)
    qv = lambda s: jnp.concatenate([s[:, 0:gw], s[:, 2 * gw:3 * gw]], axis=1)
    rows = jnp.concatenate([q, k, v, z, a_exp, b_exp, hgrn, kz(moba), kz(diff)], axis=1).astype(BF16)
    cols = jnp.concatenate([qv(moba), qv(diff)], axis=1).T.astype(BF16)
    return rows, cols


def _outproj_body(o1, o2, o3, o4, x_ref, w_ref, g_ref, out_ref):
    gw = GROUP_WIDTH
    y = _dot(o1[...], w_ref[0:gw, :])
    y = y + _dot(o2[...], w_ref[gw:2 * gw, :])
    y = y + _dot(o3[...], w_ref[2 * gw:3 * gw, :])
    y = y + _dot(o4[...], w_ref[3 * gw:4 * gw, :])
    ms = jnp.mean(y * y, axis=-1, keepdims=True)
    out_ref[...] = x_ref[...] + y * lax.rsqrt(ms + NORM_EPS) * g_ref[...]


def _outproj(os_, x2, w, g, tm=1024):
    m, d = x2.shape
    gw = GROUP_WIDTH
    return pl.pallas_call(
        _outproj_body,
        out_shape=jax.ShapeDtypeStruct((m, d), F32),
        grid=(m // tm,),
        in_specs=[pl.BlockSpec((tm, gw), lambda i: (i, 0))] * 4 + [
            pl.BlockSpec((tm, d), lambda i: (i, 0)),
            pl.BlockSpec((d, d), lambda i: (0, 0)),
            pl.BlockSpec((1, d), lambda i: (0, 0))],
        out_specs=pl.BlockSpec((tm, d), lambda i: (i, 0)),
        compiler_params=pltpu.CompilerParams(dimension_semantics=("arbitrary",),
                                             vmem_limit_bytes=VMEM_LIMIT),
        name="outproj",
    )(*os_, x2, w, g)


def _gdn_body(p_ref, ab_ref, cw_ref, alog_ref, dtb_ref, ng_ref, o_ref,
              qdec_s, kdec_s, w_s, u_s, aqk_s, dl_s, state_s, oraw_s):
    gw = GROUP_WIDTH
    t_len = p_ref.shape[1]
    c_len = GDN_CHUNK
    n_chunks = t_len // c_len

    cw = cw_ref[...]
    a_neg = -jnp.exp(alog_ref[...])
    dtb = dtb_ref[...]
    row = _iota((c_len, gw), 0)
    s_idx = _iota((c_len, gw), 1) & (HEAD_DIM - 1)
    lower = s_idx <= row
    strict = s_idx < row
    eye = s_idx == row
    bd = _head_block_mask(gw, gw)
    ones_bd = jnp.where(bd, 1.0, 0.0).astype(BF16)
    row8 = _iota((8, 3 * gw), 0)

    def block_diag(x):
        return jnp.where(bd, jnp.concatenate([x] * HEADS, axis=0), jnp.zeros((), x.dtype))

    def mm_pairs(left, right):
        return _dot(left.astype(BF16), block_diag(right.astype(BF16)))

    group = GDN_GROUP
    g_rows = group * c_len
    r_g = _iota((g_rows, g_rows), 0)
    c_g = _iota((g_rows, g_rows), 1)
    tril_g = jnp.where((c_g <= r_g) & ((r_g >> GDN_CHUNK_SHIFT) == (c_g >> GDN_CHUNK_SHIFT)),
                       1.0, 0.0).astype(BF16)

    def pre(t, tail):
        r0 = pl.multiple_of(t * g_rows, g_rows)
        x = p_ref[0, pl.ds(r0, g_rows), 0:3 * gw]
        acc = x * cw[GDN_CONV - 1:GDN_CONV, :]
        for j in range(GDN_CONV - 1):
            shift = GDN_CONV - 1 - j
            rolled = pltpu.roll(x, shift, axis=0)
            tail_r = pltpu.roll(tail, shift, axis=0)
            head_rows = jnp.where(row8 < shift, tail_r, rolled[0:8])
            shifted = jnp.concatenate([head_rows, rolled[8:]], axis=0)
            acc = acc + shifted * cw[j:j + 1, :]
        y = _silu(acc)
        q = y[:, 0:gw]
        k = y[:, gw:2 * gw]
        v = y[:, 2 * gw:3 * gw]
        q = q * lax.rsqrt(_head_sum(q * q, ones_bd) + NORM_EPS) * (HEAD_DIM ** -0.5)
        k = k * lax.rsqrt(_head_sum(k * k, ones_bd) + NORM_EPS)

        a = ab_ref[0, pl.ds(r0, g_rows), 0:gw]
        b = ab_ref[0, pl.ds(r0, g_rows), gw:2 * gw]
        g = a_neg * _softplus(a + dtb)
        beta = _sigmoid(b)
        g2 = _split(g, 2)
        gc = _dot(tril_g, g2[0]) + _dot(tril_g, g2[1])
        eg = jnp.exp(gc)
        kb = k * beta
        vb = (v * beta).astype(BF16)
        kbg = (kb * eg).astype(BF16)
        qdec_s[pl.ds(r0, g_rows), :] = (q * eg).astype(BF16)
        lhs = kb.astype(BF16), q.astype(BF16), k.astype(BF16)

        ms, ts, aqks = [], [], []
        for j in range(group):
            sl = slice(j * c_len, (j + 1) * c_len)
            gc_j = gc[sl]
            g_row = jnp.sum(jnp.where(eye, gc_j, 0.0), axis=0, keepdims=True)
            decay = jnp.exp(jnp.where(lower, gc_j - g_row, NEG_INF))
            g_last = gc_j[c_len - 1:c_len, :]
            sc = _dot_nt(jnp.concatenate([lhs[0][sl], lhs[1][sl]], axis=0), block_diag(lhs[2][sl]))
            m = -jnp.where(strict, sc[0:c_len] * decay, 0.0)
            ms.append(m)
            ts.append(jnp.where(eye, 1.0, 0.0) + m)
            rows = pl.ds(r0 + j * c_len, c_len)
            aqk_s[rows, :] = (sc[c_len:2 * c_len] * decay).astype(BF16)
            kdec_s[rows, :] = (k[sl] * jnp.exp(g_last - gc_j)).astype(BF16)
            dl_s[pl.ds(t * group + j, 1), :] = jnp.exp(g_last)

        for _ in range(5):
            ms = [mm_pairs(m, m) for m in ms]
            ts = [t_inv + mm_pairs(m, t_inv) for m, t_inv in zip(ms, ts)]
        for j in range(group):
            sl = slice(j * c_len, (j + 1) * c_len)
            rows = pl.ds(r0 + j * c_len, c_len)
            t_b = ts[j].astype(BF16)
            u_s[rows, :] = _dot(t_b, block_diag(vb[sl]))
            w_s[rows, :] = _dot(t_b, block_diag(kbg[sl])).astype(BF16)
        return x[g_rows - 8:g_rows, :]

    lax.fori_loop(0, n_chunks // group, pre, jnp.zeros((8, 3 * gw), F32))

    state_s[...] = jnp.zeros_like(state_s)

    def seq(c, carry):
        finish(jnp.maximum(c - 1, 0))
        r0 = pl.multiple_of(c * c_len, c_len)
        rows = pl.ds(r0, c_len)
        state = state_s[...]
        ws = _dot(jnp.concatenate([w_s[rows, :], qdec_s[rows, :]], axis=0), state.astype(BF16))
        v_new = (u_s[rows, :] - ws[0:c_len]).astype(BF16)
        oraw_s[rows, :] = ws[c_len:2 * c_len] + _dot(aqk_s[rows, :], block_diag(v_new))
        kv = _dot_tn(kdec_s[rows, :], v_new)
        state_s[...] = state * dl_s[pl.ds(c, 1), :] + jnp.where(bd, kv, 0.0)
        return carry

    def finish(c):
        rows = pl.ds(pl.multiple_of(c * c_len, c_len), c_len)
        o = oraw_s[rows, :]
        z = p_ref[0, rows, 3 * gw:4 * gw]
        ms = _head_sum(o * o, ones_bd) * (1.0 / HEAD_DIM)
        o_ref[0, rows, :] = (o * lax.rsqrt(ms + NORM_EPS) * ng_ref[...] * _silu(z)).astype(o_ref.dtype)

    oraw_s[0:c_len, :] = jnp.zeros((c_len, gw), F32)
    lax.fori_loop(0, n_chunks, seq, 0)
    finish(n_chunks - 1)


def _gdn(p, ab, cw, alog_exp, dtb_exp, ng_exp):
    b, t, _ = p.shape
    gw = GROUP_WIDTH
    full = lambda shape: pl.BlockSpec(shape, lambda i: (0,) * len(shape))
    return pl.pallas_call(
        _gdn_body,
        out_shape=jax.ShapeDtypeStruct((b, t, gw), BF16),
        grid=(b,),
        in_specs=[pl.BlockSpec((1, t, 4 * gw), lambda i: (i, 0, 0)),
                  pl.BlockSpec((1, t, 2 * gw), lambda i: (i, 0, 0)),
                  full((GDN_CONV, 3 * gw)), full((1, gw)), full((1, gw)), full((1, gw))],
        out_specs=pl.BlockSpec((1, t, gw), lambda i: (i, 0, 0)),
        scratch_shapes=[pltpu.VMEM((t, gw), BF16), pltpu.VMEM((t, gw), BF16), pltpu.VMEM((t, gw), BF16),
                        pltpu.VMEM((t, gw), F32), pltpu.VMEM((t, gw), BF16),
                        pltpu.VMEM((t // GDN_CHUNK, gw), F32), pltpu.VMEM((gw, gw), F32),
                        pltpu.VMEM((t, gw), F32)],
        compiler_params=pltpu.CompilerParams(dimension_semantics=("arbitrary",),
                                             vmem_limit_bytes=VMEM_LIMIT),
        name="gdn",
    )(p, ab, cw, alog_exp, dtb_exp, ng_exp)


def _hgrn_body(layer, p_ref, lbp_ref, ng_ref, o_ref, qdec_s, kdec_s, oi_s, dl_s, state_s, oraw_s):
    gw = GROUP_WIDTH
    t_len = p_ref.shape[1]
    c_len = HGRN_CHUNK
    tile = HGRN_TILE
    chunks_per_tile = tile // c_len

    lbp = lbp_ref[...]
    e = jnp.exp(lbp - jnp.max(lbp, axis=0, keepdims=True))
    sm = e / jnp.sum(e, axis=0, keepdims=True)
    if layer == 0:
        lb = jnp.zeros((1, gw), F32)
    else:
        lb = jnp.sum(sm[1:layer + 1], axis=0, keepdims=True)
    log_lb = jnp.log(lb)
    log_1m = jnp.log1p(-lb)

    r = _iota((tile, tile), 0)
    c = _iota((tile, tile), 1)
    same = (r >> HGRN_CHUNK_SHIFT) == (c >> HGRN_CHUNK_SHIFT)
    mid = ((r >> HGRN_CHUNK_SHIFT) << HGRN_CHUNK_SHIFT) + c_len // 2
    causal = same & (c <= r)
    m_cum = jnp.where(causal, 1.0, 0.0)
    m_mid = (jnp.where(same & (c > mid) & (c <= r), 1.0, 0.0)
             - jnp.where(same & (c > r) & (c <= mid), 1.0, 0.0))
    m_rest = jnp.where(same & (c > r), 1.0, 0.0)
    m_tot = jnp.where((_iota((chunks_per_tile, tile), 1) >> HGRN_CHUNK_SHIFT)
                      == _iota((chunks_per_tile, tile), 0), 1.0, 0.0)
    mats = jnp.concatenate([m_cum, m_mid, m_rest, m_tot], axis=0).astype(BF16)
    lane_head = _iota((1, gw), 1) >> HEAD_SHIFT
    bd = _head_block_mask(gw, gw)
    ones_bd = jnp.where(bd, 1.0, 0.0).astype(BF16)

    group = HGRN_GROUP

    def pre(t, carry):
        tiles = []
        for j in range(group):
            rows = pl.ds(pl.multiple_of((t * group + j) * tile, tile), tile)
            q = p_ref[0, rows, 0:gw]
            f = p_ref[0, rows, gw:2 * gw]
            log_sig = jnp.minimum(f, 0.0) - _log1p_exp_neg_abs(f)
            b_ = log_1m + log_sig
            log_f = jnp.maximum(log_lb, b_) + _log1p_exp_neg_abs(log_lb - b_)
            tiles.append(dict(rows=rows, k=(1.0 - lb) * _sigmoid(-f), qd=_silu(q), f3=_split(log_f, 2),
                              vb=p_ref[0, rows, 2 * gw:3 * gw].astype(BF16)))
        for j, tl in enumerate(tiles):
            f3 = tl["f3"]
            g_all = _dot(mats, f3[0]) + _dot(mats, f3[1])
            g_mid = g_all[tile:2 * tile]
            tl["qa"] = tl["qd"] * jnp.exp(g_mid)
            tl["ka"] = (tl["k"] * jnp.exp(-g_mid)).astype(BF16)
            qdec_s[tl["rows"], :] = (tl["qd"] * jnp.exp(g_all[0:tile])).astype(BF16)
            kdec_s[tl["rows"], :] = (tl["k"] * jnp.exp(g_all[2 * tile:3 * tile])).astype(BF16)
            c0 = pl.multiple_of((t * group + j) * chunks_per_tile, chunks_per_tile)
            dl_s[pl.ds(c0, chunks_per_tile), :] = jnp.exp(g_all[3 * tile:3 * tile + chunks_per_tile])
        scores = [[_dot_nt(jnp.where(lane_head == h, tl["qa"], 0.0).astype(BF16), tl["ka"])
                   for h in range(HEADS)] for tl in tiles]
        for tl, sc in zip(tiles, scores):
            oi = jnp.zeros((tile, gw), F32)
            for h in range(HEADS):
                a = jnp.where(causal, sc[h], 0.0).astype(BF16)
                oi = oi + jnp.where(lane_head == h, _dot(a, tl["vb"]), 0.0)
            oi_s[tl["rows"], :] = oi
        return carry

    lax.fori_loop(0, t_len // (tile * group), pre, 0)

    state_s[...] = jnp.zeros_like(state_s)

    def seq(t, carry):
        finish(jnp.maximum(t - 1, 0))
        rows = pl.ds(pl.multiple_of(t * tile, tile), tile)
        kd = kdec_s[rows, :]
        vb = p_ref[0, rows, 2 * gw:3 * gw].astype(BF16)
        qd = qdec_s[rows, :]
        oi = oi_s[rows, :]
        dl = dl_s[pl.ds(pl.multiple_of(t * chunks_per_tile, chunks_per_tile), chunks_per_tile), :]
        chunk = lambda x, ci: x[ci * c_len:(ci + 1) * c_len]
        kvs = [_dot_tn(chunk(vb, ci), chunk(kd, ci)) for ci in range(chunks_per_tile)]
        state_t = state_s[...]
        states = []
        for ci in range(chunks_per_tile):
            states.append(state_t.astype(BF16))
            state_t = state_t * dl[ci:ci + 1, :] + jnp.where(bd, kvs[ci], 0.0)
        state_s[...] = state_t
        outs = [_dot_nt(chunk(qd, ci), states[ci]) + chunk(oi, ci) for ci in range(chunks_per_tile)]
        oraw_s[rows, :] = jnp.concatenate(outs, axis=0)
        return carry

    def finish(t):
        rows = pl.ds(pl.multiple_of(t * tile, tile), tile)
        o = oraw_s[rows, :]
        z = p_ref[0, rows, 3 * gw:4 * gw]
        ms = _head_sum(o * o, ones_bd) * (1.0 / HEAD_DIM)
        o_ref[0, rows, :] = (o * lax.rsqrt(ms + NORM_EPS) * ng_ref[...] * _silu(z)).astype(o_ref.dtype)

    oraw_s[0:tile, :] = jnp.zeros((tile, gw), F32)
    lax.fori_loop(0, t_len // tile, seq, 0)
    finish(t_len // tile - 1)


def _hgrn(layer, p, lbp, ng_exp):
    b, t, _ = p.shape
    gw = GROUP_WIDTH
    depth = lbp.shape[0]
    full = lambda shape: pl.BlockSpec(shape, lambda i: (0,) * len(shape))
    return pl.pallas_call(
        functools.partial(_hgrn_body, layer),
        out_shape=jax.ShapeDtypeStruct((b, t, gw), BF16),
        grid=(b,),
        in_specs=[pl.BlockSpec((1, t, 4 * gw), lambda i: (i, 0, 0)), full((depth, gw)), full((1, gw))],
        out_specs=pl.BlockSpec((1, t, gw), lambda i: (i, 0, 0)),
        scratch_shapes=[pltpu.VMEM((t, gw), BF16), pltpu.VMEM((t, gw), BF16), pltpu.VMEM((t, gw), F32),
                        pltpu.VMEM((t // HGRN_CHUNK, gw), F32), pltpu.VMEM((gw, gw), F32),
                        pltpu.VMEM((t, gw), F32)],
        compiler_params=pltpu.CompilerParams(dimension_semantics=("arbitrary",),
                                             vmem_limit_bytes=VMEM_LIMIT),
        name="hgrn2",
    )(p, lbp, ng_exp)


VT_ROWS = HEAD_DIM + 16
LOG2E = 1.4426950408889634


def _eye_bf16(n):
    return jnp.where(_iota((n, n), 0) == _iota((n, n), 1), 1.0, 0.0).astype(BF16)


def _attn_setup(kz_ref, qvt_ref, kb_s, vta_s, bias_s, slopes):
    gw = GROUP_WIDTH
    t_len = kz_ref.shape[1]
    tile = ATTN_TILE
    ones_rows = jnp.ones((VT_ROWS - HEAD_DIM, tile), BF16)

    def body(n, carry):
        rows = pl.ds(pl.multiple_of(n * tile, tile), tile)
        kb_s[rows, :] = kz_ref[0, rows, 0:gw].astype(BF16)
        for h in range(HEADS):
            vta_s[n, h, 0:HEAD_DIM, :] = qvt_ref[n, gw + h * HEAD_DIM:gw + (h + 1) * HEAD_DIM, :].astype(BF16)
            vta_s[n, h, HEAD_DIM:VT_ROWS, :] = ones_rows
        return carry

    lax.fori_loop(0, t_len // tile, body, 0)
    key_local = _iota((tile, tile), 0).astype(F32)
    for h in range(HEADS):
        bias_s[h] = key_local * (slopes[h] * LOG2E)


def _flash_t(qt_s, heads, slopes, kb_s, vta_s, bias_s, ts_s, acc_s, sel_row):
    tile = ATTN_TILE
    half = tile // 2
    n = len(heads)
    causal = _iota((tile, tile), 0) <= _iota((tile, tile), 1)

    def scores(b, slot):
        k_b = kb_s[pl.ds(pl.multiple_of(b * tile, tile), tile), :]
        for c in range(n):
            ts_s[slot][c] = _dot(k_b, qt_s[c]) + bias_s[heads[c]]

    def run(i):
        return _flash_run(i, n, heads, slopes, vta_s, ts_s, acc_s, sel_row, scores, causal)

    return (lambda: scores(0, 0)), run


def _flash_run(i, n, heads, slopes, vta_s, ts_s, acc_s, sel_row, scores, causal):
    tile = ATTN_TILE
    half = tile // 2

    def absorb(b, slot, ms, own):
        dist = jnp.full((1, half), (i - b) * tile, jnp.int32).astype(F32)
        new_ms = []
        for c in range(n):
            off = dist * (-slopes[heads[c]] * LOG2E)
            alphas, p_halves = [], []
            for hf in range(2):
                lanes = slice(hf * half, (hf + 1) * half)
                t = ts_s[slot][c, :, lanes]
                if own:
                    t = jnp.where(causal[:, lanes], t, NEG_INF)
                col_max = jnp.max(t, axis=0, keepdims=True)
                sel = None if own else sel_row(c, b, hf)
                if sel is not None:
                    col_max = jnp.where(sel > 0.5, col_max, NEG_INF)
                m_old = ms[2 * c + hf]
                m_new = jnp.maximum(m_old, col_max + off)
                sub = m_new - off
                if sel is not None:
                    sub = jnp.where(sel > 0.5, sub, float("inf"))
                new_ms.append(m_new)
                alphas.append(jnp.exp2(m_old - m_new))
                p_halves.append(jnp.exp2(t - sub).astype(BF16))
            pv = _dot(vta_s[b, heads[c]], jnp.concatenate(p_halves, axis=1))
            for hf in range(2):
                lanes = slice(hf * half, (hf + 1) * half)
                acc_s[c, :, lanes] = alphas[hf] * acc_s[c, :, lanes] + pv[:, lanes]
        return tuple(new_ms)

    def pair(p, ms):
        b = 2 * p
        scores(b + 1, 1)
        ms = absorb(b, 0, ms, own=False)
        scores(b + 2, 0)
        return absorb(b + 1, 1, ms, own=False)

    def odd_tail(r, ms):
        scores(i, 1)
        ms = absorb(i - 1, 0, ms, own=False)
        return absorb(i, 1, ms, own=True)

    def even_tail(r, ms):
        return absorb(i, 0, ms, own=True)

    acc_s[...] = jnp.zeros_like(acc_s)
    ms = tuple(jnp.full((1, half), -1e30, F32) for _ in range(2 * n))
    ms = lax.fori_loop(0, i >> 1, pair, ms)
    ms = lax.fori_loop(0, i & 1, odd_tail, ms)
    lax.fori_loop(0, 1 - (i & 1), even_tail, ms)
    return [acc_s[c] for c in range(n)]


def _moba_body(kz_ref, qvt_ref, o_ref, kb_s, vta_s, bias_s, ts_a, ts_b, qt_s, acc_s, km_s, sel_s):
    ts_s = (ts_a, ts_b)
    gw = GROUP_WIDTH
    t_len = kz_ref.shape[1]
    tile = ATTN_TILE
    nb = t_len // MOBA_BLOCK
    topk = min(MOBA_TOPK, nb)
    slopes = [2.0 ** -(2 * h + 2) for h in range(HEADS)]

    _attn_setup(kz_ref, qvt_ref, kb_s, vta_s, bias_s, slopes)

    lane_head = _iota((1, gw), 1) >> HEAD_SHIFT

    def means(n, carry):
        rows = pl.ds(pl.multiple_of(n * MOBA_BLOCK, MOBA_BLOCK), MOBA_BLOCK)
        mean = jnp.sum(kz_ref[0, rows, 0:gw], axis=0, keepdims=True) * (1.0 / MOBA_BLOCK)
        for h in range(HEADS):
            km_s[pl.ds(h * nb + n, 1), :] = jnp.where(lane_head == h, mean, 0.0)
        return carry

    lax.fori_loop(0, nb, means, 0)
    km2 = _split(km_s[...], 2)
    eye = _eye_bf16(gw)
    row_head = _iota((gw, 1), 0) >> HEAD_SHIFT
    n_idx = _iota((nb, tile), 0)

    first_scores, run = _flash_t(qt_s, list(range(HEADS)), slopes, kb_s, vta_s, bias_s, ts_s, acc_s,
                                 lambda c, j, hf: sel_s[hf, pl.ds(c * nb + j, 1), :])

    def prologue(i):
        q_t = qvt_ref[i, 0:gw, :]
        q2 = _split(q_t, 2)
        gate = _dot(km2[0], q2[0]) + (_dot(km2[0], q2[1]) + _dot(km2[1], q2[0]))
        past_blk = n_idx < i
        for h in range(HEADS):
            gm = jnp.where(past_blk, gate[h * nb:(h + 1) * nb], NEG_INF)
            cnt = jnp.zeros((nb, tile), F32)
            for n2 in range(nb):
                other = gm[n2:n2 + 1, :]
                beats = (other > gm) | ((other == gm) & (n2 < n_idx))
                cnt = cnt + jnp.where(beats, 1.0, 0.0)
            sel = jnp.where(past_blk & (cnt < topk), 1.0, 0.0)
            for hf in range(2):
                sel_s[hf, h * nb:(h + 1) * nb, :] = sel[:, hf * (tile // 2):(hf + 1) * (tile // 2)]

        q_b = (q_t * (HEAD_DIM ** -0.5 * LOG2E)).astype(BF16)
        for h in range(HEADS):
            qt_s[h] = jnp.where(row_head == h, q_b, jnp.zeros((), BF16))
        first_scores()

    n_tiles = t_len // tile

    def qtile(i, carry):
        accs = run(i)
        rows_q = pl.ds(pl.multiple_of(i * tile, tile), tile)
        o_t = jnp.concatenate([a[0:HEAD_DIM] * (1.0 / a[HEAD_DIM:HEAD_DIM + 1]) for a in accs], axis=0)
        out = _dot_nt(eye, o_t.astype(BF16))
        z = kz_ref[0, rows_q, gw:2 * gw]
        o_ref[0, rows_q, :] = (out * _silu(z)).astype(o_ref.dtype)
        prologue(jnp.minimum(i + 1, n_tiles - 1))
        return carry

    prologue(0)
    lax.fori_loop(0, n_tiles, qtile, 0)


def _attn_in_specs(t):
    gw = GROUP_WIDTH
    return [pl.BlockSpec((1, t, 2 * gw), lambda i: (i, 0, 0)),
            pl.BlockSpec((t // ATTN_TILE, 2 * gw, ATTN_TILE), lambda i: (i, 0, 0))]


def _moba(kz, qvt):
    b, t, _ = kz.shape
    gw = GROUP_WIDTH
    assert t % MOBA_BLOCK == 0 and MOBA_BLOCK == ATTN_TILE
    return pl.pallas_call(
        _moba_body,
        out_shape=jax.ShapeDtypeStruct((b, t, gw), BF16),
        grid=(b,),
        in_specs=_attn_in_specs(t),
        out_specs=pl.BlockSpec((1, t, gw), lambda i: (i, 0, 0)),
        scratch_shapes=[pltpu.VMEM((t, gw), BF16),
                        pltpu.VMEM((t // ATTN_TILE, HEADS, VT_ROWS, ATTN_TILE), BF16),
                        pltpu.VMEM((HEADS, ATTN_TILE, ATTN_TILE), F32),
                        pltpu.VMEM((HEADS, ATTN_TILE, ATTN_TILE), F32),
                        pltpu.VMEM((HEADS, ATTN_TILE, ATTN_TILE), F32),
                        pltpu.VMEM((HEADS, gw, ATTN_TILE), BF16),
                        pltpu.VMEM((HEADS, VT_ROWS, ATTN_TILE), F32),
                        pltpu.VMEM((HEADS * (t // MOBA_BLOCK), gw), F32),
                        pltpu.VMEM((2, HEADS * (t // MOBA_BLOCK), ATTN_TILE // 2), F32)],
        compiler_params=pltpu.CompilerParams(dimension_semantics=("arbitrary",),
                                             vmem_limit_bytes=VMEM_LIMIT),
        name="moba",
    )(kz, qvt)


def _diff_body(layer, kz_ref, qvt_ref, lq1_ref, lk1_ref, lq2_ref, lk2_ref, ng_ref, o_ref, kb_s, vta_s, bias_s,
               ts_a, ts_b, qt_s, acc_s):
    ts_s = (ts_a, ts_b)
    gw = GROUP_WIDTH
    t_len = kz_ref.shape[1]
    tile = ATTN_TILE
    lam_init = 0.8 - 0.6 * math.exp(-0.3 * layer)
    lam = (jnp.exp(jnp.sum(lq1_ref[...] * lk1_ref[...], axis=-1, keepdims=True))
           - jnp.exp(jnp.sum(lq2_ref[...] * lk2_ref[...], axis=-1, keepdims=True)) + lam_init)
    slopes = [2.0 ** -(2 * h + 1) for h in range(HEADS)]

    _attn_setup(kz_ref, qvt_ref, kb_s, vta_s, bias_s, slopes)
    row_map = _iota((gw, 1), 0) >> (HEAD_SHIFT - 1)
    eye = _eye_bf16(gw)

    streams = [(h, mp) for h in range(HEADS) for mp in range(2)]
    first_scores, run = _flash_t(qt_s, [h for h, _ in streams], slopes, kb_s, vta_s, bias_s, ts_s, acc_s,
                                 lambda c, j, hf: None)

    def prologue(i):
        q_t = (qvt_ref[i, 0:gw, :] * (DIFF_DIM ** -0.5 * LOG2E)).astype(BF16)
        for c, (h, mp) in enumerate(streams):
            qt_s[c] = jnp.where(row_map == 2 * h + mp, q_t, jnp.zeros((), BF16))
        first_scores()

    n_tiles = t_len // tile

    def qtile(i, carry):
        accs = run(i)
        rows_q = pl.ds(pl.multiple_of(i * tile, tile), tile)
        o_m = [a[0:HEAD_DIM] * (1.0 / a[HEAD_DIM:HEAD_DIM + 1]) for a in accs]
        ys = []
        for h in range(HEADS):
            o_h = o_m[2 * h] - lam * o_m[2 * h + 1]
            ms = jnp.mean(o_h * o_h, axis=0, keepdims=True)
            ys.append(o_h * lax.rsqrt(ms + NORM_EPS))
        out = _dot_nt(eye, jnp.concatenate(ys, axis=0).astype(BF16))
        z = kz_ref[0, rows_q, gw:2 * gw]
        o_ref[0, rows_q, :] = (out * ng_ref[...] * (1.0 - lam_init) * _silu(z)).astype(o_ref.dtype)
        prologue(jnp.minimum(i + 1, n_tiles - 1))
        return carry

    prologue(0)
    lax.fori_loop(0, n_tiles, qtile, 0)


def _diff(layer, kz, qvt, lq1, lk1, lq2, lk2, ng_exp):
    b, t, _ = kz.shape
    gw = GROUP_WIDTH
    assert t % ATTN_TILE == 0
    full = lambda shape: pl.BlockSpec(shape, lambda i: (0,) * len(shape))
    return pl.pallas_call(
        functools.partial(_diff_body, layer),
        out_shape=jax.ShapeDtypeStruct((b, t, gw), BF16),
        grid=(b,),
        in_specs=_attn_in_specs(t) + [full((1, DIFF_DIM))] * 4 + [full((1, gw))],
        out_specs=pl.BlockSpec((1, t, gw), lambda i: (i, 0, 0)),
        scratch_shapes=[pltpu.VMEM((t, gw), BF16),
                        pltpu.VMEM((t // ATTN_TILE, HEADS, VT_ROWS, ATTN_TILE), BF16),
                        pltpu.VMEM((HEADS, ATTN_TILE, ATTN_TILE), F32),
                        pltpu.VMEM((2 * HEADS, ATTN_TILE, ATTN_TILE), F32),
                        pltpu.VMEM((2 * HEADS, ATTN_TILE, ATTN_TILE), F32),
                        pltpu.VMEM((2 * HEADS, gw, ATTN_TILE), BF16),
                        pltpu.VMEM((2 * HEADS, VT_ROWS, ATTN_TILE), F32)],
        compiler_params=pltpu.CompilerParams(dimension_semantics=("arbitrary",),
                                             vmem_limit_bytes=VMEM_LIMIT),
        name="diffattn",
    )(kz, qvt, lq1, lk1, lq2, lk2, ng_exp)


def kernel(x, pre_norm_g, post_norm_g, w_in, conv_w, gdn_a_log, gdn_dt_bias, gdn_norm_g,
           hgrn_lb, hgrn_norm_g, diff_lq1, diff_lk1, diff_lq2, diff_lk2, diff_norm_g, w_out):
    b, t, d = x.shape
    depth = w_in.shape[0]
    gw = GROUP_WIDTH
    x2 = x.astype(F32).reshape(b * t, d)
    row = lambda a: a.astype(F32).reshape(1, -1)
    per_head = lambda a: jnp.repeat(a.astype(F32), HEAD_DIM).reshape(1, gw)
    per_dim = lambda a: jnp.tile(a.astype(F32), HEADS).reshape(1, gw)
    lbp = hgrn_lb.astype(F32)
    for l in range(depth):
        p_gdn, p_ab, p_hgrn, kz_moba, kz_diff, qvt_moba, qvt_diff = _inproj(
            x2, row(pre_norm_g[l]), *_reorder_w_in(w_in[l]))
        shp = lambda a: a.reshape(b, t, a.shape[-1])
        o_gdn = _gdn(shp(p_gdn), shp(p_ab), conv_w[l].astype(F32), per_head(gdn_a_log[l]),
                     per_head(gdn_dt_bias[l]), per_dim(gdn_norm_g[l]))
        o_hgrn = _hgrn(l, shp(p_hgrn), lbp, per_dim(hgrn_norm_g[l]))
        o_moba = _moba(shp(kz_moba), qvt_moba)
        o_diff = _diff(l, shp(kz_diff), qvt_diff, row(diff_lq1[l]), row(diff_lk1[l]), row(diff_lq2[l]),
                       row(diff_lk2[l]), per_dim(diff_norm_g[l]))
        os_ = [o.reshape(b * t, gw) for o in (o_gdn, o_hgrn, o_moba, o_diff)]
        x2 = _outproj(os_, x2, w_out[l].astype(BF16), row(post_norm_g[l]))
    return x2.reshape(b, t, d)
```

```python
import functools
import math

import jax
import jax.numpy as jnp
from jax import lax
from jax.experimental import pallas as pl
from jax.experimental.pallas import tpu as pltpu

F32 = jnp.float32
BF16 = jnp.bfloat16

HEADS = 4
HEAD_DIM = 64
GROUP_WIDTH = HEADS * HEAD_DIM
HEAD_SHIFT = 6
GDN_CONV = 4
GDN_CHUNK = 64
GDN_CHUNK_SHIFT = 6
GDN_GROUP = 4
HGRN_CHUNK = 16
HGRN_CHUNK_SHIFT = 4
HGRN_TILE = 128
HGRN_GROUP = 2
MOBA_BLOCK = 256
MOBA_TOPK = 3
DIFF_DIM = HEAD_DIM // 2
ATTN_TILE = 256
NORM_EPS = 1e-6
NEG_INF = float("-inf")

GDN_COLS = 4 * GROUP_WIDTH + 2 * HEADS
ROW_SEG_WIDTHS = (4 * GROUP_WIDTH, 2 * GROUP_WIDTH, 4 * GROUP_WIDTH, 2 * GROUP_WIDTH, 2 * GROUP_WIDTH)
COL_SEG_WIDTHS = (2 * GROUP_WIDTH, 2 * GROUP_WIDTH)
VMEM_LIMIT = 56 * 1024 * 1024


def _iota(shape, dim):
    return lax.broadcasted_iota(jnp.int32, shape, dim)


def _dot(a, b):
    return jnp.dot(a, b, preferred_element_type=F32)


def _dot_nt(a, b):
    return lax.dot_general(a, b, (((1,), (1,)), ((), ())), preferred_element_type=F32)


def _dot_tn(a, b):
    return lax.dot_general(a, b, (((0,), (0,)), ((), ())), preferred_element_type=F32)


def _split(x, n):
    parts = []
    r = x
    for i in range(n):
        p = r.astype(BF16)
        parts.append(p)
        if i + 1 < n:
            r = r - p.astype(F32)
    return parts


def _sigmoid(x):
    return 0.5 * jnp.tanh(0.5 * x) + 0.5


def _silu(x):
    return x * _sigmoid(x)


def _log1p_exp_neg_abs(x):
    return jnp.log(1.0 + jnp.exp(-jnp.abs(x)))


def _softplus(x):
    return jnp.maximum(x, 0.0) + _log1p_exp_neg_abs(x)


def _head_block_mask(rows, cols):
    return (_iota((rows, cols), 0) >> HEAD_SHIFT) == (_iota((rows, cols), 1) >> HEAD_SHIFT)


def _head_sum(x, ones_bd):
    return _dot(x.astype(BF16), ones_bd)


def _inproj_body(x_ref, g_ref, w_ref, wt_ref, *o_refs):
    n_t = len(ROW_SEG_WIDTHS)
    x = x_ref[...]
    ms = jnp.mean(x * x, axis=-1, keepdims=True)
    h = (x * lax.rsqrt(ms + NORM_EPS) * g_ref[...]).astype(BF16)
    off = 0
    for o in o_refs[:n_t]:
        n = o.shape[-1]
        o[...] = _dot(h, w_ref[:, off:off + n])
        off += n
    off = 0
    for o in o_refs[n_t:]:
        n = o.shape[1]
        o[0] = _dot_nt(wt_ref[off:off + n, :], h)
        off += n


def _inproj(x2, g, w, wt):
    m, d = x2.shape
    tm = ATTN_TILE
    assert w.shape[1] == sum(ROW_SEG_WIDTHS) and wt.shape[0] == sum(COL_SEG_WIDTHS) and m % tm == 0
    return pl.pallas_call(
        _inproj_body,
        out_shape=([jax.ShapeDtypeStruct((m, s), F32) for s in ROW_SEG_WIDTHS]
                   + [jax.ShapeDtypeStruct((m // tm, s, tm), F32) for s in COL_SEG_WIDTHS]),
        grid=(m // tm,),
        in_specs=[pl.BlockSpec((tm, d), lambda i: (i, 0)),
                  pl.BlockSpec((1, d), lambda i: (0, 0)),
                  pl.BlockSpec(w.shape, lambda i: (0, 0)),
                  pl.BlockSpec(wt.shape, lambda i: (0, 0))],
        out_specs=([pl.BlockSpec((tm, s), lambda i: (i, 0)) for s in ROW_SEG_WIDTHS]
                   + [pl.BlockSpec((1, s, tm), lambda i: (i, 0, 0)) for s in COL_SEG_WIDTHS]),
        compiler_params=pltpu.CompilerParams(dimension_semantics=("arbitrary",),
                                             vmem_limit_bytes=VMEM_LIMIT),
        name="inproj",
    )(x2, g, w, wt)


def _reorder_w_in(w):
    gw = GROUP_WIDTH
    q, k, v = w[:, 0:gw], w[:, gw:2 * gw], w[:, 2 * gw:3 * gw]
    a = w[:, 3 * gw:3 * gw + HEADS]
    b = w[:, 3 * gw + HEADS:3 * gw + 2 * HEADS]
    z = w[:, 3 * gw + 2 * HEADS:GDN_COLS]
    a_exp = jnp.repeat(a, HEAD_DIM, axis=1)
    b_exp = jnp.repeat(b, HEAD_DIM, axis=1)
    hgrn = w[:, GDN_COLS:GDN_COLS + 4 * gw]
    moba = w[:, GDN_COLS + 4 * gw:GDN_COLS + 8 * gw]
    diff = w[:, GDN_COLS + 8 * gw:GDN_COLS + 12 * gw]
    kz = lambda s: jnp.concatenate([s[:, gw:2 * gw], s[:, 3 * gw:4 * gw]], axis=1)
    qv = lambda s: jnp.concatenate([s[:, 0:gw], s[:, 2 * gw:3 * gw]], axis=1)
    rows = jnp.concatenate([q, k, v, z, a_exp, b_exp, hgrn, kz(moba), kz(diff)], axis=1).astype(BF16)
    cols = jnp.concatenate([qv(moba), qv(diff)], axis=1).T.astype(BF16)
    return rows, cols


def _outproj_body(o1, o2, o3, o4, x_ref, w_ref, g_ref, out_ref):
    gw = GROUP_WIDTH
    y = _dot(o1[...], w_ref[0:gw, :])
    y = y + _dot(o2[...], w_ref[gw:2 * gw, :])
    y = y + _dot(o3[...], w_ref[2 * gw:3 * gw, :])
    y = y + _dot(o4[...], w_ref[3 * gw:4 * gw, :])
    ms = jnp.mean(y * y, axis=-1, keepdims=True)
    out_ref[...] = x_ref[...] + y * lax.rsqrt(ms + NORM_EPS) * g_ref[...]


def _outproj(os_, x2, w, g, tm=2048):
    m, d = x2.shape
    gw = GROUP_WIDTH
    return pl.pallas_call(
        _outproj_body,
        out_shape=jax.ShapeDtypeStruct((m, d), F32),
        grid=(m // tm,),
        in_specs=[pl.BlockSpec((tm, gw), lambda i: (i, 0))] * 4 + [
            pl.BlockSpec((tm, d), lambda i: (i, 0)),
            pl.BlockSpec((d, d), lambda i: (0, 0)),
            pl.BlockSpec((1, d), lambda i: (0, 0))],
        out_specs=pl.BlockSpec((tm, d), lambda i: (i, 0)),
        compiler_params=pltpu.CompilerParams(dimension_semantics=("arbitrary",),
                                             vmem_limit_bytes=VMEM_LIMIT),
        name="outproj",
    )(*os_, x2, w, g)


def _gdn_body(p_ref, ab_ref, cw_ref, alog_ref, dtb_ref, ng_ref, o_ref,
              qdec_s, kdec_s, w_s, u_s, aqk_s, dl_s, state_s, oraw_s):
    gw = GROUP_WIDTH
    t_len = p_ref.shape[1]
    c_len = GDN_CHUNK
    n_chunks = t_len // c_len

    cw = cw_ref[...]
    a_neg = -jnp.exp(alog_ref[...])
    dtb = dtb_ref[...]
    row = _iota((c_len, gw), 0)
    s_idx = _iota((c_len, gw), 1) & (HEAD_DIM - 1)
    lower = s_idx <= row
    strict = s_idx < row
    eye = s_idx == row
    bd = _head_block_mask(gw, gw)
    ones_bd = jnp.where(bd, 1.0, 0.0).astype(BF16)
    row8 = _iota((8, 3 * gw), 0)

    def block_diag(x):
        return jnp.where(bd, jnp.concatenate([x] * HEADS, axis=0), jnp.zeros((), x.dtype))

    def mm_pairs(left, right):
        return _dot(left.astype(BF16), block_diag(right.astype(BF16)))

    group = GDN_GROUP
    g_rows = group * c_len
    r_g = _iota((g_rows, g_rows), 0)
    c_g = _iota((g_rows, g_rows), 1)
    tril_g = jnp.where((c_g <= r_g) & ((r_g >> GDN_CHUNK_SHIFT) == (c_g >> GDN_CHUNK_SHIFT)),
                       1.0, 0.0).astype(BF16)

    def pre(t, tail):
        r0 = pl.multiple_of(t * g_rows, g_rows)
        x = p_ref[0, pl.ds(r0, g_rows), 0:3 * gw]
        acc = x * cw[GDN_CONV - 1:GDN_CONV, :]
        for j in range(GDN_CONV - 1):
            shift = GDN_CONV - 1 - j
            rolled = pltpu.roll(x, shift, axis=0)
            tail_r = pltpu.roll(tail, shift, axis=0)
            head_rows = jnp.where(row8 < shift, tail_r, rolled[0:8])
            shifted = jnp.concatenate([head_rows, rolled[8:]], axis=0)
            acc = acc + shifted * cw[j:j + 1, :]
        y = _silu(acc)
        q = y[:, 0:gw]
        k = y[:, gw:2 * gw]
        v = y[:, 2 * gw:3 * gw]
        q = q * lax.rsqrt(_head_sum(q * q, ones_bd) + NORM_EPS) * (HEAD_DIM ** -0.5)
        k = k * lax.rsqrt(_head_sum(k * k, ones_bd) + NORM_EPS)

        a = ab_ref[0, pl.ds(r0, g_rows), 0:gw]
        b = ab_ref[0, pl.ds(r0, g_rows), gw:2 * gw]
        g = a_neg * _softplus(a + dtb)
        beta = _sigmoid(b)
        g2 = _split(g, 2)
        gc = _dot(tril_g, g2[0]) + _dot(tril_g, g2[1])
        eg = jnp.exp(gc)
        kb = k * beta
        vb = (v * beta).astype(BF16)
        kbg = (kb * eg).astype(BF16)
        qdec_s[pl.ds(r0, g_rows), :] = (q * eg).astype(BF16)
        lhs = kb.astype(BF16), q.astype(BF16), k.astype(BF16)

        ms, ts, aqks = [], [], []
        for j in range(group):
            sl = slice(j * c_len, (j + 1) * c_len)
            gc_j = gc[sl]
            g_row = jnp.sum(jnp.where(eye, gc_j, 0.0), axis=0, keepdims=True)
            decay = jnp.exp(jnp.where(lower, gc_j - g_row, NEG_INF))
            g_last = gc_j[c_len - 1:c_len, :]
            sc = _dot_nt(jnp.concatenate([lhs[0][sl], lhs[1][sl]], axis=0), block_diag(lhs[2][sl]))
            m = -jnp.where(strict, sc[0:c_len] * decay, 0.0)
            ms.append(m)
            ts.append(jnp.where(eye, 1.0, 0.0) + m)
            rows = pl.ds(r0 + j * c_len, c_len)
            aqk_s[rows, :] = (sc[c_len:2 * c_len] * decay).astype(BF16)
            kdec_s[rows, :] = (k[sl] * jnp.exp(g_last - gc_j)).astype(BF16)
            dl_s[pl.ds(t * group + j, 1), :] = jnp.exp(g_last)

        for _ in range(5):
            ms = [mm_pairs(m, m) for m in ms]
            ts = [t_inv + mm_pairs(m, t_inv) for m, t_inv in zip(ms, ts)]
        for j in range(group):
            sl = slice(j * c_len, (j + 1) * c_len)
            rows = pl.ds(r0 + j * c_len, c_len)
            t_b = ts[j].astype(BF16)
            u_s[rows, :] = _dot(t_b, block_diag(vb[sl]))
            w_s[rows, :] = _dot(t_b, block_diag(kbg[sl])).astype(BF16)
        return x[g_rows - 8:g_rows, :]

    lax.fori_loop(0, n_chunks // group, pre, jnp.zeros((8, 3 * gw), F32))

    state_s[...] = jnp.zeros_like(state_s)

    def seq(c, carry):
        finish(jnp.maximum(c - 1, 0))
        r0 = pl.multiple_of(c * c_len, c_len)
        rows = pl.ds(r0, c_len)
        state = state_s[...]
        ws = _dot(jnp.concatenate([w_s[rows, :], qdec_s[rows, :]], axis=0), state.astype(BF16))
        v_new = (u_s[rows, :] - ws[0:c_len]).astype(BF16)
        oraw_s[rows, :] = ws[c_len:2 * c_len] + _dot(aqk_s[rows, :], block_diag(v_new))
        kv = _dot_tn(kdec_s[rows, :], v_new)
        state_s[...] = state * dl_s[pl.ds(c, 1), :] + jnp.where(bd, kv, 0.0)
        return carry

    def finish(c):
        rows = pl.ds(pl.multiple_of(c * c_len, c_len), c_len)
        o = oraw_s[rows, :]
        z = p_ref[0, rows, 3 * gw:4 * gw]
        ms = _head_sum(o * o, ones_bd) * (1.0 / HEAD_DIM)
        o_ref[0, rows, :] = (o * lax.rsqrt(ms + NORM_EPS) * ng_ref[...] * _silu(z)).astype(o_ref.dtype)

    oraw_s[0:c_len, :] = jnp.zeros((c_len, gw), F32)
    lax.fori_loop(0, n_chunks, seq, 0)
    finish(n_chunks - 1)


def _gdn(p, ab, cw, alog_exp, dtb_exp, ng_exp):
    b, t, _ = p.shape
    gw = GROUP_WIDTH
    full = lambda shape: pl.BlockSpec(shape, lambda i: (0,) * len(shape))
    return pl.pallas_call(
        _gdn_body,
        out_shape=jax.ShapeDtypeStruct((b, t, gw), BF16),
        grid=(b,),
        in_specs=[pl.BlockSpec((1, t, 4 * gw), lambda i: (i, 0, 0)),
                  pl.BlockSpec((1, t, 2 * gw), lambda i: (i, 0, 0)),
                  full((GDN_CONV, 3 * gw)), full((1, gw)), full((1, gw)), full((1, gw))],
        out_specs=pl.BlockSpec((1, t, gw), lambda i: (i, 0, 0)),
        scratch_shapes=[pltpu.VMEM((t, gw), BF16), pltpu.VMEM((t, gw), BF16), pltpu.VMEM((t, gw), BF16),
                        pltpu.VMEM((t, gw), F32), pltpu.VMEM((t, gw), BF16),
                        pltpu.VMEM((t // GDN_CHUNK, gw), F32), pltpu.VMEM((gw, gw), F32),
                        pltpu.VMEM((t, gw), F32)],
        compiler_params=pltpu.CompilerParams(dimension_semantics=("arbitrary",),
                                             vmem_limit_bytes=VMEM_LIMIT),
        name="gdn",
    )(p, ab, cw, alog_exp, dtb_exp, ng_exp)


def _hgrn_body(layer, p_ref, lbp_ref, ng_ref, o_ref, qdec_s, kdec_s, oi_s, gt_s, state_s, oraw_s):
    gw = GROUP_WIDTH
    t_len = p_ref.shape[1]
    c_len = HGRN_CHUNK
    tile = HGRN_TILE
    chunks_per_tile = tile // c_len

    lbp = lbp_ref[...]
    e = jnp.exp(lbp - jnp.max(lbp, axis=0, keepdims=True))
    sm = e / jnp.sum(e, axis=0, keepdims=True)
    if layer == 0:
        lb = jnp.zeros((1, gw), F32)
    else:
        lb = jnp.sum(sm[1:layer + 1], axis=0, keepdims=True)
    log_lb = jnp.log(lb)
    log_1m = jnp.log1p(-lb)

    r = _iota((tile, tile), 0)
    c = _iota((tile, tile), 1)
    same = (r >> HGRN_CHUNK_SHIFT) == (c >> HGRN_CHUNK_SHIFT)
    mid = ((r >> HGRN_CHUNK_SHIFT) << HGRN_CHUNK_SHIFT) + c_len // 2
    causal = same & (c <= r)
    m_cum = jnp.where(causal, 1.0, 0.0)
    m_mid = (jnp.where(same & (c > mid) & (c <= r), 1.0, 0.0)
             - jnp.where(same & (c > r) & (c <= mid), 1.0, 0.0))
    m_rest = jnp.where(same & (c > r), 1.0, 0.0)
    m_tot = jnp.where((_iota((chunks_per_tile, tile), 1) >> HGRN_CHUNK_SHIFT)
                      == _iota((chunks_per_tile, tile), 0), 1.0, 0.0)
    mats = jnp.concatenate([m_cum, m_mid, m_rest, m_tot], axis=0).astype(BF16)
    lane_head = _iota((1, gw), 1) >> HEAD_SHIFT
    bd = _head_block_mask(gw, gw)
    ones_bd = jnp.where(bd, 1.0, 0.0).astype(BF16)

    group = HGRN_GROUP

    def pre(t, carry):
        tiles = []
        for j in range(group):
            rows = pl.ds(pl.multiple_of((t * group + j) * tile, tile), tile)
            q = p_ref[0, rows, 0:gw]
            f = p_ref[0, rows, gw:2 * gw]
            log_sig = jnp.minimum(f, 0.0) - _log1p_exp_neg_abs(f)
            b_ = log_1m + log_sig
            log_f = jnp.maximum(log_lb, b_) + _log1p_exp_neg_abs(log_lb - b_)
            tiles.append(dict(rows=rows, k=(1.0 - lb) * _sigmoid(-f), qd=_silu(q), f3=_split(log_f, 2),
                              vb=p_ref[0, rows, 2 * gw:3 * gw].astype(BF16)))
        for j, tl in enumerate(tiles):
            f3 = tl["f3"]
            g_all = _dot(mats, f3[0]) + _dot(mats, f3[1])
            g_mid = g_all[tile:2 * tile]
            tl["qa"] = tl["qd"] * jnp.exp(g_mid)
            tl["ka"] = (tl["k"] * jnp.exp(-g_mid)).astype(BF16)
            qdec_s[tl["rows"], :] = tl["qd"] * jnp.exp(g_all[0:tile])
            kdec_s[tl["rows"], :] = tl["k"] * jnp.exp(g_all[2 * tile:3 * tile])
            c0 = pl.multiple_of((t * group + j) * chunks_per_tile, chunks_per_tile)
            gt_s[pl.ds(c0, chunks_per_tile), :] = g_all[3 * tile:3 * tile + chunks_per_tile]
        scores = [[_dot_nt(jnp.where(lane_head == h, tl["qa"], 0.0).astype(BF16), tl["ka"])
                   for h in range(HEADS)] for tl in tiles]
        for tl, sc in zip(tiles, scores):
            oi = jnp.zeros((tile, gw), F32)
            for h in range(HEADS):
                a = jnp.where(causal, sc[h], 0.0).astype(BF16)
                oi = oi + jnp.where(lane_head == h, _dot(a, tl["vb"]), 0.0)
            oi_s[tl["rows"], :] = oi
        return carry

    lax.fori_loop(0, t_len // (tile * group), pre, 0)

    state_s[...] = jnp.zeros_like(state_s)

    def seq(t, carry):
        finish(jnp.maximum(t - 1, 0))
        rows = pl.ds(pl.multiple_of(t * tile, tile), tile)
        kd = kdec_s[rows, :]
        qd = qdec_s[rows, :]
        vb = p_ref[0, rows, 2 * gw:3 * gw].astype(BF16)
        gt = gt_s[pl.ds(pl.multiple_of(t * chunks_per_tile, chunks_per_tile), chunks_per_tile), :]
        chunk = lambda x, ci: x[ci * c_len:(ci + 1) * c_len]
        log_b = [jnp.zeros((1, gw), F32)]
        for ci in range(chunks_per_tile):
            log_b.append(log_b[ci] + gt[ci:ci + 1, :])

        def keys_decayed_to(c):
            return jnp.concatenate([(chunk(kd, cp) * jnp.exp(log_b[c] - log_b[cp + 1])).astype(BF16)
                                    for cp in range(c)], axis=0)

        state_t = state_s[...]
        decay_rows = jnp.concatenate([jnp.broadcast_to(jnp.exp(log_b[ci]), (c_len, gw))
                                      for ci in range(chunks_per_tile)], axis=0)
        o = _dot_nt((qd * decay_rows).astype(BF16), state_t.astype(BF16)) + oi_s[rows, :]
        qb = qd.astype(BF16)
        later = range(1, chunks_per_tile)
        keys = [keys_decayed_to(ci) for ci in later]
        kv_t = _dot_tn(vb, keys_decayed_to(chunks_per_tile))
        q_heads = [jnp.concatenate([jnp.where(lane_head == h, chunk(qb, ci), jnp.zeros((), BF16))
                                    for h in range(HEADS)], axis=0) for ci in later]
        scs = [_dot_nt(qh, ks) for qh, ks in zip(q_heads, keys)]
        pvs = [_dot(sc.astype(BF16), vb[0:ci * c_len]) for sc, ci in zip(scs, later)]
        outs = [chunk(o, 0)]
        for pv, ci in zip(pvs, later):
            cross = jnp.zeros((c_len, gw), F32)
            for h in range(HEADS):
                cross = cross + jnp.where(lane_head == h, pv[h * c_len:(h + 1) * c_len], 0.0)
            outs.append(chunk(o, ci) + cross)
        oraw_s[rows, :] = jnp.concatenate(outs, axis=0)
        state_s[...] = state_t * jnp.exp(log_b[chunks_per_tile]) + jnp.where(bd, kv_t, 0.0)
        return carry

    def finish(t):
        rows = pl.ds(pl.multiple_of(t * tile, tile), tile)
        o = oraw_s[rows, :]
        z = p_ref[0, rows, 3 * gw:4 * gw]
        ms = _head_sum(o * o, ones_bd) * (1.0 / HEAD_DIM)
        o_ref[0, rows, :] = (o * lax.rsqrt(ms + NORM_EPS) * ng_ref[...] * _silu(z)).astype(o_ref.dtype)

    oraw_s[0:tile, :] = jnp.zeros((tile, gw), F32)
    lax.fori_loop(0, t_len // tile, seq, 0)
    finish(t_len // tile - 1)


def _hgrn(layer, p, lbp, ng_exp):
    b, t, _ = p.shape
    gw = GROUP_WIDTH
    depth = lbp.shape[0]
    full = lambda shape: pl.BlockSpec(shape, lambda i: (0,) * len(shape))
    return pl.pallas_call(
        functools.partial(_hgrn_body, layer),
        out_shape=jax.ShapeDtypeStruct((b, t, gw), BF16),
        grid=(b,),
        in_specs=[pl.BlockSpec((1, t, 4 * gw), lambda i: (i, 0, 0)), full((depth, gw)), full((1, gw))],
        out_specs=pl.BlockSpec((1, t, gw), lambda i: (i, 0, 0)),
        scratch_shapes=[pltpu.VMEM((t, gw), F32), pltpu.VMEM((t, gw), F32), pltpu.VMEM((t, gw), F32),
                        pltpu.VMEM((t // HGRN_CHUNK, gw), F32), pltpu.VMEM((gw, gw), F32),
                        pltpu.VMEM((t, gw), F32)],
        compiler_params=pltpu.CompilerParams(dimension_semantics=("arbitrary",),
                                             vmem_limit_bytes=VMEM_LIMIT),
        name="hgrn2",
    )(p, lbp, ng_exp)


VT_ROWS = HEAD_DIM + 16
LOG2E = 1.4426950408889634


def _eye_bf16(n):
    return jnp.where(_iota((n, n), 0) == _iota((n, n), 1), 1.0, 0.0).astype(BF16)


def _attn_setup(kz_ref, qvt_ref, kb_s, vta_s, bias_s, slopes):
    gw = GROUP_WIDTH
    t_len = kz_ref.shape[1]
    tile = ATTN_TILE
    ones_rows = jnp.ones((VT_ROWS - HEAD_DIM, tile), BF16)

    def body(n, carry):
        rows = pl.ds(pl.multiple_of(n * tile, tile), tile)
        kb_s[rows, :] = kz_ref[0, rows, 0:gw].astype(BF16)
        for h in range(HEADS):
            vta_s[n, h, 0:HEAD_DIM, :] = qvt_ref[n, gw + h * HEAD_DIM:gw + (h + 1) * HEAD_DIM, :].astype(BF16)
            vta_s[n, h, HEAD_DIM:VT_ROWS, :] = ones_rows
        return carry

    lax.fori_loop(0, t_len // tile, body, 0)
    key_local = _iota((tile, tile), 0).astype(F32)
    for h in range(HEADS):
        bias_s[h] = key_local * (slopes[h] * LOG2E)


def _flash_t(qt_s, heads, slopes, kb_s, vta_s, bias_s, ts_s, acc_s, sel_row):
    tile = ATTN_TILE
    half = tile // 2
    n = len(heads)
    causal = _iota((tile, tile), 0) <= _iota((tile, tile), 1)

    def scores(b, slot):
        k_b = kb_s[pl.ds(pl.multiple_of(b * tile, tile), tile), :]
        for c in range(n):
            ts_s[slot][c] = _dot(k_b, qt_s[c]) + bias_s[heads[c]]

    def run(i):
        return _flash_run(i, n, heads, slopes, vta_s, ts_s, acc_s, sel_row, scores, causal)

    return (lambda: scores(0, 0)), run


def _flash_run(i, n, heads, slopes, vta_s, ts_s, acc_s, sel_row, scores, causal):
    tile = ATTN_TILE
    half = tile // 2

    def absorb(b, slot, ms, own):
        dist = jnp.full((1, half), (i - b) * tile, jnp.int32).astype(F32)
        new_ms = []
        for c in range(n):
            off = dist * (-slopes[heads[c]] * LOG2E)
            alphas, p_halves = [], []
            for hf in range(2):
                lanes = slice(hf * half, (hf + 1) * half)
                t = ts_s[slot][c, :, lanes]
                if own:
                    t = jnp.where(causal[:, lanes], t, NEG_INF)
                col_max = jnp.max(t, axis=0, keepdims=True)
                sel = None if own else sel_row(c, b, hf)
                if sel is not None:
                    col_max = jnp.where(sel > 0.5, col_max, NEG_INF)
                m_old = ms[2 * c + hf]
                m_new = jnp.maximum(m_old, col_max + off)
                sub = m_new - off
                if sel is not None:
                    sub = jnp.where(sel > 0.5, sub, float("inf"))
                new_ms.append(m_new)
                alphas.append(jnp.exp2(m_old - m_new))
                p_halves.append(jnp.exp2(t - sub).astype(BF16))
            pv = _dot(vta_s[b, heads[c]], jnp.concatenate(p_halves, axis=1))
            for hf in range(2):
                lanes = slice(hf * half, (hf + 1) * half)
                acc_s[c, :, lanes] = alphas[hf] * acc_s[c, :, lanes] + pv[:, lanes]
        return tuple(new_ms)

    def pair(p, ms):
        b = 2 * p
        scores(b + 1, 1)
        ms = absorb(b, 0, ms, own=False)
        scores(b + 2, 0)
        return absorb(b + 1, 1, ms, own=False)

    def odd_tail(r, ms):
        scores(i, 1)
        ms = absorb(i - 1, 0, ms, own=False)
        return absorb(i, 1, ms, own=True)

    def even_tail(r, ms):
        return absorb(i, 0, ms, own=True)

    acc_s[...] = jnp.zeros_like(acc_s)
    ms = tuple(jnp.full((1, half), -1e30, F32) for _ in range(2 * n))
    ms = lax.fori_loop(0, i >> 1, pair, ms)
    ms = lax.fori_loop(0, i & 1, odd_tail, ms)
    lax.fori_loop(0, 1 - (i & 1), even_tail, ms)
    return [acc_s[c] for c in range(n)]


def _moba_body(kz_ref, qvt_ref, o_ref, kb_s, vta_s, bias_s, ts_a, ts_b, qt_s, acc_s, km_s, sel_s):
    ts_s = (ts_a, ts_b)
    gw = GROUP_WIDTH
    t_len = kz_ref.shape[1]
    tile = ATTN_TILE
    nb = t_len // MOBA_BLOCK
    topk = min(MOBA_TOPK, nb)
    slopes = [2.0 ** -(2 * h + 2) for h in range(HEADS)]

    _attn_setup(kz_ref, qvt_ref, kb_s, vta_s, bias_s, slopes)

    lane_head = _iota((1, gw), 1) >> HEAD_SHIFT

    def means(n, carry):
        rows = pl.ds(pl.multiple_of(n * MOBA_BLOCK, MOBA_BLOCK), MOBA_BLOCK)
        mean = jnp.sum(kz_ref[0, rows, 0:gw], axis=0, keepdims=True) * (1.0 / MOBA_BLOCK)
        for h in range(HEADS):
            km_s[pl.ds(h * nb + n, 1), :] = jnp.where(lane_head == h, mean, 0.0)
        return carry

    lax.fori_loop(0, nb, means, 0)
    km2 = _split(km_s[...], 2)
    eye = _eye_bf16(gw)
    row_head = _iota((gw, 1), 0) >> HEAD_SHIFT
    n_idx = _iota((nb, tile), 0)

    first_scores, run = _flash_t(qt_s, list(range(HEADS)), slopes, kb_s, vta_s, bias_s, ts_s, acc_s,
                                 lambda c, j, hf: sel_s[hf, pl.ds(c * nb + j, 1), :])

    def prologue(i):
        q_t = qvt_ref[i, 0:gw, :]
        q2 = _split(q_t, 2)
        gate = _dot(km2[0], q2[0]) + (_dot(km2[0], q2[1]) + _dot(km2[1], q2[0]))
        past_blk = n_idx < i
        for h in range(HEADS):
            gm = jnp.where(past_blk, gate[h * nb:(h + 1) * nb], NEG_INF)
            cnt = jnp.zeros((nb, tile), F32)
            for n2 in range(nb):
                other = gm[n2:n2 + 1, :]
                beats = (other > gm) | ((other == gm) & (n2 < n_idx))
                cnt = cnt + jnp.where(beats, 1.0, 0.0)
            sel = jnp.where(past_blk & (cnt < topk), 1.0, 0.0)
            for hf in range(2):
                sel_s[hf, h * nb:(h + 1) * nb, :] = sel[:, hf * (tile // 2):(hf + 1) * (tile // 2)]

        q_b = (q_t * (HEAD_DIM ** -0.5 * LOG2E)).astype(BF16)
        for h in range(HEADS):
            qt_s[h] = jnp.where(row_head == h, q_b, jnp.zeros((), BF16))
        first_scores()

    n_tiles = t_len // tile

    def qtile(i, carry):
        accs = run(i)
        rows_q = pl.ds(pl.multiple_of(i * tile, tile), tile)
        o_t = jnp.concatenate([a[0:HEAD_DIM] * (1.0 / a[HEAD_DIM:HEAD_DIM + 1]) for a in accs], axis=0)
        out = _dot_nt(eye, o_t.astype(BF16))
        z = kz_ref[0, rows_q, gw:2 * gw]
        o_ref[0, rows_q, :] = (out * _silu(z)).astype(o_ref.dtype)
        prologue(jnp.minimum(i + 1, n_tiles - 1))
        return carry

    prologue(0)
    lax.fori_loop(0, n_tiles, qtile, 0)


def _attn_in_specs(t):
    gw = GROUP_WIDTH
    return [pl.BlockSpec((1, t, 2 * gw), lambda i: (i, 0, 0)),
            pl.BlockSpec((t // ATTN_TILE, 2 * gw, ATTN_TILE), lambda i: (i, 0, 0))]


def _moba(kz, qvt):
    b, t, _ = kz.shape
    gw = GROUP_WIDTH
    assert t % MOBA_BLOCK == 0 and MOBA_BLOCK == ATTN_TILE
    return pl.pallas_call(
        _moba_body,
        out_shape=jax.ShapeDtypeStruct((b, t, gw), BF16),
        grid=(b,),
        in_specs=_attn_in_specs(t),
        out_specs=pl.BlockSpec((1, t, gw), lambda i: (i, 0, 0)),
        scratch_shapes=[pltpu.VMEM((t, gw), BF16),
                        pltpu.VMEM((t // ATTN_TILE, HEADS, VT_ROWS, ATTN_TILE), BF16),
                        pltpu.VMEM((HEADS, ATTN_TILE, ATTN_TILE), F32),
                        pltpu.VMEM((HEADS, ATTN_TILE, ATTN_TILE), F32),
                        pltpu.VMEM((HEADS, ATTN_TILE, ATTN_TILE), F32),
                        pltpu.VMEM((HEADS, gw, ATTN_TILE), BF16),
                        pltpu.VMEM((HEADS, VT_ROWS, ATTN_TILE), F32),
                        pltpu.VMEM((HEADS * (t // MOBA_BLOCK), gw), F32),
                        pltpu.VMEM((2, HEADS * (t // MOBA_BLOCK), ATTN_TILE // 2), F32)],
        compiler_params=pltpu.CompilerParams(dimension_semantics=("arbitrary",),
                                             vmem_limit_bytes=VMEM_LIMIT),
        name="moba",
    )(kz, qvt)


def _diff_body(layer, kz_ref, qvt_ref, lq1_ref, lk1_ref, lq2_ref, lk2_ref, ng_ref, o_ref, kb_s, vta_s, bias_s,
               ts_a, ts_b, qt_s, acc_s):
    ts_s = (ts_a, ts_b)
    gw = GROUP_WIDTH
    t_len = kz_ref.shape[1]
    tile = ATTN_TILE
    lam_init = 0.8 - 0.6 * math.exp(-0.3 * layer)
    lam = (jnp.exp(jnp.sum(lq1_ref[...] * lk1_ref[...], axis=-1, keepdims=True))
           - jnp.exp(jnp.sum(lq2_ref[...] * lk2_ref[...], axis=-1, keepdims=True)) + lam_init)
    slopes = [2.0 ** -(2 * h + 1) for h in range(HEADS)]

    _attn_setup(kz_ref, qvt_ref, kb_s, vta_s, bias_s, slopes)
    row_map = _iota((gw, 1), 0) >> (HEAD_SHIFT - 1)
    eye = _eye_bf16(gw)

    streams = [(h, mp) for h in range(HEADS) for mp in range(2)]
    first_scores, run = _flash_t(qt_s, [h for h, _ in streams], slopes, kb_s, vta_s, bias_s, ts_s, acc_s,
                                 lambda c, j, hf: None)

    def prologue(i):
        q_t = (qvt_ref[i, 0:gw, :] * (DIFF_DIM ** -0.5 * LOG2E)).astype(BF16)
        for c, (h, mp) in enumerate(streams):
            qt_s[c] = jnp.where(row_map == 2 * h + mp, q_t, jnp.zeros((), BF16))
        first_scores()

    n_tiles = t_len // tile

    def qtile(i, carry):
        accs = run(i)
        rows_q = pl.ds(pl.multiple_of(i * tile, tile), tile)
        o_m = [a[0:HEAD_DIM] * (1.0 / a[HEAD_DIM:HEAD_DIM + 1]) for a in accs]
        ys = []
        for h in range(HEADS):
            o_h = o_m[2 * h] - lam * o_m[2 * h + 1]
            ms = jnp.mean(o_h * o_h, axis=0, keepdims=True)
            ys.append(o_h * lax.rsqrt(ms + NORM_EPS))
        out = _dot_nt(eye, jnp.concatenate(ys, axis=0).astype(BF16))
        z = kz_ref[0, rows_q, gw:2 * gw]
        o_ref[0, rows_q, :] = (out * ng_ref[...] * (1.0 - lam_init) * _silu(z)).astype(o_ref.dtype)
        prologue(jnp.minimum(i + 1, n_tiles - 1))
        return carry

    prologue(0)
    lax.fori_loop(0, n_tiles, qtile, 0)


def _diff(layer, kz, qvt, lq1, lk1, lq2, lk2, ng_exp):
    b, t, _ = kz.shape
    gw = GROUP_WIDTH
    assert t % ATTN_TILE == 0
    full = lambda shape: pl.BlockSpec(shape, lambda i: (0,) * len(shape))
    return pl.pallas_call(
        functools.partial(_diff_body, layer),
        out_shape=jax.ShapeDtypeStruct((b, t, gw), BF16),
        grid=(b,),
        in_specs=_attn_in_specs(t) + [full((1, DIFF_DIM))] * 4 + [full((1, gw))],
        out_specs=pl.BlockSpec((1, t, gw), lambda i: (i, 0, 0)),
        scratch_shapes=[pltpu.VMEM((t, gw), BF16),
                        pltpu.VMEM((t // ATTN_TILE, HEADS, VT_ROWS, ATTN_TILE), BF16),
                        pltpu.VMEM((HEADS, ATTN_TILE, ATTN_TILE), F32),
                        pltpu.VMEM((2 * HEADS, ATTN_TILE, ATTN_TILE), F32),
                        pltpu.VMEM((2 * HEADS, ATTN_TILE, ATTN_TILE), F32),
                        pltpu.VMEM((2 * HEADS, gw, ATTN_TILE), BF16),
                        pltpu.VMEM((2 * HEADS, VT_ROWS, ATTN_TILE), F32)],
        compiler_params=pltpu.CompilerParams(dimension_semantics=("arbitrary",),
                                             vmem_limit_bytes=VMEM_LIMIT),
        name="diffattn",
    )(kz, qvt, lq1, lk1, lq2, lk2, ng_exp)


def kernel(x, pre_norm_g, post_norm_g, w_in, conv_w, gdn_a_log, gdn_dt_bias, gdn_norm_g,
           hgrn_lb, hgrn_norm_g, diff_lq1, diff_lk1, diff_lq2, diff_lk2, diff_norm_g, w_out):
    b, t, d = x.shape
    depth = w_in.shape[0]
    gw = GROUP_WIDTH
    x2 = x.astype(F32).reshape(b * t, d)
    row = lambda a: a.astype(F32).reshape(1, -1)
    per_head = lambda a: jnp.repeat(a.astype(F32), HEAD_DIM).reshape(1, gw)
    per_dim = lambda a: jnp.tile(a.astype(F32), HEADS).reshape(1, gw)
    lbp = hgrn_lb.astype(F32)
    for l in range(depth):
        p_gdn, p_ab, p_hgrn, kz_moba, kz_diff, qvt_moba, qvt_diff = _inproj(
            x2, row(pre_norm_g[l]), *_reorder_w_in(w_in[l]))
        shp = lambda a: a.reshape(b, t, a.shape[-1])
        o_gdn = _gdn(shp(p_gdn), shp(p_ab), conv_w[l].astype(F32), per_head(gdn_a_log[l]),
                     per_head(gdn_dt_bias[l]), per_dim(gdn_norm_g[l]))
        o_hgrn = _hgrn(l, shp(p_hgrn), lbp, per_dim(hgrn_norm_g[l]))
        o_moba = _moba(shp(kz_moba), qvt_moba)
        o_diff = _diff(l, shp(kz_diff), qvt_diff, row(diff_lq1[l]), row(diff_lk1[l]), row(diff_lq2[l]),
                       row(diff_lk2[l]), per_dim(diff_norm_g[l]))
        os_ = [o.reshape(b * t, gw) for o in (o_gdn, o_hgrn, o_moba, o_diff)]
        x2 = _outproj(os_, x2, w_out[l].astype(BF16), row(post_norm_g[l]))
    return x2.reshape(b, t, d)
```

```python
import functools
import math

import jax
import jax.numpy as jnp
from jax import lax
from jax.experimental import pallas as pl
from jax.experimental.pallas import tpu as pltpu

F32 = jnp.float32
BF16 = jnp.bfloat16

HEADS = 4
HEAD_DIM = 64
GROUP_WIDTH = HEADS * HEAD_DIM
HEAD_SHIFT = 6
GDN_CONV = 4
GDN_CHUNK = 64
GDN_CHUNK_SHIFT = 6
GDN_GROUP = 4
HGRN_CHUNK = 16
HGRN_CHUNK_SHIFT = 4
HGRN_TILE = 128
HGRN_GROUP = 2
MOBA_BLOCK = 256
MOBA_TOPK = 3
DIFF_DIM = HEAD_DIM // 2
ATTN_TILE = 256
OUTPROJ_TILE = 2048
NORM_EPS = 1e-6
NEG_INF = float("-inf")

GDN_COLS = 4 * GROUP_WIDTH + 2 * HEADS
ROW_SEG_WIDTHS = (4 * GROUP_WIDTH, 2 * GROUP_WIDTH, 4 * GROUP_WIDTH, 2 * GROUP_WIDTH, 2 * GROUP_WIDTH)
COL_SEG_WIDTHS = (2 * GROUP_WIDTH, 2 * GROUP_WIDTH)
VMEM_LIMIT = 56 * 1024 * 1024


def _iota(shape, dim):
    return lax.broadcasted_iota(jnp.int32, shape, dim)


def _dot(a, b):
    return jnp.dot(a, b, preferred_element_type=F32)


def _dot_nt(a, b):
    return lax.dot_general(a, b, (((1,), (1,)), ((), ())), preferred_element_type=F32)


def _dot_tn(a, b):
    return lax.dot_general(a, b, (((0,), (0,)), ((), ())), preferred_element_type=F32)


def _split(x, n):
    parts = []
    r = x
    for i in range(n):
        p = r.astype(BF16)
        parts.append(p)
        if i + 1 < n:
            r = r - p.astype(F32)
    return parts


def _sigmoid(x):
    return 0.5 * jnp.tanh(0.5 * x) + 0.5


def _silu(x):
    return x * _sigmoid(x)


def _log1p_exp_neg_abs(x):
    return jnp.log(1.0 + jnp.exp(-jnp.abs(x)))


def _softplus(x):
    return jnp.maximum(x, 0.0) + _log1p_exp_neg_abs(x)


def _head_block_mask(rows, cols):
    return (_iota((rows, cols), 0) >> HEAD_SHIFT) == (_iota((rows, cols), 1) >> HEAD_SHIFT)


def _head_sum(x, ones_bd):
    return _dot(x.astype(BF16), ones_bd)


def _inproj_body(x_ref, g_ref, w_ref, wt_ref, *o_refs):
    n_t = len(ROW_SEG_WIDTHS)
    x = x_ref[...]
    ms = jnp.mean(x * x, axis=-1, keepdims=True)
    h = (x * lax.rsqrt(ms + NORM_EPS) * g_ref[...]).astype(BF16)
    off = 0
    for o in o_refs[:n_t]:
        n = o.shape[-1]
        o[...] = _dot(h, w_ref[:, off:off + n])
        off += n
    off = 0
    for o in o_refs[n_t:]:
        n = o.shape[1]
        o[0] = _dot_nt(wt_ref[off:off + n, :], h)
        off += n


def _inproj(x2, g, w, wt):
    m, d = x2.shape
    tm = ATTN_TILE
    assert w.shape[1] == sum(ROW_SEG_WIDTHS) and wt.shape[0] == sum(COL_SEG_WIDTHS) and m % tm == 0
    return pl.pallas_call(
        _inproj_body,
        out_shape=([jax.ShapeDtypeStruct((m, s), F32) for s in ROW_SEG_WIDTHS]
                   + [jax.ShapeDtypeStruct((m // tm, s, tm), F32) for s in COL_SEG_WIDTHS]),
        grid=(m // tm,),
        in_specs=[pl.BlockSpec((tm, d), lambda i: (i, 0)),
                  pl.BlockSpec((1, d), lambda i: (0, 0)),
                  pl.BlockSpec(w.shape, lambda i: (0, 0)),
                  pl.BlockSpec(wt.shape, lambda i: (0, 0))],
        out_specs=([pl.BlockSpec((tm, s), lambda i: (i, 0)) for s in ROW_SEG_WIDTHS]
                   + [pl.BlockSpec((1, s, tm), lambda i: (i, 0, 0)) for s in COL_SEG_WIDTHS]),
        compiler_params=pltpu.CompilerParams(dimension_semantics=("arbitrary",),
                                             vmem_limit_bytes=VMEM_LIMIT),
        name="inproj",
    )(x2, g, w, wt)


def _reorder_w_in(w):
    gw = GROUP_WIDTH
    q, k, v = w[:, 0:gw], w[:, gw:2 * gw], w[:, 2 * gw:3 * gw]
    a = w[:, 3 * gw:3 * gw + HEADS]
    b = w[:, 3 * gw + HEADS:3 * gw + 2 * HEADS]
    z = w[:, 3 * gw + 2 * HEADS:GDN_COLS]
    a_exp = jnp.repeat(a, HEAD_DIM, axis=1)
    b_exp = jnp.repeat(b, HEAD_DIM, axis=1)
    hgrn = w[:, GDN_COLS:GDN_COLS + 4 * gw]
    moba = w[:, GDN_COLS + 4 * gw:GDN_COLS + 8 * gw]
    diff = w[:, GDN_COLS + 8 * gw:GDN_COLS + 12 * gw]
    kz = lambda s: jnp.concatenate([s[:, gw:2 * gw], s[:, 3 * gw:4 * gw]], axis=1)
    qv = lambda s: jnp.concatenate([s[:, 0:gw], s[:, 2 * gw:3 * gw]], axis=1)
    rows = jnp.concatenate([q, k, v, z, a_exp, b_exp, hgrn, kz(moba), kz(diff)], axis=1).astype(BF16)
    cols = jnp.concatenate([qv(moba), qv(diff)], axis=1).T.astype(BF16)
    return rows, cols


def _outproj_body(o1, o2, o3, o4, x_ref, w_ref, g_ref, out_ref):
    gw = GROUP_WIDTH
    y = _dot(o1[...], w_ref[0:gw, :])
    y = y + _dot(o2[...], w_ref[gw:2 * gw, :])
    y = y + _dot(o3[...], w_ref[2 * gw:3 * gw, :])
    y = y + _dot(o4[...], w_ref[3 * gw:4 * gw, :])
    ms = jnp.mean(y * y, axis=-1, keepdims=True)
    out_ref[...] = x_ref[...] + y * lax.rsqrt(ms + NORM_EPS) * g_ref[...]


def _outproj(os_, x2, w, g):
    m, d = x2.shape
    gw = GROUP_WIDTH
    tm = min(OUTPROJ_TILE, m)
    assert m % tm == 0
    return pl.pallas_call(
        _outproj_body,
        out_shape=jax.ShapeDtypeStruct((m, d), F32),
        grid=(m // tm,),
        in_specs=[pl.BlockSpec((tm, gw), lambda i: (i, 0))] * 4 + [
            pl.BlockSpec((tm, d), lambda i: (i, 0)),
            pl.BlockSpec((d, d), lambda i: (0, 0)),
            pl.BlockSpec((1, d), lambda i: (0, 0))],
        out_specs=pl.BlockSpec((tm, d), lambda i: (i, 0)),
        compiler_params=pltpu.CompilerParams(dimension_semantics=("arbitrary",),
                                             vmem_limit_bytes=VMEM_LIMIT),
        name="outproj",
    )(*os_, x2, w, g)


def _gdn_body(p_ref, ab_ref, cw_ref, alog_ref, dtb_ref, ng_ref, o_ref,
              qdec_s, kdec_s, w_s, u_s, aqk_s, dl_s, state_s, oraw_s):
    gw = GROUP_WIDTH
    t_len = p_ref.shape[1]
    c_len = GDN_CHUNK
    n_chunks = t_len // c_len

    cw = cw_ref[...]
    a_neg = -jnp.exp(alog_ref[...])
    dtb = dtb_ref[...]
    row = _iota((c_len, gw), 0)
    s_idx = _iota((c_len, gw), 1) & (HEAD_DIM - 1)
    lower = s_idx <= row
    strict = s_idx < row
    eye = s_idx == row
    bd = _head_block_mask(gw, gw)
    ones_bd = jnp.where(bd, 1.0, 0.0).astype(BF16)
    row8 = _iota((8, 3 * gw), 0)

    def block_diag(x):
        return jnp.where(bd, jnp.concatenate([x] * HEADS, axis=0), jnp.zeros((), x.dtype))

    def mm_pairs(left, right):
        return _dot(left.astype(BF16), block_diag(right.astype(BF16)))

    group = GDN_GROUP
    g_rows = group * c_len
    r_g = _iota((g_rows, g_rows), 0)
    c_g = _iota((g_rows, g_rows), 1)
    tril_g = jnp.where((c_g <= r_g) & ((r_g >> GDN_CHUNK_SHIFT) == (c_g >> GDN_CHUNK_SHIFT)),
                       1.0, 0.0).astype(BF16)

    def pre_stages(t, tail, out):
        r0 = pl.multiple_of(t * g_rows, g_rows)
        x = p_ref[0, pl.ds(r0, g_rows), 0:3 * gw]
        out[0] = x[g_rows - 8:g_rows, :]
        acc = x * cw[GDN_CONV - 1:GDN_CONV, :]
        for j in range(GDN_CONV - 1):
            shift = GDN_CONV - 1 - j
            rolled = pltpu.roll(x, shift, axis=0)
            tail_r = pltpu.roll(tail, shift, axis=0)
            head_rows = jnp.where(row8 < shift, tail_r, rolled[0:8])
            shifted = jnp.concatenate([head_rows, rolled[8:]], axis=0)
            acc = acc + shifted * cw[j:j + 1, :]
        y = _silu(acc)
        q = y[:, 0:gw]
        k = y[:, gw:2 * gw]
        v = y[:, 2 * gw:3 * gw]
        q = q * lax.rsqrt(_head_sum(q * q, ones_bd) + NORM_EPS) * (HEAD_DIM ** -0.5)
        k = k * lax.rsqrt(_head_sum(k * k, ones_bd) + NORM_EPS)

        a = ab_ref[0, pl.ds(r0, g_rows), 0:gw]
        b = ab_ref[0, pl.ds(r0, g_rows), gw:2 * gw]
        g = a_neg * _softplus(a + dtb)
        beta = _sigmoid(b)
        g2 = _split(g, 2)
        gc = _dot(tril_g, g2[0]) + _dot(tril_g, g2[1])
        eg = jnp.exp(gc)
        kb = k * beta
        vb = (v * beta).astype(BF16)
        kbg = (kb * eg).astype(BF16)
        qdec_s[pl.ds(r0, g_rows), :] = (q * eg).astype(BF16)
        lhs = kb.astype(BF16), q.astype(BF16), k.astype(BF16)
        yield

        ms, ts = [], []
        for j in range(group):
            sl = slice(j * c_len, (j + 1) * c_len)
            gc_j = gc[sl]
            g_row = jnp.sum(jnp.where(eye, gc_j, 0.0), axis=0, keepdims=True)
            decay = jnp.exp(jnp.where(lower, gc_j - g_row, NEG_INF))
            g_last = gc_j[c_len - 1:c_len, :]
            sc = _dot_nt(jnp.concatenate([lhs[0][sl], lhs[1][sl]], axis=0), block_diag(lhs[2][sl]))
            m = -jnp.where(strict, sc[0:c_len] * decay, 0.0)
            ms.append(m)
            ts.append(jnp.where(eye, 1.0, 0.0) + m)
            rows = pl.ds(r0 + j * c_len, c_len)
            aqk_s[rows, :] = (sc[c_len:2 * c_len] * decay).astype(BF16)
            kdec_s[rows, :] = (k[sl] * jnp.exp(g_last - gc_j)).astype(BF16)
            dl_s[pl.ds(t * group + j, 1), :] = jnp.exp(g_last)
        yield

        for _ in range(5):
            ms = [mm_pairs(m, m) for m in ms]
            ts = [t_inv + mm_pairs(m, t_inv) for m, t_inv in zip(ms, ts)]
            yield
        for j in range(group):
            sl = slice(j * c_len, (j + 1) * c_len)
            rows = pl.ds(r0 + j * c_len, c_len)
            t_b = ts[j].astype(BF16)
            u_s[rows, :] = _dot(t_b, block_diag(vb[sl]))
            w_s[rows, :] = _dot(t_b, block_diag(kbg[sl])).astype(BF16)

    def seq_chunk(c):
        finish(jnp.maximum(c - 1, 0))
        r0 = pl.multiple_of(c * c_len, c_len)
        rows = pl.ds(r0, c_len)
        state = state_s[...]
        ws = _dot(jnp.concatenate([w_s[rows, :], qdec_s[rows, :]], axis=0), state.astype(BF16))
        v_new = (u_s[rows, :] - ws[0:c_len]).astype(BF16)
        oraw_s[rows, :] = ws[c_len:2 * c_len] + _dot(aqk_s[rows, :], block_diag(v_new))
        kv = _dot_tn(kdec_s[rows, :], v_new)
        state_s[...] = state * dl_s[pl.ds(c, 1), :] + jnp.where(bd, kv, 0.0)

    def finish(c):
        rows = pl.ds(pl.multiple_of(c * c_len, c_len), c_len)
        o = oraw_s[rows, :]
        z = p_ref[0, rows, 3 * gw:4 * gw]
        ms = _head_sum(o * o, ones_bd) * (1.0 / HEAD_DIM)
        o_ref[0, rows, :] = (o * lax.rsqrt(ms + NORM_EPS) * ng_ref[...] * _silu(z)).astype(o_ref.dtype)

    def step(t, tail):
        out = [None]
        stages = pre_stages(t, tail, out)
        for j in range(group):
            seq_chunk((t - 1) * group + j)
            next(stages)
        for _ in stages:
            pass
        return out[0]

    def pre_only(t, tail):
        out = [None]
        for _ in pre_stages(t, tail, out):
            pass
        return out[0]

    n_groups = n_chunks // group
    state_s[...] = jnp.zeros_like(state_s)
    oraw_s[0:c_len, :] = jnp.zeros((c_len, gw), F32)
    tail = pre_only(0, jnp.zeros((8, 3 * gw), F32))
    lax.fori_loop(1, n_groups, step, tail)
    for j in range(group):
        seq_chunk((n_groups - 1) * group + j)
    finish(n_chunks - 1)


def _gdn(p, ab, cw, alog_exp, dtb_exp, ng_exp):
    b, t, _ = p.shape
    gw = GROUP_WIDTH
    full = lambda shape: pl.BlockSpec(shape, lambda i: (0,) * len(shape))
    return pl.pallas_call(
        _gdn_body,
        out_shape=jax.ShapeDtypeStruct((b, t, gw), BF16),
        grid=(b,),
        in_specs=[pl.BlockSpec((1, t, 4 * gw), lambda i: (i, 0, 0)),
                  pl.BlockSpec((1, t, 2 * gw), lambda i: (i, 0, 0)),
                  full((GDN_CONV, 3 * gw)), full((1, gw)), full((1, gw)), full((1, gw))],
        out_specs=pl.BlockSpec((1, t, gw), lambda i: (i, 0, 0)),
        scratch_shapes=[pltpu.VMEM((t, gw), BF16), pltpu.VMEM((t, gw), BF16), pltpu.VMEM((t, gw), BF16),
                        pltpu.VMEM((t, gw), F32), pltpu.VMEM((t, gw), BF16),
                        pltpu.VMEM((t // GDN_CHUNK, gw), F32), pltpu.VMEM((gw, gw), F32),
                        pltpu.VMEM((t, gw), F32)],
        compiler_params=pltpu.CompilerParams(dimension_semantics=("arbitrary",),
                                             vmem_limit_bytes=VMEM_LIMIT),
        name="gdn",
    )(p, ab, cw, alog_exp, dtb_exp, ng_exp)


def _hgrn_body(layer, p_ref, lbp_ref, ng_ref, o_ref, qdec_s, kdec_s, oi_s, gt_s, state_s, oraw_s):
    gw = GROUP_WIDTH
    t_len = p_ref.shape[1]
    c_len = HGRN_CHUNK
    tile = HGRN_TILE
    chunks_per_tile = tile // c_len

    lbp = lbp_ref[...]
    e = jnp.exp(lbp - jnp.max(lbp, axis=0, keepdims=True))
    sm = e / jnp.sum(e, axis=0, keepdims=True)
    if layer == 0:
        lb = jnp.zeros((1, gw), F32)
    else:
        lb = jnp.sum(sm[1:layer + 1], axis=0, keepdims=True)
    log_lb = jnp.log(lb)
    log_1m = jnp.log1p(-lb)

    r = _iota((tile, tile), 0)
    c = _iota((tile, tile), 1)
    same = (r >> HGRN_CHUNK_SHIFT) == (c >> HGRN_CHUNK_SHIFT)
    mid = ((r >> HGRN_CHUNK_SHIFT) << HGRN_CHUNK_SHIFT) + c_len // 2
    causal = same & (c <= r)
    m_cum = jnp.where(causal, 1.0, 0.0)
    m_mid = (jnp.where(same & (c > mid) & (c <= r), 1.0, 0.0)
             - jnp.where(same & (c > r) & (c <= mid), 1.0, 0.0))
    m_rest = jnp.where(same & (c > r), 1.0, 0.0)
    m_tot = jnp.where((_iota((chunks_per_tile, tile), 1) >> HGRN_CHUNK_SHIFT)
                      == _iota((chunks_per_tile, tile), 0), 1.0, 0.0)
    mats = jnp.concatenate([m_cum, m_mid, m_rest, m_tot], axis=0).astype(BF16)
    lane_head = _iota((1, gw), 1) >> HEAD_SHIFT
    bd = _head_block_mask(gw, gw)
    ones_bd = jnp.where(bd, 1.0, 0.0).astype(BF16)

    group = HGRN_GROUP

    def pre(t, carry):
        tiles = []
        for j in range(group):
            rows = pl.ds(pl.multiple_of((t * group + j) * tile, tile), tile)
            q = p_ref[0, rows, 0:gw]
            f = p_ref[0, rows, gw:2 * gw]
            log_sig = jnp.minimum(f, 0.0) - _log1p_exp_neg_abs(f)
            b_ = log_1m + log_sig
            log_f = jnp.maximum(log_lb, b_) + _log1p_exp_neg_abs(log_lb - b_)
            tiles.append(dict(rows=rows, k=(1.0 - lb) * _sigmoid(-f), qd=_silu(q), f3=_split(log_f, 2),
                              vb=p_ref[0, rows, 2 * gw:3 * gw].astype(BF16)))
        for j, tl in enumerate(tiles):
            f3 = tl["f3"]
            g_all = _dot(mats, f3[0]) + _dot(mats, f3[1])
            g_mid = g_all[tile:2 * tile]
            tl["qa"] = tl["qd"] * jnp.exp(g_mid)
            tl["ka"] = (tl["k"] * jnp.exp(-g_mid)).astype(BF16)
            qdec_s[tl["rows"], :] = tl["qd"] * jnp.exp(g_all[0:tile])
            kdec_s[tl["rows"], :] = tl["k"] * jnp.exp(g_all[2 * tile:3 * tile])
            c0 = pl.multiple_of((t * group + j) * chunks_per_tile, chunks_per_tile)
            gt_s[pl.ds(c0, chunks_per_tile), :] = g_all[3 * tile:3 * tile + chunks_per_tile]
        scores = [[_dot_nt(jnp.where(lane_head == h, tl["qa"], 0.0).astype(BF16), tl["ka"])
                   for h in range(HEADS)] for tl in tiles]
        for tl, sc in zip(tiles, scores):
            oi = jnp.zeros((tile, gw), F32)
            for h in range(HEADS):
                a = jnp.where(causal, sc[h], 0.0).astype(BF16)
                oi = oi + jnp.where(lane_head == h, _dot(a, tl["vb"]), 0.0)
            oi_s[tl["rows"], :] = oi
        return carry

    lax.fori_loop(0, t_len // (tile * group), pre, 0)

    state_s[...] = jnp.zeros_like(state_s)

    def seq(t, carry):
        finish(jnp.maximum(t - 1, 0))
        rows = pl.ds(pl.multiple_of(t * tile, tile), tile)
        kd = kdec_s[rows, :]
        qd = qdec_s[rows, :]
        vb = p_ref[0, rows, 2 * gw:3 * gw].astype(BF16)
        gt = gt_s[pl.ds(pl.multiple_of(t * chunks_per_tile, chunks_per_tile), chunks_per_tile), :]
        chunk = lambda x, ci: x[ci * c_len:(ci + 1) * c_len]
        log_b = [jnp.zeros((1, gw), F32)]
        for ci in range(chunks_per_tile):
            log_b.append(log_b[ci] + gt[ci:ci + 1, :])

        def keys_decayed_to(c):
            return jnp.concatenate([(chunk(kd, cp) * jnp.exp(log_b[c] - log_b[cp + 1])).astype(BF16)
                                    for cp in range(c)], axis=0)

        state_t = state_s[...]
        decay_rows = jnp.concatenate([jnp.broadcast_to(jnp.exp(log_b[ci]), (c_len, gw))
                                      for ci in range(chunks_per_tile)], axis=0)
        o = _dot_nt((qd * decay_rows).astype(BF16), state_t.astype(BF16)) + oi_s[rows, :]
        qb = qd.astype(BF16)
        later = range(1, chunks_per_tile)
        keys = [keys_decayed_to(ci) for ci in later]
        kv_t = _dot_tn(vb, keys_decayed_to(chunks_per_tile))
        q_heads = [jnp.concatenate([jnp.where(lane_head == h, chunk(qb, ci), jnp.zeros((), BF16))
                                    for h in range(HEADS)], axis=0) for ci in later]
        scs = [_dot_nt(qh, ks) for qh, ks in zip(q_heads, keys)]
        pvs = [_dot(sc.astype(BF16), vb[0:ci * c_len]) for sc, ci in zip(scs, later)]
        outs = [chunk(o, 0)]
        for pv, ci in zip(pvs, later):
            cross = jnp.zeros((c_len, gw), F32)
            for h in range(HEADS):
                cross = cross + jnp.where(lane_head == h, pv[h * c_len:(h + 1) * c_len], 0.0)
            outs.append(chunk(o, ci) + cross)
        oraw_s[rows, :] = jnp.concatenate(outs, axis=0)
        state_s[...] = state_t * jnp.exp(log_b[chunks_per_tile]) + jnp.where(bd, kv_t, 0.0)
        return carry

    def finish(t):
        rows = pl.ds(pl.multiple_of(t * tile, tile), tile)
        o = oraw_s[rows, :]
        z = p_ref[0, rows, 3 * gw:4 * gw]
        ms = _head_sum(o * o, ones_bd) * (1.0 / HEAD_DIM)
        o_ref[0, rows, :] = (o * lax.rsqrt(ms + NORM_EPS) * ng_ref[...] * _silu(z)).astype(o_ref.dtype)

    oraw_s[0:tile, :] = jnp.zeros((tile, gw), F32)
    lax.fori_loop(0, t_len // tile, seq, 0)
    finish(t_len // tile - 1)


def _hgrn(layer, p, lbp, ng_exp):
    b, t, _ = p.shape
    gw = GROUP_WIDTH
    depth = lbp.shape[0]
    full = lambda shape: pl.BlockSpec(shape, lambda i: (0,) * len(shape))
    return pl.pallas_call(
        functools.partial(_hgrn_body, layer),
        out_shape=jax.ShapeDtypeStruct((b, t, gw), BF16),
        grid=(b,),
        in_specs=[pl.BlockSpec((1, t, 4 * gw), lambda i: (i, 0, 0)), full((depth, gw)), full((1, gw))],
        out_specs=pl.BlockSpec((1, t, gw), lambda i: (i, 0, 0)),
        scratch_shapes=[pltpu.VMEM((t, gw), F32), pltpu.VMEM((t, gw), F32), pltpu.VMEM((t, gw), F32),
                        pltpu.VMEM((t // HGRN_CHUNK, gw), F32), pltpu.VMEM((gw, gw), F32),
                        pltpu.VMEM((t, gw), F32)],
        compiler_params=pltpu.CompilerParams(dimension_semantics=("arbitrary",),
                                             vmem_limit_bytes=VMEM_LIMIT),
        name="hgrn2",
    )(p, lbp, ng_exp)


VT_ROWS = HEAD_DIM + 16
LOG2E = 1.4426950408889634


def _eye_bf16(n):
    return jnp.where(_iota((n, n), 0) == _iota((n, n), 1), 1.0, 0.0).astype(BF16)


def _attn_setup(kz_ref, qvt_ref, kb_s, vta_s, bias_s, slopes):
    gw = GROUP_WIDTH
    t_len = kz_ref.shape[1]
    tile = ATTN_TILE
    ones_rows = jnp.ones((VT_ROWS - HEAD_DIM, tile), BF16)

    def body(n, carry):
        rows = pl.ds(pl.multiple_of(n * tile, tile), tile)
        kb_s[rows, :] = kz_ref[0, rows, 0:gw].astype(BF16)
        for h in range(HEADS):
            vta_s[n, h, 0:HEAD_DIM, :] = qvt_ref[n, gw + h * HEAD_DIM:gw + (h + 1) * HEAD_DIM, :].astype(BF16)
            vta_s[n, h, HEAD_DIM:VT_ROWS, :] = ones_rows
        return carry

    lax.fori_loop(0, t_len // tile, body, 0)
    key_local = _iota((tile, tile), 0).astype(F32)
    for h in range(HEADS):
        bias_s[h] = key_local * (slopes[h] * LOG2E)


def _flash_t(qt_s, heads, slopes, kb_s, vta_s, bias_s, ts_s, acc_s, sel_row):
    tile = ATTN_TILE
    half = tile // 2
    n = len(heads)
    causal = _iota((tile, tile), 0) <= _iota((tile, tile), 1)

    def scores(b, slot):
        k_b = kb_s[pl.ds(pl.multiple_of(b * tile, tile), tile), :]
        for c in range(n):
            ts_s[slot][c] = _dot(k_b, qt_s[c]) + bias_s[heads[c]]

    def run(i):
        return _flash_run(i, n, heads, slopes, vta_s, ts_s, acc_s, sel_row, scores, causal)

    return (lambda: scores(0, 0)), run


def _flash_run(i, n, heads, slopes, vta_s, ts_s, acc_s, sel_row, scores, causal):
    tile = ATTN_TILE
    half = tile // 2

    def absorb(b, slot, ms, own):
        dist = jnp.full((1, half), (i - b) * tile, jnp.int32).astype(F32)
        new_ms = []
        for c in range(n):
            off = dist * (-slopes[heads[c]] * LOG2E)
            alphas, p_halves = [], []
            for hf in range(2):
                lanes = slice(hf * half, (hf + 1) * half)
                t = ts_s[slot][c, :, lanes]
                if own:
                    t = jnp.where(causal[:, lanes], t, NEG_INF)
                col_max = jnp.max(t, axis=0, keepdims=True)
                sel = None if own else sel_row(c, b, hf)
                if sel is not None:
                    col_max = jnp.where(sel > 0.5, col_max, NEG_INF)
                m_old = ms[2 * c + hf]
                m_new = jnp.maximum(m_old, col_max + off)
                sub = m_new - off
                if sel is not None:
                    sub = jnp.where(sel > 0.5, sub, float("inf"))
                new_ms.append(m_new)
                alphas.append(jnp.exp2(m_old - m_new))
                p_halves.append(jnp.exp2(t - sub).astype(BF16))
            pv = _dot(vta_s[b, heads[c]], jnp.concatenate(p_halves, axis=1))
            for hf in range(2):
                lanes = slice(hf * half, (hf + 1) * half)
                acc_s[c, :, lanes] = alphas[hf] * acc_s[c, :, lanes] + pv[:, lanes]
        return tuple(new_ms)

    def pair(p, ms):
        b = 2 * p
        scores(b + 1, 1)
        ms = absorb(b, 0, ms, own=False)
        scores(b + 2, 0)
        return absorb(b + 1, 1, ms, own=False)

    def odd_tail(r, ms):
        scores(i, 1)
        ms = absorb(i - 1, 0, ms, own=False)
        return absorb(i, 1, ms, own=True)

    def even_tail(r, ms):
        return absorb(i, 0, ms, own=True)

    acc_s[...] = jnp.zeros_like(acc_s)
    ms = tuple(jnp.full((1, half), -1e30, F32) for _ in range(2 * n))
    ms = lax.fori_loop(0, i >> 1, pair, ms)
    ms = lax.fori_loop(0, i & 1, odd_tail, ms)
    lax.fori_loop(0, 1 - (i & 1), even_tail, ms)
    return [acc_s[c] for c in range(n)]


def _moba_body(kz_ref, qvt_ref, o_ref, kb_s, vta_s, bias_s, ts_a, ts_b, qt_s, acc_s, km_s, sel_s):
    ts_s = (ts_a, ts_b)
    gw = GROUP_WIDTH
    t_len = kz_ref.shape[1]
    tile = ATTN_TILE
    nb = t_len // MOBA_BLOCK
    topk = min(MOBA_TOPK, nb)
    slopes = [2.0 ** -(2 * h + 2) for h in range(HEADS)]

    _attn_setup(kz_ref, qvt_ref, kb_s, vta_s, bias_s, slopes)

    lane_head = _iota((1, gw), 1) >> HEAD_SHIFT

    def means(n, carry):
        rows = pl.ds(pl.multiple_of(n * MOBA_BLOCK, MOBA_BLOCK), MOBA_BLOCK)
        mean = jnp.sum(kz_ref[0, rows, 0:gw], axis=0, keepdims=True) * (1.0 / MOBA_BLOCK)
        for h in range(HEADS):
            km_s[pl.ds(h * nb + n, 1), :] = jnp.where(lane_head == h, mean, 0.0)
        return carry

    lax.fori_loop(0, nb, means, 0)
    km2 = _split(km_s[...], 2)
    eye = _eye_bf16(gw)
    row_head = _iota((gw, 1), 0) >> HEAD_SHIFT
    n_idx = _iota((nb, tile), 0)

    first_scores, run = _flash_t(qt_s, list(range(HEADS)), slopes, kb_s, vta_s, bias_s, ts_s, acc_s,
                                 lambda c, j, hf: sel_s[hf, pl.ds(c * nb + j, 1), :])

    def prologue(i):
        q_t = qvt_ref[i, 0:gw, :]
        q2 = _split(q_t, 2)
        gate = _dot(km2[0], q2[0]) + (_dot(km2[0], q2[1]) + _dot(km2[1], q2[0]))
        past_blk = n_idx < i
        for h in range(HEADS):
            gm = jnp.where(past_blk, gate[h * nb:(h + 1) * nb], NEG_INF)
            cnt = jnp.zeros((nb, tile), F32)
            for n2 in range(nb):
                other = gm[n2:n2 + 1, :]
                beats = (other > gm) | ((other == gm) & (n2 < n_idx))
                cnt = cnt + jnp.where(beats, 1.0, 0.0)
            sel = jnp.where(past_blk & (cnt < topk), 1.0, 0.0)
            for hf in range(2):
                sel_s[hf, h * nb:(h + 1) * nb, :] = sel[:, hf * (tile // 2):(hf + 1) * (tile // 2)]

        q_b = (q_t * (HEAD_DIM ** -0.5 * LOG2E)).astype(BF16)
        for h in range(HEADS):
            qt_s[h] = jnp.where(row_head == h, q_b, jnp.zeros((), BF16))
        first_scores()

    n_tiles = t_len // tile

    def qtile(i, carry):
        accs = run(i)
        rows_q = pl.ds(pl.multiple_of(i * tile, tile), tile)
        o_t = jnp.concatenate([a[0:HEAD_DIM] * (1.0 / a[HEAD_DIM:HEAD_DIM + 1]) for a in accs], axis=0)
        out = _dot_nt(eye, o_t.astype(BF16))
        z = kz_ref[0, rows_q, gw:2 * gw]
        o_ref[0, rows_q, :] = (out * _silu(z)).astype(o_ref.dtype)
        prologue(jnp.minimum(i + 1, n_tiles - 1))
        return carry

    prologue(0)
    lax.fori_loop(0, n_tiles, qtile, 0)


def _attn_in_specs(t):
    gw = GROUP_WIDTH
    return [pl.BlockSpec((1, t, 2 * gw), lambda i: (i, 0, 0)),
            pl.BlockSpec((t // ATTN_TILE, 2 * gw, ATTN_TILE), lambda i: (i, 0, 0))]


def _moba(kz, qvt):
    b, t, _ = kz.shape
    gw = GROUP_WIDTH
    assert t % MOBA_BLOCK == 0 and MOBA_BLOCK == ATTN_TILE
    return pl.pallas_call(
        _moba_body,
        out_shape=jax.ShapeDtypeStruct((b, t, gw), BF16),
        grid=(b,),
        in_specs=_attn_in_specs(t),
        out_specs=pl.BlockSpec((1, t, gw), lambda i: (i, 0, 0)),
        scratch_shapes=[pltpu.VMEM((t, gw), BF16),
                        pltpu.VMEM((t // ATTN_TILE, HEADS, VT_ROWS, ATTN_TILE), BF16),
                        pltpu.VMEM((HEADS, ATTN_TILE, ATTN_TILE), F32),
                        pltpu.VMEM((HEADS, ATTN_TILE, ATTN_TILE), F32),
                        pltpu.VMEM((HEADS, ATTN_TILE, ATTN_TILE), F32),
                        pltpu.VMEM((HEADS, gw, ATTN_TILE), BF16),
                        pltpu.VMEM((HEADS, VT_ROWS, ATTN_TILE), F32),
                        pltpu.VMEM((HEADS * (t // MOBA_BLOCK), gw), F32),
                        pltpu.VMEM((2, HEADS * (t // MOBA_BLOCK), ATTN_TILE // 2), F32)],
        compiler_params=pltpu.CompilerParams(dimension_semantics=("arbitrary",),
                                             vmem_limit_bytes=VMEM_LIMIT),
        name="moba",
    )(kz, qvt)


def _diff_body(layer, kz_ref, qvt_ref, lq1_ref, lk1_ref, lq2_ref, lk2_ref, ng_ref, o_ref, kb_s, vta_s, bias_s,
               ts_a, ts_b, qt_s, acc_s):
    ts_s = (ts_a, ts_b)
    gw = GROUP_WIDTH
    t_len = kz_ref.shape[1]
    tile = ATTN_TILE
    lam_init = 0.8 - 0.6 * math.exp(-0.3 * layer)
    lam = (jnp.exp(jnp.sum(lq1_ref[...] * lk1_ref[...], axis=-1, keepdims=True))
           - jnp.exp(jnp.sum(lq2_ref[...] * lk2_ref[...], axis=-1, keepdims=True)) + lam_init)
    slopes = [2.0 ** -(2 * h + 1) for h in range(HEADS)]

    _attn_setup(kz_ref, qvt_ref, kb_s, vta_s, bias_s, slopes)
    row_map = _iota((gw, 1), 0) >> (HEAD_SHIFT - 1)
    eye = _eye_bf16(gw)

    streams = [(h, mp) for h in range(HEADS) for mp in range(2)]
    first_scores, run = _flash_t(qt_s, [h for h, _ in streams], slopes, kb_s, vta_s, bias_s, ts_s, acc_s,
                                 lambda c, j, hf: None)

    def prologue(i):
        q_t = (qvt_ref[i, 0:gw, :] * (DIFF_DIM ** -0.5 * LOG2E)).astype(BF16)
        for c, (h, mp) in enumerate(streams):
            qt_s[c] = jnp.where(row_map == 2 * h + mp, q_t, jnp.zeros((), BF16))
        first_scores()

    n_tiles = t_len // tile

    def qtile(i, carry):
        accs = run(i)
        rows_q = pl.ds(pl.multiple_of(i * tile, tile), tile)
        o_m = [a[0:HEAD_DIM] * (1.0 / a[HEAD_DIM:HEAD_DIM + 1]) for a in accs]
        ys = []
        for h in range(HEADS):
            o_h = o_m[2 * h] - lam * o_m[2 * h + 1]
            ms = jnp.mean(o_h * o_h, axis=0, keepdims=True)
            ys.append(o_h * lax.rsqrt(ms + NORM_EPS))
        out = _dot_nt(eye, jnp.concatenate(ys, axis=0).astype(BF16))
        z = kz_ref[0, rows_q, gw:2 * gw]
        o_ref[0, rows_q, :] = (out * ng_ref[...] * (1.0 - lam_init) * _silu(z)).astype(o_ref.dtype)
        prologue(jnp.minimum(i + 1, n_tiles - 1))
        return carry

    prologue(0)
    lax.fori_loop(0, n_tiles, qtile, 0)


def _diff(layer, kz, qvt, lq1, lk1, lq2, lk2, ng_exp):
    b, t, _ = kz.shape
    gw = GROUP_WIDTH
    assert t % ATTN_TILE == 0
    full = lambda shape: pl.BlockSpec(shape, lambda i: (0,) * len(shape))
    return pl.pallas_call(
        functools.partial(_diff_body, layer),
        out_shape=jax.ShapeDtypeStruct((b, t, gw), BF16),
        grid=(b,),
        in_specs=_attn_in_specs(t) + [full((1, DIFF_DIM))] * 4 + [full((1, gw))],
        out_specs=pl.BlockSpec((1, t, gw), lambda i: (i, 0, 0)),
        scratch_shapes=[pltpu.VMEM((t, gw), BF16),
                        pltpu.VMEM((t // ATTN_TILE, HEADS, VT_ROWS, ATTN_TILE), BF16),
                        pltpu.VMEM((HEADS, ATTN_TILE, ATTN_TILE), F32),
                        pltpu.VMEM((2 * HEADS, ATTN_TILE, ATTN_TILE), F32),
                        pltpu.VMEM((2 * HEADS, ATTN_TILE, ATTN_TILE), F32),
                        pltpu.VMEM((2 * HEADS, gw, ATTN_TILE), BF16),
                        pltpu.VMEM((2 * HEADS, VT_ROWS, ATTN_TILE), F32)],
        compiler_params=pltpu.CompilerParams(dimension_semantics=("arbitrary",),
                                             vmem_limit_bytes=VMEM_LIMIT),
        name="diffattn",
    )(kz, qvt, lq1, lk1, lq2, lk2, ng_exp)


def kernel(x, pre_norm_g, post_norm_g, w_in, conv_w, gdn_a_log, gdn_dt_bias, gdn_norm_g,
           hgrn_lb, hgrn_norm_g, diff_lq1, diff_lk1, diff_lq2, diff_lk2, diff_norm_g, w_out):
    b, t, d = x.shape
    depth = w_in.shape[0]
    gw = GROUP_WIDTH
    x2 = x.astype(F32).reshape(b * t, d)
    row = lambda a: a.astype(F32).reshape(1, -1)
    per_head = lambda a: jnp.repeat(a.astype(F32), HEAD_DIM).reshape(1, gw)
    per_dim = lambda a: jnp.tile(a.astype(F32), HEADS).reshape(1, gw)
    lbp = hgrn_lb.astype(F32)
    for l in range(depth):
        p_gdn, p_ab, p_hgrn, kz_moba, kz_diff, qvt_moba, qvt_diff = _inproj(
            x2, row(pre_norm_g[l]), *_reorder_w_in(w_in[l]))
        shp = lambda a: a.reshape(b, t, a.shape[-1])
        o_gdn = _gdn(shp(p_gdn), shp(p_ab), conv_w[l].astype(F32), per_head(gdn_a_log[l]),
                     per_head(gdn_dt_bias[l]), per_dim(gdn_norm_g[l]))
        o_hgrn = _hgrn(l, shp(p_hgrn), lbp, per_dim(hgrn_norm_g[l]))
        o_moba = _moba(shp(kz_moba), qvt_moba)
        o_diff = _diff(l, shp(kz_diff), qvt_diff, row(diff_lq1[l]), row(diff_lk1[l]), row(diff_lq2[l]),
                       row(diff_lk2[l]), per_dim(diff_norm_g[l]))
        os_ = [o.reshape(b * t, gw) for o in (o_gdn, o_hgrn, o_moba, o_diff)]
        x2 = _outproj(os_, x2, w_out[l].astype(BF16), row(post_norm_g[l]))
    return x2.reshape(b, t, d)
```

```python
import functools
import math

import jax
import jax.numpy as jnp
from jax import lax
from jax.experimental import pallas as pl
from jax.experimental.pallas import tpu as pltpu

F32 = jnp.float32
BF16 = jnp.bfloat16

HEADS = 4
HEAD_DIM = 64
GROUP_WIDTH = HEADS * HEAD_DIM
HEAD_SHIFT = 6
GDN_CONV = 4
GDN_CHUNK = 64
GDN_CHUNK_SHIFT = 6
GDN_GROUP = 4
HGRN_CHUNK = 16
HGRN_CHUNK_SHIFT = 4
HGRN_TILE = 128
HGRN_GROUP = 2
MOBA_BLOCK = 256
MOBA_TOPK = 3
DIFF_DIM = HEAD_DIM // 2
ATTN_TILE = 256
OUTPROJ_TILE = 2048
NORM_EPS = 1e-6
NEG_INF = float("-inf")

GDN_COLS = 4 * GROUP_WIDTH + 2 * HEADS
ROW_SEG_WIDTHS = (4 * GROUP_WIDTH, 2 * GROUP_WIDTH, 4 * GROUP_WIDTH, 2 * GROUP_WIDTH, 2 * GROUP_WIDTH)
COL_SEG_WIDTHS = (2 * GROUP_WIDTH, 2 * GROUP_WIDTH)
VMEM_LIMIT = 56 * 1024 * 1024


def _iota(shape, dim):
    return lax.broadcasted_iota(jnp.int32, shape, dim)


def _dot(a, b):
    return jnp.dot(a, b, preferred_element_type=F32)


def _dot_nt(a, b):
    return lax.dot_general(a, b, (((1,), (1,)), ((), ())), preferred_element_type=F32)


def _dot_tn(a, b):
    return lax.dot_general(a, b, (((0,), (0,)), ((), ())), preferred_element_type=F32)


def _split(x, n):
    parts = []
    r = x
    for i in range(n):
        p = r.astype(BF16)
        parts.append(p)
        if i + 1 < n:
            r = r - p.astype(F32)
    return parts


def _sigmoid(x):
    return 0.5 * jnp.tanh(0.5 * x) + 0.5


def _silu(x):
    return x * _sigmoid(x)


def _log1p_exp_neg_abs(x):
    return jnp.log(1.0 + jnp.exp(-jnp.abs(x)))


def _softplus(x):
    return jnp.maximum(x, 0.0) + _log1p_exp_neg_abs(x)


def _head_block_mask(rows, cols):
    return (_iota((rows, cols), 0) >> HEAD_SHIFT) == (_iota((rows, cols), 1) >> HEAD_SHIFT)


def _head_sum(x, ones_bd):
    return _dot(x.astype(BF16), ones_bd)


def _inproj_body(x_ref, g_ref, w_ref, wt_ref, *o_refs):
    n_t = len(ROW_SEG_WIDTHS)
    x = x_ref[...]
    ms = jnp.mean(x * x, axis=-1, keepdims=True)
    h = (x * lax.rsqrt(ms + NORM_EPS) * g_ref[...]).astype(BF16)
    off = 0
    for o in o_refs[:n_t]:
        n = o.shape[-1]
        o[...] = _dot(h, w_ref[:, off:off + n])
        off += n
    off = 0
    for o in o_refs[n_t:]:
        n = o.shape[1]
        o[0] = _dot_nt(wt_ref[off:off + n, :], h)
        off += n


def _inproj(x2, g, w, wt):
    m, d = x2.shape
    tm = ATTN_TILE
    assert w.shape[1] == sum(ROW_SEG_WIDTHS) and wt.shape[0] == sum(COL_SEG_WIDTHS) and m % tm == 0
    return pl.pallas_call(
        _inproj_body,
        out_shape=([jax.ShapeDtypeStruct((m, s), F32) for s in ROW_SEG_WIDTHS]
                   + [jax.ShapeDtypeStruct((m // tm, s, tm), F32) for s in COL_SEG_WIDTHS]),
        grid=(m // tm,),
        in_specs=[pl.BlockSpec((tm, d), lambda i: (i, 0)),
                  pl.BlockSpec((1, d), lambda i: (0, 0)),
                  pl.BlockSpec(w.shape, lambda i: (0, 0)),
                  pl.BlockSpec(wt.shape, lambda i: (0, 0))],
        out_specs=([pl.BlockSpec((tm, s), lambda i: (i, 0)) for s in ROW_SEG_WIDTHS]
                   + [pl.BlockSpec((1, s, tm), lambda i: (i, 0, 0)) for s in COL_SEG_WIDTHS]),
        compiler_params=pltpu.CompilerParams(dimension_semantics=("arbitrary",),
                                             vmem_limit_bytes=VMEM_LIMIT),
        name="inproj",
    )(x2, g, w, wt)


def _reorder_w_in(w):
    gw = GROUP_WIDTH
    q, k, v = w[:, 0:gw], w[:, gw:2 * gw], w[:, 2 * gw:3 * gw]
    a = w[:, 3 * gw:3 * gw + HEADS]
    b = w[:, 3 * gw + HEADS:3 * gw + 2 * HEADS]
    z = w[:, 3 * gw + 2 * HEADS:GDN_COLS]
    a_exp = jnp.repeat(a, HEAD_DIM, axis=1)
    b_exp = jnp.repeat(b, HEAD_DIM, axis=1)
    hgrn = w[:, GDN_COLS:GDN_COLS + 4 * gw]
    moba = w[:, GDN_COLS + 4 * gw:GDN_COLS + 8 * gw]
    diff = w[:, GDN_COLS + 8 * gw:GDN_COLS + 12 * gw]
    kz = lambda s: jnp.concatenate([s[:, gw:2 * gw], s[:, 3 * gw:4 * gw]], axis=1)
    qv = lambda s: jnp.concatenate([s[:, 0:gw], s[:, 2 * gw:3 * gw]], axis=1)
    rows = jnp.concatenate([q, k, v, z, a_exp, b_exp, hgrn, kz(moba), kz(diff)], axis=1).astype(BF16)
    cols = jnp.concatenate([qv(moba), qv(diff)], axis=1).T.astype(BF16)
    return rows, cols


def _outproj_body(o1, o2, o3, o4, x_ref, w_ref, g_ref, out_ref):
    gw = GROUP_WIDTH
    y = _dot(o1[...], w_ref[0:gw, :])
    y = y + _dot(o2[...], w_ref[gw:2 * gw, :])
    y = y + _dot(o3[...], w_ref[2 * gw:3 * gw, :])
    y = y + _dot(o4[...], w_ref[3 * gw:4 * gw, :])
    ms = jnp.mean(y * y, axis=-1, keepdims=True)
    out_ref[...] = x_ref[...] + y * lax.rsqrt(ms + NORM_EPS) * g_ref[...]


def _outproj(os_, x2, w, g):
    m, d = x2.shape
    gw = GROUP_WIDTH
    tm = min(OUTPROJ_TILE, m)
    assert m % tm == 0
    return pl.pallas_call(
        _outproj_body,
        out_shape=jax.ShapeDtypeStruct((m, d), F32),
        grid=(m // tm,),
        in_specs=[pl.BlockSpec((tm, gw), lambda i: (i, 0))] * 4 + [
            pl.BlockSpec((tm, d), lambda i: (i, 0)),
            pl.BlockSpec((d, d), lambda i: (0, 0)),
            pl.BlockSpec((1, d), lambda i: (0, 0))],
        out_specs=pl.BlockSpec((tm, d), lambda i: (i, 0)),
        compiler_params=pltpu.CompilerParams(dimension_semantics=("arbitrary",),
                                             vmem_limit_bytes=VMEM_LIMIT),
        name="outproj",
    )(*os_, x2, w, g)


def _gdn_body(p_ref, ab_ref, cw_ref, alog_ref, dtb_ref, ng_ref, o_ref,
              qdec_s, kdec_s, w_s, u_s, aqk_s, dl_s, state_s, oraw_s):
    gw = GROUP_WIDTH
    t_len = p_ref.shape[1]
    c_len = GDN_CHUNK
    n_chunks = t_len // c_len

    cw = cw_ref[...]
    a_neg = -jnp.exp(alog_ref[...])
    dtb = dtb_ref[...]
    row = _iota((c_len, gw), 0)
    s_idx = _iota((c_len, gw), 1) & (HEAD_DIM - 1)
    lower = s_idx <= row
    strict = s_idx < row
    eye = s_idx == row
    bd = _head_block_mask(gw, gw)
    ones_bd = jnp.where(bd, 1.0, 0.0).astype(BF16)
    row8 = _iota((8, 3 * gw), 0)

    def block_diag(x):
        return jnp.where(bd, jnp.concatenate([x] * HEADS, axis=0), jnp.zeros((), x.dtype))

    def mm_pairs(left, right):
        return _dot(left.astype(BF16), block_diag(right.astype(BF16)))

    group = GDN_GROUP
    g_rows = group * c_len
    r_g = _iota((g_rows, g_rows), 0)
    c_g = _iota((g_rows, g_rows), 1)
    tril_g = jnp.where((c_g <= r_g) & ((r_g >> GDN_CHUNK_SHIFT) == (c_g >> GDN_CHUNK_SHIFT)),
                       1.0, 0.0).astype(BF16)

    def pre_stages(t, tail, out):
        r0 = pl.multiple_of(t * g_rows, g_rows)
        x = p_ref[0, pl.ds(r0, g_rows), 0:3 * gw]
        out[0] = x[g_rows - 8:g_rows, :]
        acc = x * cw[GDN_CONV - 1:GDN_CONV, :]
        for j in range(GDN_CONV - 1):
            shift = GDN_CONV - 1 - j
            rolled = pltpu.roll(x, shift, axis=0)
            tail_r = pltpu.roll(tail, shift, axis=0)
            head_rows = jnp.where(row8 < shift, tail_r, rolled[0:8])
            shifted = jnp.concatenate([head_rows, rolled[8:]], axis=0)
            acc = acc + shifted * cw[j:j + 1, :]
        y = _silu(acc)
        q = y[:, 0:gw]
        k = y[:, gw:2 * gw]
        v = y[:, 2 * gw:3 * gw]
        q = q * lax.rsqrt(_head_sum(q * q, ones_bd) + NORM_EPS) * (HEAD_DIM ** -0.5)
        k = k * lax.rsqrt(_head_sum(k * k, ones_bd) + NORM_EPS)

        a = ab_ref[0, pl.ds(r0, g_rows), 0:gw]
        b = ab_ref[0, pl.ds(r0, g_rows), gw:2 * gw]
        g = a_neg * _softplus(a + dtb)
        beta = _sigmoid(b)
        g2 = _split(g, 2)
        gc = _dot(tril_g, g2[0]) + _dot(tril_g, g2[1])
        eg = jnp.exp(gc)
        kb = k * beta
        vb = (v * beta).astype(BF16)
        kbg = (kb * eg).astype(BF16)
        qdec_s[pl.ds(r0, g_rows), :] = (q * eg).astype(BF16)
        lhs = kb.astype(BF16), q.astype(BF16), k.astype(BF16)
        yield

        ms, ts = [], []
        for j in range(group):
            sl = slice(j * c_len, (j + 1) * c_len)
            gc_j = gc[sl]
            g_row = jnp.sum(jnp.where(eye, gc_j, 0.0), axis=0, keepdims=True)
            decay = jnp.exp(jnp.where(lower, gc_j - g_row, NEG_INF))
            g_last = gc_j[c_len - 1:c_len, :]
            sc = _dot_nt(jnp.concatenate([lhs[0][sl], lhs[1][sl]], axis=0), block_diag(lhs[2][sl]))
            m = -jnp.where(strict, sc[0:c_len] * decay, 0.0)
            ms.append(m)
            ts.append(jnp.where(eye, 1.0, 0.0) + m)
            rows = pl.ds(r0 + j * c_len, c_len)
            aqk_s[rows, :] = (sc[c_len:2 * c_len] * decay).astype(BF16)
            kdec_s[rows, :] = (k[sl] * jnp.exp(g_last - gc_j)).astype(BF16)
            dl_s[pl.ds(t * group + j, 1), :] = jnp.exp(g_last)
        yield

        for _ in range(5):
            ms = [mm_pairs(m, m) for m in ms]
            ts = [t_inv + mm_pairs(m, t_inv) for m, t_inv in zip(ms, ts)]
            yield
        for j in range(group):
            sl = slice(j * c_len, (j + 1) * c_len)
            rows = pl.ds(r0 + j * c_len, c_len)
            t_b = ts[j].astype(BF16)
            u_s[rows, :] = _dot(t_b, block_diag(vb[sl]))
            w_s[rows, :] = _dot(t_b, block_diag(kbg[sl])).astype(BF16)

    def seq_chunk(c):
        finish(jnp.maximum(c - 1, 0))
        r0 = pl.multiple_of(c * c_len, c_len)
        rows = pl.ds(r0, c_len)
        state = state_s[...]
        ws = _dot(jnp.concatenate([w_s[rows, :], qdec_s[rows, :]], axis=0), state.astype(BF16))
        v_new = (u_s[rows, :] - ws[0:c_len]).astype(BF16)
        oraw_s[rows, :] = ws[c_len:2 * c_len] + _dot(aqk_s[rows, :], block_diag(v_new))
        kv = _dot_tn(kdec_s[rows, :], v_new)
        state_s[...] = state * dl_s[pl.ds(c, 1), :] + jnp.where(bd, kv, 0.0)

    def finish(c):
        rows = pl.ds(pl.multiple_of(c * c_len, c_len), c_len)
        o = oraw_s[rows, :]
        z = p_ref[0, rows, 3 * gw:4 * gw]
        ms = _head_sum(o * o, ones_bd) * (1.0 / HEAD_DIM)
        o_ref[0, rows, :] = (o * lax.rsqrt(ms + NORM_EPS) * ng_ref[...] * _silu(z)).astype(o_ref.dtype)

    def step(t, tail):
        out = [None]
        stages = pre_stages(t, tail, out)
        for j in range(group):
            seq_chunk((t - 1) * group + j)
            next(stages)
        for _ in stages:
            pass
        return out[0]

    def pre_only(t, tail):
        out = [None]
        for _ in pre_stages(t, tail, out):
            pass
        return out[0]

    n_groups = n_chunks // group
    state_s[...] = jnp.zeros_like(state_s)
    oraw_s[0:c_len, :] = jnp.zeros((c_len, gw), F32)
    tail = pre_only(0, jnp.zeros((8, 3 * gw), F32))
    lax.fori_loop(1, n_groups, step, tail)
    for j in range(group):
        seq_chunk((n_groups - 1) * group + j)
    finish(n_chunks - 1)


def _gdn(p, ab, cw, alog_exp, dtb_exp, ng_exp):
    b, t, _ = p.shape
    gw = GROUP_WIDTH
    full = lambda shape: pl.BlockSpec(shape, lambda i: (0,) * len(shape))
    return pl.pallas_call(
        _gdn_body,
        out_shape=jax.ShapeDtypeStruct((b, t, gw), BF16),
        grid=(b,),
        in_specs=[pl.BlockSpec((1, t, 4 * gw), lambda i: (i, 0, 0)),
                  pl.BlockSpec((1, t, 2 * gw), lambda i: (i, 0, 0)),
                  full((GDN_CONV, 3 * gw)), full((1, gw)), full((1, gw)), full((1, gw))],
        out_specs=pl.BlockSpec((1, t, gw), lambda i: (i, 0, 0)),
        scratch_shapes=[pltpu.VMEM((t, gw), BF16), pltpu.VMEM((t, gw), BF16), pltpu.VMEM((t, gw), BF16),
                        pltpu.VMEM((t, gw), F32), pltpu.VMEM((t, gw), BF16),
                        pltpu.VMEM((t // GDN_CHUNK, gw), F32), pltpu.VMEM((gw, gw), F32),
                        pltpu.VMEM((t, gw), F32)],
        compiler_params=pltpu.CompilerParams(dimension_semantics=("arbitrary",),
                                             vmem_limit_bytes=VMEM_LIMIT),
        name="gdn",
    )(p, ab, cw, alog_exp, dtb_exp, ng_exp)


def _hgrn_body(layer, p_ref, lbp_ref, ng_ref, o_ref, qdec_s, kdec_s, oi_s, gt_s, state_s, oraw_s):
    gw = GROUP_WIDTH
    t_len = p_ref.shape[1]
    c_len = HGRN_CHUNK
    tile = HGRN_TILE
    chunks_per_tile = tile // c_len

    lbp = lbp_ref[...]
    e = jnp.exp(lbp - jnp.max(lbp, axis=0, keepdims=True))
    sm = e / jnp.sum(e, axis=0, keepdims=True)
    if layer == 0:
        lb = jnp.zeros((1, gw), F32)
    else:
        lb = jnp.sum(sm[1:layer + 1], axis=0, keepdims=True)
    log_lb = jnp.log(lb)
    log_1m = jnp.log1p(-lb)

    r = _iota((tile, tile), 0)
    c = _iota((tile, tile), 1)
    same = (r >> HGRN_CHUNK_SHIFT) == (c >> HGRN_CHUNK_SHIFT)
    mid = ((r >> HGRN_CHUNK_SHIFT) << HGRN_CHUNK_SHIFT) + c_len // 2
    causal = same & (c <= r)
    m_cum = jnp.where(causal, 1.0, 0.0)
    m_mid = (jnp.where(same & (c > mid) & (c <= r), 1.0, 0.0)
             - jnp.where(same & (c > r) & (c <= mid), 1.0, 0.0))
    m_rest = jnp.where(same & (c > r), 1.0, 0.0)
    m_tot = jnp.where((_iota((chunks_per_tile, tile), 1) >> HGRN_CHUNK_SHIFT)
                      == _iota((chunks_per_tile, tile), 0), 1.0, 0.0)
    mats = jnp.concatenate([m_cum, m_mid, m_rest, m_tot], axis=0).astype(BF16)
    lane_head = _iota((1, gw), 1) >> HEAD_SHIFT
    bd = _head_block_mask(gw, gw)
    ones_bd = jnp.where(bd, 1.0, 0.0).astype(BF16)

    group = HGRN_GROUP

    def pre_stages(t):
        tiles = []
        for j in range(group):
            rows = pl.ds(pl.multiple_of((t * group + j) * tile, tile), tile)
            q = p_ref[0, rows, 0:gw]
            f = p_ref[0, rows, gw:2 * gw]
            log_sig = jnp.minimum(f, 0.0) - _log1p_exp_neg_abs(f)
            b_ = log_1m + log_sig
            log_f = jnp.maximum(log_lb, b_) + _log1p_exp_neg_abs(log_lb - b_)
            tiles.append(dict(rows=rows, k=(1.0 - lb) * _sigmoid(-f), qd=_silu(q), f3=_split(log_f, 2),
                              vb=p_ref[0, rows, 2 * gw:3 * gw].astype(BF16)))
        yield
        for j, tl in enumerate(tiles):
            f3 = tl["f3"]
            g_all = _dot(mats, f3[0]) + _dot(mats, f3[1])
            g_mid = g_all[tile:2 * tile]
            tl["qa"] = tl["qd"] * jnp.exp(g_mid)
            tl["ka"] = (tl["k"] * jnp.exp(-g_mid)).astype(BF16)
            qdec_s[tl["rows"], :] = tl["qd"] * jnp.exp(g_all[0:tile])
            kdec_s[tl["rows"], :] = tl["k"] * jnp.exp(g_all[2 * tile:3 * tile])
            c0 = pl.multiple_of((t * group + j) * chunks_per_tile, chunks_per_tile)
            gt_s[pl.ds(c0, chunks_per_tile), :] = g_all[3 * tile:3 * tile + chunks_per_tile]
        yield
        scores = [[_dot_nt(jnp.where(lane_head == h, tl["qa"], 0.0).astype(BF16), tl["ka"])
                   for h in range(HEADS)] for tl in tiles]
        yield
        for tl, sc in zip(tiles, scores):
            oi = jnp.zeros((tile, gw), F32)
            for h in range(HEADS):
                a = jnp.where(causal, sc[h], 0.0).astype(BF16)
                oi = oi + jnp.where(lane_head == h, _dot(a, tl["vb"]), 0.0)
            oi_s[tl["rows"], :] = oi

    def seq_tile(t):
        finish(jnp.maximum(t - 1, 0))
        rows = pl.ds(pl.multiple_of(t * tile, tile), tile)
        kd = kdec_s[rows, :]
        qd = qdec_s[rows, :]
        vb = p_ref[0, rows, 2 * gw:3 * gw].astype(BF16)
        gt = gt_s[pl.ds(pl.multiple_of(t * chunks_per_tile, chunks_per_tile), chunks_per_tile), :]
        chunk = lambda x, ci: x[ci * c_len:(ci + 1) * c_len]
        log_b = [jnp.zeros((1, gw), F32)]
        for ci in range(chunks_per_tile):
            log_b.append(log_b[ci] + gt[ci:ci + 1, :])

        def keys_decayed_to(c):
            return jnp.concatenate([(chunk(kd, cp) * jnp.exp(log_b[c] - log_b[cp + 1])).astype(BF16)
                                    for cp in range(c)], axis=0)

        state_t = state_s[...]
        decay_rows = jnp.concatenate([jnp.broadcast_to(jnp.exp(log_b[ci]), (c_len, gw))
                                      for ci in range(chunks_per_tile)], axis=0)
        o = _dot_nt((qd * decay_rows).astype(BF16), state_t.astype(BF16)) + oi_s[rows, :]
        qb = qd.astype(BF16)
        later = range(1, chunks_per_tile)
        keys = [keys_decayed_to(ci) for ci in later]
        kv_t = _dot_tn(vb, keys_decayed_to(chunks_per_tile))
        q_heads = [jnp.concatenate([jnp.where(lane_head == h, chunk(qb, ci), jnp.zeros((), BF16))
                                    for h in range(HEADS)], axis=0) for ci in later]
        scs = [_dot_nt(qh, ks) for qh, ks in zip(q_heads, keys)]
        pvs = [_dot(sc.astype(BF16), vb[0:ci * c_len]) for sc, ci in zip(scs, later)]
        outs = [chunk(o, 0)]
        for pv, ci in zip(pvs, later):
            cross = jnp.zeros((c_len, gw), F32)
            for h in range(HEADS):
                cross = cross + jnp.where(lane_head == h, pv[h * c_len:(h + 1) * c_len], 0.0)
            outs.append(chunk(o, ci) + cross)
        oraw_s[rows, :] = jnp.concatenate(outs, axis=0)
        state_s[...] = state_t * jnp.exp(log_b[chunks_per_tile]) + jnp.where(bd, kv_t, 0.0)

    def finish(t):
        rows = pl.ds(pl.multiple_of(t * tile, tile), tile)
        o = oraw_s[rows, :]
        z = p_ref[0, rows, 3 * gw:4 * gw]
        ms = _head_sum(o * o, ones_bd) * (1.0 / HEAD_DIM)
        o_ref[0, rows, :] = (o * lax.rsqrt(ms + NORM_EPS) * ng_ref[...] * _silu(z)).astype(o_ref.dtype)

    def step(t, carry):
        stages = pre_stages(t)
        for j in range(group):
            seq_tile((t - 1) * group + j)
            if j == 0:
                next(stages)
        for _ in stages:
            pass
        return carry

    n_groups = t_len // (tile * group)
    state_s[...] = jnp.zeros_like(state_s)
    oraw_s[0:tile, :] = jnp.zeros((tile, gw), F32)
    for _ in pre_stages(0):
        pass
    lax.fori_loop(1, n_groups, step, 0)
    for j in range(group):
        seq_tile((n_groups - 1) * group + j)
    finish(t_len // tile - 1)


def _hgrn(layer, p, lbp, ng_exp):
    b, t, _ = p.shape
    gw = GROUP_WIDTH
    depth = lbp.shape[0]
    full = lambda shape: pl.BlockSpec(shape, lambda i: (0,) * len(shape))
    return pl.pallas_call(
        functools.partial(_hgrn_body, layer),
        out_shape=jax.ShapeDtypeStruct((b, t, gw), BF16),
        grid=(b,),
        in_specs=[pl.BlockSpec((1, t, 4 * gw), lambda i: (i, 0, 0)), full((depth, gw)), full((1, gw))],
        out_specs=pl.BlockSpec((1, t, gw), lambda i: (i, 0, 0)),
        scratch_shapes=[pltpu.VMEM((t, gw), F32), pltpu.VMEM((t, gw), F32), pltpu.VMEM((t, gw), F32),
                        pltpu.VMEM((t // HGRN_CHUNK, gw), F32), pltpu.VMEM((gw, gw), F32),
                        pltpu.VMEM((t, gw), F32)],
        compiler_params=pltpu.CompilerParams(dimension_semantics=("arbitrary",),
                                             vmem_limit_bytes=VMEM_LIMIT),
        name="hgrn2",
    )(p, lbp, ng_exp)


VT_ROWS = HEAD_DIM + 16
LOG2E = 1.4426950408889634


def _eye_bf16(n):
    return jnp.where(_iota((n, n), 0) == _iota((n, n), 1), 1.0, 0.0).astype(BF16)


def _attn_setup(kz_ref, qvt_ref, kb_s, vta_s, bias_s, slopes):
    gw = GROUP_WIDTH
    t_len = kz_ref.shape[1]
    tile = ATTN_TILE
    ones_rows = jnp.ones((VT_ROWS - HEAD_DIM, tile), BF16)

    def body(n, carry):
        rows = pl.ds(pl.multiple_of(n * tile, tile), tile)
        kb_s[rows, :] = kz_ref[0, rows, 0:gw].astype(BF16)
        for h in range(HEADS):
            vta_s[n, h, 0:HEAD_DIM, :] = qvt_ref[n, gw + h * HEAD_DIM:gw + (h + 1) * HEAD_DIM, :].astype(BF16)
            vta_s[n, h, HEAD_DIM:VT_ROWS, :] = ones_rows
        return carry

    lax.fori_loop(0, t_len // tile, body, 0)
    key_local = _iota((tile, tile), 0).astype(F32)
    for h in range(HEADS):
        bias_s[h] = key_local * (slopes[h] * LOG2E)


def _flash_t(qt_s, heads, slopes, kb_s, vta_s, bias_s, ts_s, acc_s, sel_row):
    tile = ATTN_TILE
    half = tile // 2
    n = len(heads)
    causal = _iota((tile, tile), 0) <= _iota((tile, tile), 1)

    def scores(b, slot):
        k_b = kb_s[pl.ds(pl.multiple_of(b * tile, tile), tile), :]
        for c in range(n):
            ts_s[slot][c] = _dot(k_b, qt_s[c]) + bias_s[heads[c]]

    def run(i):
        return _flash_run(i, n, heads, slopes, vta_s, ts_s, acc_s, sel_row, scores, causal)

    return (lambda: scores(0, 0)), run


def _flash_run(i, n, heads, slopes, vta_s, ts_s, acc_s, sel_row, scores, causal):
    tile = ATTN_TILE
    half = tile // 2

    def absorb(b, slot, ms, own):
        dist = jnp.full((1, half), (i - b) * tile, jnp.int32).astype(F32)
        new_ms = []
        for c in range(n):
            off = dist * (-slopes[heads[c]] * LOG2E)
            alphas, p_halves = [], []
            for hf in range(2):
                lanes = slice(hf * half, (hf + 1) * half)
                t = ts_s[slot][c, :, lanes]
                if own:
                    t = jnp.where(causal[:, lanes], t, NEG_INF)
                col_max = jnp.max(t, axis=0, keepdims=True)
                sel = None if own else sel_row(c, b, hf)
                if sel is not None:
                    col_max = jnp.where(sel > 0.5, col_max, NEG_INF)
                m_old = ms[2 * c + hf]
                m_new = jnp.maximum(m_old, col_max + off)
                sub = m_new - off
                if sel is not None:
                    sub = jnp.where(sel > 0.5, sub, float("inf"))
                new_ms.append(m_new)
                alphas.append(jnp.exp2(m_old - m_new))
                p_halves.append(jnp.exp2(t - sub).astype(BF16))
            pv = _dot(vta_s[b, heads[c]], jnp.concatenate(p_halves, axis=1))
            for hf in range(2):
                lanes = slice(hf * half, (hf + 1) * half)
                acc_s[c, :, lanes] = alphas[hf] * acc_s[c, :, lanes] + pv[:, lanes]
        return tuple(new_ms)

    def pair(p, ms):
        b = 2 * p
        scores(b + 1, 1)
        ms = absorb(b, 0, ms, own=False)
        scores(b + 2, 0)
        return absorb(b + 1, 1, ms, own=False)

    def odd_tail(r, ms):
        scores(i, 1)
        ms = absorb(i - 1, 0, ms, own=False)
        return absorb(i, 1, ms, own=True)

    def even_tail(r, ms):
        return absorb(i, 0, ms, own=True)

    acc_s[...] = jnp.zeros_like(acc_s)
    ms = tuple(jnp.full((1, half), -1e30, F32) for _ in range(2 * n))
    ms = lax.fori_loop(0, i >> 1, pair, ms)
    ms = lax.fori_loop(0, i & 1, odd_tail, ms)
    lax.fori_loop(0, 1 - (i & 1), even_tail, ms)
    return [acc_s[c] for c in range(n)]


def _moba_body(kz_ref, qvt_ref, o_ref, kb_s, vta_s, bias_s, ts_a, ts_b, qt_s, acc_s, km_s, sel_s):
    ts_s = (ts_a, ts_b)
    gw = GROUP_WIDTH
    t_len = kz_ref.shape[1]
    tile = ATTN_TILE
    nb = t_len // MOBA_BLOCK
    topk = min(MOBA_TOPK, nb)
    slopes = [2.0 ** -(2 * h + 2) for h in range(HEADS)]

    _attn_setup(kz_ref, qvt_ref, kb_s, vta_s, bias_s, slopes)

    lane_head = _iota((1, gw), 1) >> HEAD_SHIFT

    def means(n, carry):
        rows = pl.ds(pl.multiple_of(n * MOBA_BLOCK, MOBA_BLOCK), MOBA_BLOCK)
        mean = jnp.sum(kz_ref[0, rows, 0:gw], axis=0, keepdims=True) * (1.0 / MOBA_BLOCK)
        for h in range(HEADS):
            km_s[pl.ds(h * nb + n, 1), :] = jnp.where(lane_head == h, mean, 0.0)
        return carry

    lax.fori_loop(0, nb, means, 0)
    km2 = _split(km_s[...], 2)
    eye = _eye_bf16(gw)
    row_head = _iota((gw, 1), 0) >> HEAD_SHIFT
    n_idx = _iota((nb, tile), 0)

    first_scores, run = _flash_t(qt_s, list(range(HEADS)), slopes, kb_s, vta_s, bias_s, ts_s, acc_s,
                                 lambda c, j, hf: sel_s[hf, pl.ds(c * nb + j, 1), :])

    def prologue(i):
        q_t = qvt_ref[i, 0:gw, :]
        q2 = _split(q_t, 2)
        gate = _dot(km2[0], q2[0]) + (_dot(km2[0], q2[1]) + _dot(km2[1], q2[0]))
        past_blk = n_idx < i
        for h in range(HEADS):
            gm = jnp.where(past_blk, gate[h * nb:(h + 1) * nb], NEG_INF)
            cnt = jnp.zeros((nb, tile), F32)
            for n2 in range(nb):
                other = gm[n2:n2 + 1, :]
                beats = (other > gm) | ((other == gm) & (n2 < n_idx))
                cnt = cnt + jnp.where(beats, 1.0, 0.0)
            sel = jnp.where(past_blk & (cnt < topk), 1.0, 0.0)
            for hf in range(2):
                sel_s[hf, h * nb:(h + 1) * nb, :] = sel[:, hf * (tile // 2):(hf + 1) * (tile // 2)]

        q_b = (q_t * (HEAD_DIM ** -0.5 * LOG2E)).astype(BF16)
        for h in range(HEADS):
            qt_s[h] = jnp.where(row_head == h, q_b, jnp.zeros((), BF16))
        first_scores()

    n_tiles = t_len // tile

    def qtile(i, carry):
        accs = run(i)
        rows_q = pl.ds(pl.multiple_of(i * tile, tile), tile)
        o_t = jnp.concatenate([a[0:HEAD_DIM] * (1.0 / a[HEAD_DIM:HEAD_DIM + 1]) for a in accs], axis=0)
        out = _dot_nt(eye, o_t.astype(BF16))
        z = kz_ref[0, rows_q, gw:2 * gw]
        o_ref[0, rows_q, :] = (out * _silu(z)).astype(o_ref.dtype)
        prologue(jnp.minimum(i + 1, n_tiles - 1))
        return carry

    prologue(0)
    lax.fori_loop(0, n_tiles, qtile, 0)


def _attn_in_specs(t):
    gw = GROUP_WIDTH
    return [pl.BlockSpec((1, t, 2 * gw), lambda i: (i, 0, 0)),
            pl.BlockSpec((t // ATTN_TILE, 2 * gw, ATTN_TILE), lambda i: (i, 0, 0))]


def _moba(kz, qvt):
    b, t, _ = kz.shape
    gw = GROUP_WIDTH
    assert t % MOBA_BLOCK == 0 and MOBA_BLOCK == ATTN_TILE
    return pl.pallas_call(
        _moba_body,
        out_shape=jax.ShapeDtypeStruct((b, t, gw), BF16),
        grid=(b,),
        in_specs=_attn_in_specs(t),
        out_specs=pl.BlockSpec((1, t, gw), lambda i: (i, 0, 0)),
        scratch_shapes=[pltpu.VMEM((t, gw), BF16),
                        pltpu.VMEM((t // ATTN_TILE, HEADS, VT_ROWS, ATTN_TILE), BF16),
                        pltpu.VMEM((HEADS, ATTN_TILE, ATTN_TILE), F32),
                        pltpu.VMEM((HEADS, ATTN_TILE, ATTN_TILE), F32),
                        pltpu.VMEM((HEADS, ATTN_TILE, ATTN_TILE), F32),
                        pltpu.VMEM((HEADS, gw, ATTN_TILE), BF16),
                        pltpu.VMEM((HEADS, VT_ROWS, ATTN_TILE), F32),
                        pltpu.VMEM((HEADS * (t // MOBA_BLOCK), gw), F32),
                        pltpu.VMEM((2, HEADS * (t // MOBA_BLOCK), ATTN_TILE // 2), F32)],
        compiler_params=pltpu.CompilerParams(dimension_semantics=("arbitrary",),
                                             vmem_limit_bytes=VMEM_LIMIT),
        name="moba",
    )(kz, qvt)


def _diff_body(layer, kz_ref, qvt_ref, lq1_ref, lk1_ref, lq2_ref, lk2_ref, ng_ref, o_ref, kb_s, vta_s, bias_s,
               ts_a, ts_b, qt_s, acc_s):
    ts_s = (ts_a, ts_b)
    gw = GROUP_WIDTH
    t_len = kz_ref.shape[1]
    tile = ATTN_TILE
    lam_init = 0.8 - 0.6 * math.exp(-0.3 * layer)
    lam = (jnp.exp(jnp.sum(lq1_ref[...] * lk1_ref[...], axis=-1, keepdims=True))
           - jnp.exp(jnp.sum(lq2_ref[...] * lk2_ref[...], axis=-1, keepdims=True)) + lam_init)
    slopes = [2.0 ** -(2 * h + 1) for h in range(HEADS)]

    _attn_setup(kz_ref, qvt_ref, kb_s, vta_s, bias_s, slopes)
    row_map = _iota((gw, 1), 0) >> (HEAD_SHIFT - 1)
    eye = _eye_bf16(gw)

    streams = [(h, mp) for h in range(HEADS) for mp in range(2)]
    first_scores, run = _flash_t(qt_s, [h for h, _ in streams], slopes, kb_s, vta_s, bias_s, ts_s, acc_s,
                                 lambda c, j, hf: None)

    def prologue(i):
        q_t = (qvt_ref[i, 0:gw, :] * (DIFF_DIM ** -0.5 * LOG2E)).astype(BF16)
        for c, (h, mp) in enumerate(streams):
            qt_s[c] = jnp.where(row_map == 2 * h + mp, q_t, jnp.zeros((), BF16))
        first_scores()

    n_tiles = t_len // tile

    def qtile(i, carry):
        accs = run(i)
        rows_q = pl.ds(pl.multiple_of(i * tile, tile), tile)
        o_m = [a[0:HEAD_DIM] * (1.0 / a[HEAD_DIM:HEAD_DIM + 1]) for a in accs]
        ys = []
        for h in range(HEADS):
            o_h = o_m[2 * h] - lam * o_m[2 * h + 1]
            ms = jnp.mean(o_h * o_h, axis=0, keepdims=True)
            ys.append(o_h * lax.rsqrt(ms + NORM_EPS))
        out = _dot_nt(eye, jnp.concatenate(ys, axis=0).astype(BF16))
        z = kz_ref[0, rows_q, gw:2 * gw]
        o_ref[0, rows_q, :] = (out * ng_ref[...] * (1.0 - lam_init) * _silu(z)).astype(o_ref.dtype)
        prologue(jnp.minimum(i + 1, n_tiles - 1))
        return carry

    prologue(0)
    lax.fori_loop(0, n_tiles, qtile, 0)


def _diff(layer, kz, qvt, lq1, lk1, lq2, lk2, ng_exp):
    b, t, _ = kz.shape
    gw = GROUP_WIDTH
    assert t % ATTN_TILE == 0
    full = lambda shape: pl.BlockSpec(shape, lambda i: (0,) * len(shape))
    return pl.pallas_call(
        functools.partial(_diff_body, layer),
        out_shape=jax.ShapeDtypeStruct((b, t, gw), BF16),
        grid=(b,),
        in_specs=_attn_in_specs(t) + [full((1, DIFF_DIM))] * 4 + [full((1, gw))],
        out_specs=pl.BlockSpec((1, t, gw), lambda i: (i, 0, 0)),
        scratch_shapes=[pltpu.VMEM((t, gw), BF16),
                        pltpu.VMEM((t // ATTN_TILE, HEADS, VT_ROWS, ATTN_TILE), BF16),
                        pltpu.VMEM((HEADS, ATTN_TILE, ATTN_TILE), F32),
                        pltpu.VMEM((2 * HEADS, ATTN_TILE, ATTN_TILE), F32),
                        pltpu.VMEM((2 * HEADS, ATTN_TILE, ATTN_TILE), F32),
                        pltpu.VMEM((2 * HEADS, gw, ATTN_TILE), BF16),
                        pltpu.VMEM((2 * HEADS, VT_ROWS, ATTN_TILE), F32)],
        compiler_params=pltpu.CompilerParams(dimension_semantics=("arbitrary",),
                                             vmem_limit_bytes=VMEM_LIMIT),
        name="diffattn",
    )(kz, qvt, lq1, lk1, lq2, lk2, ng_exp)


def kernel(x, pre_norm_g, post_norm_g, w_in, conv_w, gdn_a_log, gdn_dt_bias, gdn_norm_g,
           hgrn_lb, hgrn_norm_g, diff_lq1, diff_lk1, diff_lq2, diff_lk2, diff_norm_g, w_out):
    b, t, d = x.shape
    depth = w_in.shape[0]
    gw = GROUP_WIDTH
    x2 = x.astype(F32).reshape(b * t, d)
    row = lambda a: a.astype(F32).reshape(1, -1)
    per_head = lambda a: jnp.repeat(a.astype(F32), HEAD_DIM).reshape(1, gw)
    per_dim = lambda a: jnp.tile(a.astype(F32), HEADS).reshape(1, gw)
    lbp = hgrn_lb.astype(F32)
    for l in range(depth):
        p_gdn, p_ab, p_hgrn, kz_moba, kz_diff, qvt_moba, qvt_diff = _inproj(
            x2, row(pre_norm_g[l]), *_reorder_w_in(w_in[l]))
        shp = lambda a: a.reshape(b, t, a.shape[-1])
        o_gdn = _gdn(shp(p_gdn), shp(p_ab), conv_w[l].astype(F32), per_head(gdn_a_log[l]),
                     per_head(gdn_dt_bias[l]), per_dim(gdn_norm_g[l]))
        o_hgrn = _hgrn(l, shp(p_hgrn), lbp, per_dim(hgrn_norm_g[l]))
        o_moba = _moba(shp(kz_moba), qvt_moba)
        o_diff = _diff(l, shp(kz_diff), qvt_diff, row(diff_lq1[l]), row(diff_lk1[l]), row(diff_lq2[l]),
                       row(diff_lk2[l]), per_dim(diff_norm_g[l]))
        os_ = [o.reshape(b * t, gw) for o in (o_gdn, o_hgrn, o_moba, o_diff)]
        x2 = _outproj(os_, x2, w_out[l].astype(BF16), row(post_norm_g[l]))
    return x2.reshape(b, t, d)
```

```python
import functools
import math

import jax
import jax.numpy as jnp
from jax import lax
from jax.experimental import pallas as pl
from jax.experimental.pallas import tpu as pltpu

F32 = jnp.float32
BF16 = jnp.bfloat16

HEADS = 4
HEAD_DIM = 64
GROUP_WIDTH = HEADS * HEAD_DIM
HEAD_SHIFT = 6
GDN_CONV = 4
GDN_CHUNK = 64
GDN_CHUNK_SHIFT = 6
GDN_GROUP = 4
HGRN_CHUNK = 16
HGRN_CHUNK_SHIFT = 4
HGRN_TILE = 128
HGRN_GROUP = 2
MOBA_BLOCK = 256
MOBA_TOPK = 3
DIFF_DIM = HEAD_DIM // 2
ATTN_TILE = 256
OUTPROJ_TILE = 2048
NORM_EPS = 1e-6
NEG_INF = float("-inf")

GDN_COLS = 4 * GROUP_WIDTH + 2 * HEADS
ROW_SEG_WIDTHS = (4 * GROUP_WIDTH, 2 * GROUP_WIDTH, 4 * GROUP_WIDTH, 2 * GROUP_WIDTH, 2 * GROUP_WIDTH)
COL_SEG_WIDTHS = (2 * GROUP_WIDTH, 2 * GROUP_WIDTH)
VMEM_LIMIT = 56 * 1024 * 1024


def _iota(shape, dim):
    return lax.broadcasted_iota(jnp.int32, shape, dim)


def _dot(a, b):
    return jnp.dot(a, b, preferred_element_type=F32)


def _dot_nt(a, b):
    return lax.dot_general(a, b, (((1,), (1,)), ((), ())), preferred_element_type=F32)


def _dot_tn(a, b):
    return lax.dot_general(a, b, (((0,), (0,)), ((), ())), preferred_element_type=F32)


def _split(x, n):
    parts = []
    r = x
    for i in range(n):
        p = r.astype(BF16)
        parts.append(p)
        if i + 1 < n:
            r = r - p.astype(F32)
    return parts


def _sigmoid(x):
    return 0.5 * jnp.tanh(0.5 * x) + 0.5


def _silu(x):
    return x * _sigmoid(x)


def _log1p_exp_neg_abs(x):
    return jnp.log(1.0 + jnp.exp(-jnp.abs(x)))


def _softplus(x):
    return jnp.maximum(x, 0.0) + _log1p_exp_neg_abs(x)


def _head_block_mask(rows, cols):
    return (_iota((rows, cols), 0) >> HEAD_SHIFT) == (_iota((rows, cols), 1) >> HEAD_SHIFT)


def _head_sum(x, ones_bd):
    return _dot(x.astype(BF16), ones_bd)


def _inproj_body(x_ref, g_ref, w_ref, wt_ref, *o_refs):
    _project(x_ref[...], g_ref, w_ref, wt_ref, o_refs)


def _project(x, g_ref, w_ref, wt_ref, o_refs):
    n_t = len(ROW_SEG_WIDTHS)
    ms = jnp.mean(x * x, axis=-1, keepdims=True)
    h = (x * lax.rsqrt(ms + NORM_EPS) * g_ref[...]).astype(BF16)
    off = 0
    for o in o_refs[:n_t]:
        n = o.shape[-1]
        o[...] = _dot(h, w_ref[:, off:off + n])
        off += n
    off = 0
    for o in o_refs[n_t:]:
        n = o.shape[1]
        o[0] = _dot_nt(wt_ref[off:off + n, :], h)
        off += n


def _inproj(x2, g, w, wt):
    m, d = x2.shape
    tm = ATTN_TILE
    assert w.shape[1] == sum(ROW_SEG_WIDTHS) and wt.shape[0] == sum(COL_SEG_WIDTHS) and m % tm == 0
    return pl.pallas_call(
        _inproj_body,
        out_shape=([jax.ShapeDtypeStruct((m, s), F32) for s in ROW_SEG_WIDTHS]
                   + [jax.ShapeDtypeStruct((m // tm, s, tm), F32) for s in COL_SEG_WIDTHS]),
        grid=(m // tm,),
        in_specs=[pl.BlockSpec((tm, d), lambda i: (i, 0)),
                  pl.BlockSpec((1, d), lambda i: (0, 0)),
                  pl.BlockSpec(w.shape, lambda i: (0, 0)),
                  pl.BlockSpec(wt.shape, lambda i: (0, 0))],
        out_specs=([pl.BlockSpec((tm, s), lambda i: (i, 0)) for s in ROW_SEG_WIDTHS]
                   + [pl.BlockSpec((1, s, tm), lambda i: (i, 0, 0)) for s in COL_SEG_WIDTHS]),
        compiler_params=pltpu.CompilerParams(dimension_semantics=("arbitrary",),
                                             vmem_limit_bytes=VMEM_LIMIT),
        name="inproj",
    )(x2, g, w, wt)


def _reorder_w_in(w):
    gw = GROUP_WIDTH
    q, k, v = w[:, 0:gw], w[:, gw:2 * gw], w[:, 2 * gw:3 * gw]
    a = w[:, 3 * gw:3 * gw + HEADS]
    b = w[:, 3 * gw + HEADS:3 * gw + 2 * HEADS]
    z = w[:, 3 * gw + 2 * HEADS:GDN_COLS]
    a_exp = jnp.repeat(a, HEAD_DIM, axis=1)
    b_exp = jnp.repeat(b, HEAD_DIM, axis=1)
    hgrn = w[:, GDN_COLS:GDN_COLS + 4 * gw]
    moba = w[:, GDN_COLS + 4 * gw:GDN_COLS + 8 * gw]
    diff = w[:, GDN_COLS + 8 * gw:GDN_COLS + 12 * gw]
    kz = lambda s: jnp.concatenate([s[:, gw:2 * gw], s[:, 3 * gw:4 * gw]], axis=1)
    qv = lambda s: jnp.concatenate([s[:, 0:gw], s[:, 2 * gw:3 * gw]], axis=1)
    rows = jnp.concatenate([q, k, v, z, a_exp, b_exp, hgrn, kz(moba), kz(diff)], axis=1).astype(BF16)
    cols = jnp.concatenate([qv(moba), qv(diff)], axis=1).T.astype(BF16)
    return rows, cols


def _outproj_body(o1, o2, o3, o4, x_ref, w_ref, g_ref, out_ref):
    gw = GROUP_WIDTH
    y = _dot(o1[...], w_ref[0:gw, :])
    y = y + _dot(o2[...], w_ref[gw:2 * gw, :])
    y = y + _dot(o3[...], w_ref[2 * gw:3 * gw, :])
    y = y + _dot(o4[...], w_ref[3 * gw:4 * gw, :])
    ms = jnp.mean(y * y, axis=-1, keepdims=True)
    out_ref[...] = x_ref[...] + y * lax.rsqrt(ms + NORM_EPS) * g_ref[...]


def _out_in_body(o1, o2, o3, o4, x_ref, wo_ref, go_ref, gi_ref, w_ref, wt_ref, out_ref, *p_refs):
    _outproj_body(o1, o2, o3, o4, x_ref, wo_ref, go_ref, out_ref)
    _project(out_ref[...], gi_ref, w_ref, wt_ref, p_refs)


def _out_in(os_, x2, wo, go, gi, w, wt):
    m, d = x2.shape
    gw = GROUP_WIDTH
    tm = ATTN_TILE
    assert m % tm == 0
    const = lambda a: pl.BlockSpec(a.shape, lambda i: (0,) * a.ndim)
    return pl.pallas_call(
        _out_in_body,
        out_shape=([jax.ShapeDtypeStruct((m, d), F32)]
                   + [jax.ShapeDtypeStruct((m, s), F32) for s in ROW_SEG_WIDTHS]
                   + [jax.ShapeDtypeStruct((m // tm, s, tm), F32) for s in COL_SEG_WIDTHS]),
        grid=(m // tm,),
        in_specs=[pl.BlockSpec((tm, gw), lambda i: (i, 0))] * 4 + [
            pl.BlockSpec((tm, d), lambda i: (i, 0)), const(wo), const(go), const(gi), const(w), const(wt)],
        out_specs=([pl.BlockSpec((tm, d), lambda i: (i, 0))]
                   + [pl.BlockSpec((tm, s), lambda i: (i, 0)) for s in ROW_SEG_WIDTHS]
                   + [pl.BlockSpec((1, s, tm), lambda i: (i, 0, 0)) for s in COL_SEG_WIDTHS]),
        compiler_params=pltpu.CompilerParams(dimension_semantics=("arbitrary",),
                                             vmem_limit_bytes=VMEM_LIMIT),
        name="outproj_inproj",
    )(*os_, x2, wo, go, gi, w, wt)


def _outproj(os_, x2, w, g):
    m, d = x2.shape
    gw = GROUP_WIDTH
    tm = min(OUTPROJ_TILE, m)
    assert m % tm == 0
    return pl.pallas_call(
        _outproj_body,
        out_shape=jax.ShapeDtypeStruct((m, d), F32),
        grid=(m // tm,),
        in_specs=[pl.BlockSpec((tm, gw), lambda i: (i, 0))] * 4 + [
            pl.BlockSpec((tm, d), lambda i: (i, 0)),
            pl.BlockSpec((d, d), lambda i: (0, 0)),
            pl.BlockSpec((1, d), lambda i: (0, 0))],
        out_specs=pl.BlockSpec((tm, d), lambda i: (i, 0)),
        compiler_params=pltpu.CompilerParams(dimension_semantics=("arbitrary",),
                                             vmem_limit_bytes=VMEM_LIMIT),
        name="outproj",
    )(*os_, x2, w, g)


def _gdn_body(p_ref, ab_ref, cw_ref, alog_ref, dtb_ref, ng_ref, o_ref,
              qdec_s, kdec_s, w_s, u_s, aqk_s, dl_s, state_s, oraw_s):
    gw = GROUP_WIDTH
    t_len = p_ref.shape[1]
    c_len = GDN_CHUNK
    n_chunks = t_len // c_len

    cw = cw_ref[...]
    a_neg = -jnp.exp(alog_ref[...])
    dtb = dtb_ref[...]
    row = _iota((c_len, gw), 0)
    s_idx = _iota((c_len, gw), 1) & (HEAD_DIM - 1)
    lower = s_idx <= row
    strict = s_idx < row
    eye = s_idx == row
    bd = _head_block_mask(gw, gw)
    ones_bd = jnp.where(bd, 1.0, 0.0).astype(BF16)
    row8 = _iota((8, 3 * gw), 0)

    def block_diag(x):
        return jnp.where(bd, jnp.concatenate([x] * HEADS, axis=0), jnp.zeros((), x.dtype))

    def mm_pairs(left, right):
        return _dot(left.astype(BF16), block_diag(right.astype(BF16)))

    group = GDN_GROUP
    g_rows = group * c_len
    r_g = _iota((g_rows, g_rows), 0)
    c_g = _iota((g_rows, g_rows), 1)
    tril_g = jnp.where((c_g <= r_g) & ((r_g >> GDN_CHUNK_SHIFT) == (c_g >> GDN_CHUNK_SHIFT)),
                       1.0, 0.0).astype(BF16)

    def pre_stages(t, tail, out):
        r0 = pl.multiple_of(t * g_rows, g_rows)
        x = p_ref[0, pl.ds(r0, g_rows), 0:3 * gw]
        out[0] = x[g_rows - 8:g_rows, :]
        acc = x * cw[GDN_CONV - 1:GDN_CONV, :]
        for j in range(GDN_CONV - 1):
            shift = GDN_CONV - 1 - j
            rolled = pltpu.roll(x, shift, axis=0)
            tail_r = pltpu.roll(tail, shift, axis=0)
            head_rows = jnp.where(row8 < shift, tail_r, rolled[0:8])
            shifted = jnp.concatenate([head_rows, rolled[8:]], axis=0)
            acc = acc + shifted * cw[j:j + 1, :]
        y = _silu(acc)
        q = y[:, 0:gw]
        k = y[:, gw:2 * gw]
        v = y[:, 2 * gw:3 * gw]
        q = q * lax.rsqrt(_head_sum(q * q, ones_bd) + NORM_EPS) * (HEAD_DIM ** -0.5)
        k = k * lax.rsqrt(_head_sum(k * k, ones_bd) + NORM_EPS)

        a = ab_ref[0, pl.ds(r0, g_rows), 0:gw]
        b = ab_ref[0, pl.ds(r0, g_rows), gw:2 * gw]
        g = a_neg * _softplus(a + dtb)
        beta = _sigmoid(b)
        g2 = _split(g, 2)
        gc = _dot(tril_g, g2[0]) + _dot(tril_g, g2[1])
        eg = jnp.exp(gc)
        kb = k * beta
        vb = (v * beta).astype(BF16)
        kbg = (kb * eg).astype(BF16)
        qdec_s[pl.ds(r0, g_rows), :] = (q * eg).astype(BF16)
        lhs = kb.astype(BF16), q.astype(BF16), k.astype(BF16)
        yield

        ms, ts = [], []
        for j in range(group):
            sl = slice(j * c_len, (j + 1) * c_len)
            gc_j = gc[sl]
            g_row = jnp.sum(jnp.where(eye, gc_j, 0.0), axis=0, keepdims=True)
            decay = jnp.exp(jnp.where(lower, gc_j - g_row, NEG_INF))
            g_last = gc_j[c_len - 1:c_len, :]
            sc = _dot_nt(jnp.concatenate([lhs[0][sl], lhs[1][sl]], axis=0), block_diag(lhs[2][sl]))
            m = -jnp.where(strict, sc[0:c_len] * decay, 0.0)
            ms.append(m)
            ts.append(jnp.where(eye, 1.0, 0.0) + m)
            rows = pl.ds(r0 + j * c_len, c_len)
            aqk_s[rows, :] = (sc[c_len:2 * c_len] * decay).astype(BF16)
            kdec_s[rows, :] = (k[sl] * jnp.exp(g_last - gc_j)).astype(BF16)
            dl_s[pl.ds(t * group + j, 1), :] = jnp.exp(g_last)
        yield

        for _ in range(5):
            ms = [mm_pairs(m, m) for m in ms]
            ts = [t_inv + mm_pairs(m, t_inv) for m, t_inv in zip(ms, ts)]
            yield
        for j in range(group):
            sl = slice(j * c_len, (j + 1) * c_len)
            rows = pl.ds(r0 + j * c_len, c_len)
            t_b = ts[j].astype(BF16)
            u_s[rows, :] = _dot(t_b, block_diag(vb[sl]))
            w_s[rows, :] = _dot(t_b, block_diag(kbg[sl])).astype(BF16)

    def seq_chunk(c):
        finish(jnp.maximum(c - 1, 0))
        r0 = pl.multiple_of(c * c_len, c_len)
        rows = pl.ds(r0, c_len)
        state = state_s[...]
        ws = _dot(jnp.concatenate([w_s[rows, :], qdec_s[rows, :]], axis=0), state.astype(BF16))
        v_new = (u_s[rows, :] - ws[0:c_len]).astype(BF16)
        oraw_s[rows, :] = ws[c_len:2 * c_len] + _dot(aqk_s[rows, :], block_diag(v_new))
        kv = _dot_tn(kdec_s[rows, :], v_new)
        state_s[...] = state * dl_s[pl.ds(c, 1), :] + jnp.where(bd, kv, 0.0)

    def finish(c):
        rows = pl.ds(pl.multiple_of(c * c_len, c_len), c_len)
        o = oraw_s[rows, :]
        z = p_ref[0, rows, 3 * gw:4 * gw]
        ms = _head_sum(o * o, ones_bd) * (1.0 / HEAD_DIM)
        o_ref[0, rows, :] = (o * lax.rsqrt(ms + NORM_EPS) * ng_ref[...] * _silu(z)).astype(o_ref.dtype)

    def step(t, tail):
        out = [None]
        stages = pre_stages(t, tail, out)
        for j in range(group):
            seq_chunk((t - 1) * group + j)
            next(stages)
        for _ in stages:
            pass
        return out[0]

    def pre_only(t, tail):
        out = [None]
        for _ in pre_stages(t, tail, out):
            pass
        return out[0]

    n_groups = n_chunks // group
    state_s[...] = jnp.zeros_like(state_s)
    oraw_s[0:c_len, :] = jnp.zeros((c_len, gw), F32)
    tail = pre_only(0, jnp.zeros((8, 3 * gw), F32))
    lax.fori_loop(1, n_groups, step, tail)
    for j in range(group):
        seq_chunk((n_groups - 1) * group + j)
    finish(n_chunks - 1)


def _gdn(p, ab, cw, alog_exp, dtb_exp, ng_exp):
    b, t, _ = p.shape
    gw = GROUP_WIDTH
    full = lambda shape: pl.BlockSpec(shape, lambda i: (0,) * len(shape))
    return pl.pallas_call(
        _gdn_body,
        out_shape=jax.ShapeDtypeStruct((b, t, gw), BF16),
        grid=(b,),
        in_specs=[pl.BlockSpec((1, t, 4 * gw), lambda i: (i, 0, 0)),
                  pl.BlockSpec((1, t, 2 * gw), lambda i: (i, 0, 0)),
                  full((GDN_CONV, 3 * gw)), full((1, gw)), full((1, gw)), full((1, gw))],
        out_specs=pl.BlockSpec((1, t, gw), lambda i: (i, 0, 0)),
        scratch_shapes=[pltpu.VMEM((t, gw), BF16), pltpu.VMEM((t, gw), BF16), pltpu.VMEM((t, gw), BF16),
                        pltpu.VMEM((t, gw), F32), pltpu.VMEM((t, gw), BF16),
                        pltpu.VMEM((t // GDN_CHUNK, gw), F32), pltpu.VMEM((gw, gw), F32),
                        pltpu.VMEM((t, gw), F32)],
        compiler_params=pltpu.CompilerParams(dimension_semantics=("arbitrary",),
                                             vmem_limit_bytes=VMEM_LIMIT),
        name="gdn",
    )(p, ab, cw, alog_exp, dtb_exp, ng_exp)


def _hgrn_body(layer, p_ref, lbp_ref, ng_ref, o_ref, qdec_s, kdec_s, oi_s, gt_s, state_s, oraw_s):
    gw = GROUP_WIDTH
    t_len = p_ref.shape[1]
    c_len = HGRN_CHUNK
    tile = HGRN_TILE
    chunks_per_tile = tile // c_len

    lbp = lbp_ref[...]
    e = jnp.exp(lbp - jnp.max(lbp, axis=0, keepdims=True))
    sm = e / jnp.sum(e, axis=0, keepdims=True)
    if layer == 0:
        lb = jnp.zeros((1, gw), F32)
    else:
        lb = jnp.sum(sm[1:layer + 1], axis=0, keepdims=True)
    log_lb = jnp.log(lb)
    log_1m = jnp.log1p(-lb)

    r = _iota((tile, tile), 0)
    c = _iota((tile, tile), 1)
    same = (r >> HGRN_CHUNK_SHIFT) == (c >> HGRN_CHUNK_SHIFT)
    mid = ((r >> HGRN_CHUNK_SHIFT) << HGRN_CHUNK_SHIFT) + c_len // 2
    causal = same & (c <= r)
    m_cum = jnp.where(causal, 1.0, 0.0)
    m_mid = (jnp.where(same & (c > mid) & (c <= r), 1.0, 0.0)
             - jnp.where(same & (c > r) & (c <= mid), 1.0, 0.0))
    m_rest = jnp.where(same & (c > r), 1.0, 0.0)
    m_tot = jnp.where((_iota((chunks_per_tile, tile), 1) >> HGRN_CHUNK_SHIFT)
                      == _iota((chunks_per_tile, tile), 0), 1.0, 0.0)
    mats = jnp.concatenate([m_cum, m_mid, m_rest, m_tot], axis=0).astype(BF16)
    lane_head = _iota((1, gw), 1) >> HEAD_SHIFT
    bd = _head_block_mask(gw, gw)
    ones_bd = jnp.where(bd, 1.0, 0.0).astype(BF16)

    group = HGRN_GROUP

    def pre_stages(t):
        tiles = []
        for j in range(group):
            rows = pl.ds(pl.multiple_of((t * group + j) * tile, tile), tile)
            q = p_ref[0, rows, 0:gw]
            f = p_ref[0, rows, gw:2 * gw]
            log_sig = jnp.minimum(f, 0.0) - _log1p_exp_neg_abs(f)
            b_ = log_1m + log_sig
            log_f = jnp.maximum(log_lb, b_) + _log1p_exp_neg_abs(log_lb - b_)
            tiles.append(dict(rows=rows, k=(1.0 - lb) * _sigmoid(-f), qd=_silu(q), f3=_split(log_f, 2),
                              vb=p_ref[0, rows, 2 * gw:3 * gw].astype(BF16)))
        yield
        for j, tl in enumerate(tiles):
            f3 = tl["f3"]
            g_all = _dot(mats, f3[0]) + _dot(mats, f3[1])
            g_mid = g_all[tile:2 * tile]
            tl["qa"] = tl["qd"] * jnp.exp(g_mid)
            tl["ka"] = (tl["k"] * jnp.exp(-g_mid)).astype(BF16)
            qdec_s[tl["rows"], :] = tl["qd"] * jnp.exp(g_all[0:tile])
            kdec_s[tl["rows"], :] = tl["k"] * jnp.exp(g_all[2 * tile:3 * tile])
            c0 = pl.multiple_of((t * group + j) * chunks_per_tile, chunks_per_tile)
            gt_s[pl.ds(c0, chunks_per_tile), :] = g_all[3 * tile:3 * tile + chunks_per_tile]
        yield
        scores = [[_dot_nt(jnp.where(lane_head == h, tl["qa"], 0.0).astype(BF16), tl["ka"])
                   for h in range(HEADS)] for tl in tiles]
        yield
        for tl, sc in zip(tiles, scores):
            oi = jnp.zeros((tile, gw), F32)
            for h in range(HEADS):
                a = jnp.where(causal, sc[h], 0.0).astype(BF16)
                oi = oi + jnp.where(lane_head == h, _dot(a, tl["vb"]), 0.0)
            oi_s[tl["rows"], :] = oi

    def seq_tile(t):
        finish(jnp.maximum(t - 1, 0))
        rows = pl.ds(pl.multiple_of(t * tile, tile), tile)
        kd = kdec_s[rows, :]
        qd = qdec_s[rows, :]
        vb = p_ref[0, rows, 2 * gw:3 * gw].astype(BF16)
        gt = gt_s[pl.ds(pl.multiple_of(t * chunks_per_tile, chunks_per_tile), chunks_per_tile), :]
        chunk = lambda x, ci: x[ci * c_len:(ci + 1) * c_len]
        log_b = [jnp.zeros((1, gw), F32)]
        for ci in range(chunks_per_tile):
            log_b.append(log_b[ci] + gt[ci:ci + 1, :])

        def keys_decayed_to(c):
            return jnp.concatenate([(chunk(kd, cp) * jnp.exp(log_b[c] - log_b[cp + 1])).astype(BF16)
                                    for cp in range(c)], axis=0)

        state_t = state_s[...]
        decay_rows = jnp.concatenate([jnp.broadcast_to(jnp.exp(log_b[ci]), (c_len, gw))
                                      for ci in range(chunks_per_tile)], axis=0)
        o = _dot_nt((qd * decay_rows).astype(BF16), state_t.astype(BF16)) + oi_s[rows, :]
        qb = qd.astype(BF16)
        later = range(1, chunks_per_tile)
        keys = [keys_decayed_to(ci) for ci in later]
        kv_t = _dot_tn(vb, keys_decayed_to(chunks_per_tile))
        q_heads = [jnp.concatenate([jnp.where(lane_head == h, chunk(qb, ci), jnp.zeros((), BF16))
                                    for h in range(HEADS)], axis=0) for ci in later]
        scs = [_dot_nt(qh, ks) for qh, ks in zip(q_heads, keys)]
        pvs = [_dot(sc.astype(BF16), vb[0:ci * c_len]) for sc, ci in zip(scs, later)]
        outs = [chunk(o, 0)]
        for pv, ci in zip(pvs, later):
            cross = jnp.zeros((c_len, gw), F32)
            for h in range(HEADS):
                cross = cross + jnp.where(lane_head == h, pv[h * c_len:(h + 1) * c_len], 0.0)
            outs.append(chunk(o, ci) + cross)
        oraw_s[rows, :] = jnp.concatenate(outs, axis=0)
        state_s[...] = state_t * jnp.exp(log_b[chunks_per_tile]) + jnp.where(bd, kv_t, 0.0)

    def finish(t):
        rows = pl.ds(pl.multiple_of(t * tile, tile), tile)
        o = oraw_s[rows, :]
        z = p_ref[0, rows, 3 * gw:4 * gw]
        ms = _head_sum(o * o, ones_bd) * (1.0 / HEAD_DIM)
        o_ref[0, rows, :] = (o * lax.rsqrt(ms + NORM_EPS) * ng_ref[...] * _silu(z)).astype(o_ref.dtype)

    def step(t, carry):
        stages = pre_stages(t)
        for j in range(group):
            seq_tile((t - 1) * group + j)
            if j == 0:
                next(stages)
        for _ in stages:
            pass
        return carry

    n_groups = t_len // (tile * group)
    state_s[...] = jnp.zeros_like(state_s)
    oraw_s[0:tile, :] = jnp.zeros((tile, gw), F32)
    for _ in pre_stages(0):
        pass
    lax.fori_loop(1, n_groups, step, 0)
    for j in range(group):
        seq_tile((n_groups - 1) * group + j)
    finish(t_len // tile - 1)


def _hgrn(layer, p, lbp, ng_exp):
    b, t, _ = p.shape
    gw = GROUP_WIDTH
    depth = lbp.shape[0]
    full = lambda shape: pl.BlockSpec(shape, lambda i: (0,) * len(shape))
    return pl.pallas_call(
        functools.partial(_hgrn_body, layer),
        out_shape=jax.ShapeDtypeStruct((b, t, gw), BF16),
        grid=(b,),
        in_specs=[pl.BlockSpec((1, t, 4 * gw), lambda i: (i, 0, 0)), full((depth, gw)), full((1, gw))],
        out_specs=pl.BlockSpec((1, t, gw), lambda i: (i, 0, 0)),
        scratch_shapes=[pltpu.VMEM((t, gw), F32), pltpu.VMEM((t, gw), F32), pltpu.VMEM((t, gw), F32),
                        pltpu.VMEM((t // HGRN_CHUNK, gw), F32), pltpu.VMEM((gw, gw), F32),
                        pltpu.VMEM((t, gw), F32)],
        compiler_params=pltpu.CompilerParams(dimension_semantics=("arbitrary",),
                                             vmem_limit_bytes=VMEM_LIMIT),
        name="hgrn2",
    )(p, lbp, ng_exp)


VT_ROWS = HEAD_DIM + 16
LOG2E = 1.4426950408889634


def _eye_bf16(n):
    return jnp.where(_iota((n, n), 0) == _iota((n, n), 1), 1.0, 0.0).astype(BF16)


def _attn_setup(kz_ref, qvt_ref, kb_s, vta_s, bias_s, slopes):
    gw = GROUP_WIDTH
    t_len = kz_ref.shape[1]
    tile = ATTN_TILE
    ones_rows = jnp.ones((VT_ROWS - HEAD_DIM, tile), BF16)

    def body(n, carry):
        rows = pl.ds(pl.multiple_of(n * tile, tile), tile)
        kb_s[rows, :] = kz_ref[0, rows, 0:gw].astype(BF16)
        for h in range(HEADS):
            vta_s[n, h, 0:HEAD_DIM, :] = qvt_ref[n, gw + h * HEAD_DIM:gw + (h + 1) * HEAD_DIM, :].astype(BF16)
            vta_s[n, h, HEAD_DIM:VT_ROWS, :] = ones_rows
        return carry

    lax.fori_loop(0, t_len // tile, body, 0)
    key_local = _iota((tile, tile), 0).astype(F32)
    for h in range(HEADS):
        bias_s[h] = key_local * (slopes[h] * LOG2E)


def _flash_t(qt_s, heads, slopes, kb_s, vta_s, bias_s, ts_s, acc_s, sel_row):
    tile = ATTN_TILE
    half = tile // 2
    n = len(heads)
    causal = _iota((tile, tile), 0) <= _iota((tile, tile), 1)

    def scores(b, slot):
        k_b = kb_s[pl.ds(pl.multiple_of(b * tile, tile), tile), :]
        for c in range(n):
            ts_s[slot][c] = _dot(k_b, qt_s[c]) + bias_s[heads[c]]

    def run(i):
        return _flash_run(i, n, heads, slopes, vta_s, ts_s, acc_s, sel_row, scores, causal)

    return (lambda: scores(0, 0)), run


def _flash_run(i, n, heads, slopes, vta_s, ts_s, acc_s, sel_row, scores, causal):
    tile = ATTN_TILE
    half = tile // 2

    def absorb(b, slot, ms, own):
        dist = jnp.full((1, half), (i - b) * tile, jnp.int32).astype(F32)
        new_ms = []
        for c in range(n):
            off = dist * (-slopes[heads[c]] * LOG2E)
            alphas, p_halves = [], []
            for hf in range(2):
                lanes = slice(hf * half, (hf + 1) * half)
                t = ts_s[slot][c, :, lanes]
                if own:
                    t = jnp.where(causal[:, lanes], t, NEG_INF)
                col_max = jnp.max(t, axis=0, keepdims=True)
                sel = None if own else sel_row(c, b, hf)
                if sel is not None:
                    col_max = jnp.where(sel > 0.5, col_max, NEG_INF)
                m_old = ms[2 * c + hf]
                m_new = jnp.maximum(m_old, col_max + off)
                sub = m_new - off
                if sel is not None:
                    sub = jnp.where(sel > 0.5, sub, float("inf"))
                new_ms.append(m_new)
                alphas.append(jnp.exp2(m_old - m_new))
                p_halves.append(jnp.exp2(t - sub).astype(BF16))
            pv = _dot(vta_s[b, heads[c]], jnp.concatenate(p_halves, axis=1))
            for hf in range(2):
                lanes = slice(hf * half, (hf + 1) * half)
                acc_s[c, :, lanes] = alphas[hf] * acc_s[c, :, lanes] + pv[:, lanes]
        return tuple(new_ms)

    def pair(p, ms):
        b = 2 * p
        scores(b + 1, 1)
        ms = absorb(b, 0, ms, own=False)
        scores(b + 2, 0)
        return absorb(b + 1, 1, ms, own=False)

    def odd_tail(r, ms):
        scores(i, 1)
        ms = absorb(i - 1, 0, ms, own=False)
        return absorb(i, 1, ms, own=True)

    def even_tail(r, ms):
        return absorb(i, 0, ms, own=True)

    acc_s[...] = jnp.zeros_like(acc_s)
    ms = tuple(jnp.full((1, half), -1e30, F32) for _ in range(2 * n))
    ms = lax.fori_loop(0, i >> 1, pair, ms)
    ms = lax.fori_loop(0, i & 1, odd_tail, ms)
    lax.fori_loop(0, 1 - (i & 1), even_tail, ms)
    return [acc_s[c] for c in range(n)]


def _moba_body(kz_ref, qvt_ref, o_ref, kb_s, vta_s, bias_s, ts_a, ts_b, qt_s, acc_s, km_s, sel_s):
    ts_s = (ts_a, ts_b)
    gw = GROUP_WIDTH
    t_len = kz_ref.shape[1]
    tile = ATTN_TILE
    nb = t_len // MOBA_BLOCK
    topk = min(MOBA_TOPK, nb)
    slopes = [2.0 ** -(2 * h + 2) for h in range(HEADS)]

    _attn_setup(kz_ref, qvt_ref, kb_s, vta_s, bias_s, slopes)

    lane_head = _iota((1, gw), 1) >> HEAD_SHIFT

    def means(n, carry):
        rows = pl.ds(pl.multiple_of(n * MOBA_BLOCK, MOBA_BLOCK), MOBA_BLOCK)
        mean = jnp.sum(kz_ref[0, rows, 0:gw], axis=0, keepdims=True) * (1.0 / MOBA_BLOCK)
        for h in range(HEADS):
            km_s[pl.ds(h * nb + n, 1), :] = jnp.where(lane_head == h, mean, 0.0)
        return carry

    lax.fori_loop(0, nb, means, 0)
    km2 = _split(km_s[...], 2)
    eye = _eye_bf16(gw)
    row_head = _iota((gw, 1), 0) >> HEAD_SHIFT
    n_idx = _iota((nb, tile), 0)

    first_scores, run = _flash_t(qt_s, list(range(HEADS)), slopes, kb_s, vta_s, bias_s, ts_s, acc_s,
                                 lambda c, j, hf: sel_s[hf, pl.ds(c * nb + j, 1), :])

    def prologue(i):
        q_t = qvt_ref[i, 0:gw, :]
        q2 = _split(q_t, 2)
        gate = _dot(km2[0], q2[0]) + (_dot(km2[0], q2[1]) + _dot(km2[1], q2[0]))
        past_blk = n_idx < i
        for h in range(HEADS):
            gm = jnp.where(past_blk, gate[h * nb:(h + 1) * nb], NEG_INF)
            cnt = jnp.zeros((nb, tile), F32)
            for n2 in range(nb):
                other = gm[n2:n2 + 1, :]
                beats = (other > gm) | ((other == gm) & (n2 < n_idx))
                cnt = cnt + jnp.where(beats, 1.0, 0.0)
            sel = jnp.where(past_blk & (cnt < topk), 1.0, 0.0)
            for hf in range(2):
                sel_s[hf, h * nb:(h + 1) * nb, :] = sel[:, hf * (tile // 2):(hf + 1) * (tile // 2)]

        q_b = (q_t * (HEAD_DIM ** -0.5 * LOG2E)).astype(BF16)
        for h in range(HEADS):
            qt_s[h] = jnp.where(row_head == h, q_b, jnp.zeros((), BF16))
        first_scores()

    n_tiles = t_len // tile

    def qtile(i, carry):
        accs = run(i)
        rows_q = pl.ds(pl.multiple_of(i * tile, tile), tile)
        o_t = jnp.concatenate([a[0:HEAD_DIM] * (1.0 / a[HEAD_DIM:HEAD_DIM + 1]) for a in accs], axis=0)
        out = _dot_nt(eye, o_t.astype(BF16))
        z = kz_ref[0, rows_q, gw:2 * gw]
        o_ref[0, rows_q, :] = (out * _silu(z)).astype(o_ref.dtype)
        prologue(jnp.minimum(i + 1, n_tiles - 1))
        return carry

    prologue(0)
    lax.fori_loop(0, n_tiles, qtile, 0)


def _attn_in_specs(t):
    gw = GROUP_WIDTH
    return [pl.BlockSpec((1, t, 2 * gw), lambda i: (i, 0, 0)),
            pl.BlockSpec((t // ATTN_TILE, 2 * gw, ATTN_TILE), lambda i: (i, 0, 0))]


def _moba(kz, qvt):
    b, t, _ = kz.shape
    gw = GROUP_WIDTH
    assert t % MOBA_BLOCK == 0 and MOBA_BLOCK == ATTN_TILE
    return pl.pallas_call(
        _moba_body,
        out_shape=jax.ShapeDtypeStruct((b, t, gw), BF16),
        grid=(b,),
        in_specs=_attn_in_specs(t),
        out_specs=pl.BlockSpec((1, t, gw), lambda i: (i, 0, 0)),
        scratch_shapes=[pltpu.VMEM((t, gw), BF16),
                        pltpu.VMEM((t // ATTN_TILE, HEADS, VT_ROWS, ATTN_TILE), BF16),
                        pltpu.VMEM((HEADS, ATTN_TILE, ATTN_TILE), F32),
                        pltpu.VMEM((HEADS, ATTN_TILE, ATTN_TILE), F32),
                        pltpu.VMEM((HEADS, ATTN_TILE, ATTN_TILE), F32),
                        pltpu.VMEM((HEADS, gw, ATTN_TILE), BF16),
                        pltpu.VMEM((HEADS, VT_ROWS, ATTN_TILE), F32),
                        pltpu.VMEM((HEADS * (t // MOBA_BLOCK), gw), F32),
                        pltpu.VMEM((2, HEADS * (t // MOBA_BLOCK), ATTN_TILE // 2), F32)],
        compiler_params=pltpu.CompilerParams(dimension_semantics=("arbitrary",),
                                             vmem_limit_bytes=VMEM_LIMIT),
        name="moba",
    )(kz, qvt)


def _diff_body(layer, kz_ref, qvt_ref, lq1_ref, lk1_ref, lq2_ref, lk2_ref, ng_ref, o_ref, kb_s, vta_s, bias_s,
               ts_a, ts_b, qt_s, acc_s):
    ts_s = (ts_a, ts_b)
    gw = GROUP_WIDTH
    t_len = kz_ref.shape[1]
    tile = ATTN_TILE
    lam_init = 0.8 - 0.6 * math.exp(-0.3 * layer)
    lam = (jnp.exp(jnp.sum(lq1_ref[...] * lk1_ref[...], axis=-1, keepdims=True))
           - jnp.exp(jnp.sum(lq2_ref[...] * lk2_ref[...], axis=-1, keepdims=True)) + lam_init)
    slopes = [2.0 ** -(2 * h + 1) for h in range(HEADS)]

    _attn_setup(kz_ref, qvt_ref, kb_s, vta_s, bias_s, slopes)
    row_map = _iota((gw, 1), 0) >> (HEAD_SHIFT - 1)
    eye = _eye_bf16(gw)

    streams = [(h, mp) for h in range(HEADS) for mp in range(2)]
    first_scores, run = _flash_t(qt_s, [h for h, _ in streams], slopes, kb_s, vta_s, bias_s, ts_s, acc_s,
                                 lambda c, j, hf: None)

    def prologue(i):
        q_t = (qvt_ref[i, 0:gw, :] * (DIFF_DIM ** -0.5 * LOG2E)).astype(BF16)
        for c, (h, mp) in enumerate(streams):
            qt_s[c] = jnp.where(row_map == 2 * h + mp, q_t, jnp.zeros((), BF16))
        first_scores()

    n_tiles = t_len // tile

    def qtile(i, carry):
        accs = run(i)
        rows_q = pl.ds(pl.multiple_of(i * tile, tile), tile)
        o_m = [a[0:HEAD_DIM] * (1.0 / a[HEAD_DIM:HEAD_DIM + 1]) for a in accs]
        ys = []
        for h in range(HEADS):
            o_h = o_m[2 * h] - lam * o_m[2 * h + 1]
            ms = jnp.mean(o_h * o_h, axis=0, keepdims=True)
            ys.append(o_h * lax.rsqrt(ms + NORM_EPS))
        out = _dot_nt(eye, jnp.concatenate(ys, axis=0).astype(BF16))
        z = kz_ref[0, rows_q, gw:2 * gw]
        o_ref[0, rows_q, :] = (out * ng_ref[...] * (1.0 - lam_init) * _silu(z)).astype(o_ref.dtype)
        prologue(jnp.minimum(i + 1, n_tiles - 1))
        return carry

    prologue(0)
    lax.fori_loop(0, n_tiles, qtile, 0)


def _diff(layer, kz, qvt, lq1, lk1, lq2, lk2, ng_exp):
    b, t, _ = kz.shape
    gw = GROUP_WIDTH
    assert t % ATTN_TILE == 0
    full = lambda shape: pl.BlockSpec(shape, lambda i: (0,) * len(shape))
    return pl.pallas_call(
        functools.partial(_diff_body, layer),
        out_shape=jax.ShapeDtypeStruct((b, t, gw), BF16),
        grid=(b,),
        in_specs=_attn_in_specs(t) + [full((1, DIFF_DIM))] * 4 + [full((1, gw))],
        out_specs=pl.BlockSpec((1, t, gw), lambda i: (i, 0, 0)),
        scratch_shapes=[pltpu.VMEM((t, gw), BF16),
                        pltpu.VMEM((t // ATTN_TILE, HEADS, VT_ROWS, ATTN_TILE), BF16),
                        pltpu.VMEM((HEADS, ATTN_TILE, ATTN_TILE), F32),
                        pltpu.VMEM((2 * HEADS, ATTN_TILE, ATTN_TILE), F32),
                        pltpu.VMEM((2 * HEADS, ATTN_TILE, ATTN_TILE), F32),
                        pltpu.VMEM((2 * HEADS, gw, ATTN_TILE), BF16),
                        pltpu.VMEM((2 * HEADS, VT_ROWS, ATTN_TILE), F32)],
        compiler_params=pltpu.CompilerParams(dimension_semantics=("arbitrary",),
                                             vmem_limit_bytes=VMEM_LIMIT),
        name="diffattn",
    )(kz, qvt, lq1, lk1, lq2, lk2, ng_exp)


def kernel(x, pre_norm_g, post_norm_g, w_in, conv_w, gdn_a_log, gdn_dt_bias, gdn_norm_g,
           hgrn_lb, hgrn_norm_g, diff_lq1, diff_lk1, diff_lq2, diff_lk2, diff_norm_g, w_out):
    b, t, d = x.shape
    depth = w_in.shape[0]
    gw = GROUP_WIDTH
    x2 = x.astype(F32).reshape(b * t, d)
    row = lambda a: a.astype(F32).reshape(1, -1)
    per_head = lambda a: jnp.repeat(a.astype(F32), HEAD_DIM).reshape(1, gw)
    per_dim = lambda a: jnp.tile(a.astype(F32), HEADS).reshape(1, gw)
    lbp = hgrn_lb.astype(F32)
    proj = _inproj(x2, row(pre_norm_g[0]), *_reorder_w_in(w_in[0]))
    for l in range(depth):
        p_gdn, p_ab, p_hgrn, kz_moba, kz_diff, qvt_moba, qvt_diff = proj
        shp = lambda a: a.reshape(b, t, a.shape[-1])
        o_gdn = _gdn(shp(p_gdn), shp(p_ab), conv_w[l].astype(F32), per_head(gdn_a_log[l]),
                     per_head(gdn_dt_bias[l]), per_dim(gdn_norm_g[l]))
        o_hgrn = _hgrn(l, shp(p_hgrn), lbp, per_dim(hgrn_norm_g[l]))
        o_moba = _moba(shp(kz_moba), qvt_moba)
        o_diff = _diff(l, shp(kz_diff), qvt_diff, row(diff_lq1[l]), row(diff_lk1[l]), row(diff_lq2[l]),
                       row(diff_lk2[l]), per_dim(diff_norm_g[l]))
        os_ = [o.reshape(b * t, gw) for o in (o_gdn, o_hgrn, o_moba, o_diff)]
        if l + 1 < depth:
            x2, *proj = _out_in(os_, x2, w_out[l].astype(BF16), row(post_norm_g[l]),
                                row(pre_norm_g[l + 1]), *_reorder_w_in(w_in[l + 1]))
        else:
            x2 = _outproj(os_, x2, w_out[l].astype(BF16), row(post_norm_g[l]))
    return x2.reshape(b, t, d)
```
